```python
import math
import jax, jax.numpy as jnp
from jax import lax
import numpy as np

D_MODEL = 1024
BATCH = 2
SEQ = 8192
DEPTH = 1

MEM_LEN = 256
EPS = 1e-6
LRU_WIDTH = 512
LRU_BLOCKS = 8
LRU_BLOCK = LRU_WIDTH // LRU_BLOCKS
CONV_WIDTH = 4
LRU_C = 8.0
MLA_HEADS = 8
QK_NOPE = 64
QK_ROPE = 32
QK_HEAD = QK_NOPE + QK_ROPE
V_DIM = 64
Q_LORA = 256
KV_LORA = 128
MLA_WIDTH = MLA_HEADS * V_DIM
ROPE_THETA = 10000.0
Q_BLOCK = 128
MIX_WIDTH = LRU_WIDTH + MLA_WIDTH
OFF_Y = LRU_WIDTH
OFF_CQ = 2 * LRU_WIDTH
OFF_CKV = OFF_CQ + Q_LORA
OFF_KR = OFF_CKV + KV_LORA
IN_COLS = OFF_KR + QK_ROPE
MEM_HEADS = 4
MEM_HEAD_DIM = 128
MEM_WIDTH = MEM_HEADS * MEM_HEAD_DIM
D_FF = 2816
FFN_CONV = 3

kernel_name = "hybrid_rglru_mla_memxattn_convffn_encoder"


def rms_norm(x, g):
    xf = x.astype(jnp.float32)
    y = xf * lax.rsqrt(jnp.mean(xf * xf, axis=-1, keepdims=True) + EPS)
    return (y * g.astype(jnp.float32)).astype(x.dtype)


def depthwise_conv(x, w, b, left, right):
    S = x.shape[1]
    xp = jnp.pad(x, ((0, 0), (left, right), (0, 0)))
    out = xp[:, 0:S] * w[0] + b
    for k in range(1, w.shape[0]):
        out = out + xp[:, k:k + S] * w[k]
    return out


def rope_tables(positions):
    inv = ROPE_THETA ** (-jnp.arange(0, QK_ROPE, 2, dtype=jnp.float32) / QK_ROPE)
    ang = positions.astype(jnp.float32)[..., None] * inv
    return jnp.cos(ang), jnp.sin(ang)


def apply_rope(t, cos, sin):
    half = QK_ROPE // 2
    c = cos[:, :, None, :].astype(t.dtype)
    s = sin[:, :, None, :].astype(t.dtype)
    t1, t2 = t[..., :half], t[..., half:]
    return jnp.concatenate([t1 * c - t2 * s, t1 * s + t2 * c], axis=-1)


def block_diag(x, w):
    B_, S_, _ = x.shape
    xb = x.reshape(B_, S_, LRU_BLOCKS, LRU_BLOCK)
    return jnp.einsum('bsnc,ncd->bsnd', xb, w).reshape(B_, S_, LRU_WIDTH)


def rg_lru(x, w_a, b_a, w_i, b_i, lam, reverse):
    r = jax.nn.sigmoid((block_diag(x, w_a) + b_a).astype(jnp.float32))
    i = jax.nn.sigmoid((block_diag(x, w_i) + b_i).astype(jnp.float32))
    log_a = -LRU_C * r * jax.nn.softplus(-lam.astype(jnp.float32))
    a = jnp.exp(log_a)
    mult = jnp.sqrt(-jnp.expm1(2.0 * log_a))
    b = mult * (i * x.astype(jnp.float32))

    def combine(lhs, rhs):
        a_l, b_l = lhs
        a_r, b_r = rhs
        return a_l * a_r, a_r * b_l + b_r

    _, h = lax.associative_scan(combine, (a, b), reverse=reverse, axis=1)
    return h


def mla_attention(proj, cos, sin, q_a_norm, w_uq, kv_a_norm, w_ukv, q_norm, k_norm):
    B_, S_, _ = proj.shape
    c_q = rms_norm(proj[..., OFF_CQ:OFF_CKV], q_a_norm)
    c_kv = rms_norm(proj[..., OFF_CKV:OFF_KR], kv_a_norm)
    k_rope = proj[..., OFF_KR:IN_COLS]
    q = (c_q @ w_uq).reshape(B_, S_, MLA_HEADS, QK_HEAD)
    kv = (c_kv @ w_ukv).reshape(B_, S_, MLA_HEADS, QK_NOPE + V_DIM)
    k_nope, v = kv[..., :QK_NOPE], kv[..., QK_NOPE:]
    k_rope_h = jnp.broadcast_to(k_rope[:, :, None, :], (B_, S_, MLA_HEADS, QK_ROPE))
    k = jnp.concatenate([k_nope, k_rope_h], axis=-1)
    q = rms_norm(q, q_norm)
    k = rms_norm(k, k_norm)
    q = jnp.concatenate([q[..., :QK_NOPE], apply_rope(q[..., QK_NOPE:], cos, sin)], axis=-1)
    k = jnp.concatenate([k[..., :QK_NOPE], apply_rope(k[..., QK_NOPE:], cos, sin)], axis=-1)
    scale = QK_HEAD ** -0.5
    kh = k.transpose(0, 2, 1, 3)
    vh = v.transpose(0, 2, 1, 3)
    nb = S_ // Q_BLOCK
    qb = q.transpose(0, 2, 1, 3).reshape(B_, MLA_HEADS, nb, Q_BLOCK, QK_HEAD).transpose(2, 0, 1, 3, 4)

    def attend(q_blk):
        s = jnp.einsum('bhqd,bhkd->bhqk', q_blk, kh).astype(jnp.float32) * scale
        p = jax.nn.softmax(s, axis=-1)
        return jnp.einsum('bhqk,bhkd->bhqd', p.astype(vh.dtype), vh)

    o = lax.map(attend, qb)
    return o.transpose(1, 0, 3, 2, 4).reshape(B_, S_, MLA_WIDTH)


def memory_cross_attention(h, mem_n, w_q, w_kv, q_norm, k_norm, w_o):
    B_, S_, _ = h.shape
    M = mem_n.shape[1]
    q = (h @ w_q).reshape(B_, S_, MEM_HEADS, MEM_HEAD_DIM)
    kv = mem_n @ w_kv
    k = kv[..., :MEM_WIDTH].reshape(B_, M, MEM_HEADS, MEM_HEAD_DIM)
    v = kv[..., MEM_WIDTH:].reshape(B_, M, MEM_HEADS, MEM_HEAD_DIM)
    q = rms_norm(q, q_norm)
    k = rms_norm(k, k_norm)
    s = jnp.einsum('bqhd,bkhd->bhqk', q, k).astype(jnp.float32) * (MEM_HEAD_DIM ** -0.5)
    p = jax.nn.softmax(s, axis=-1)
    o = jnp.einsum('bhqk,bkhd->bqhd', p.astype(v.dtype), v).reshape(B_, S_, MEM_WIDTH)
    return o @ w_o


def hybrid_layer(x, mem, cos, sin, attn_norm, w_in, lru_conv_w, lru_conv_b, lru_w_a, lru_b_a,
                 lru_w_i, lru_b_i, lru_lambda, q_a_norm, w_uq, kv_a_norm, w_ukv, mla_q_norm,
                 mla_k_norm, lru_out_norm, mla_out_norm, w_out, mem_attn_norm, mem_norm, w_mem_q,
                 w_mem_kv, mem_q_norm, mem_k_norm, w_mem_o, ffn_norm, w_up, ffn_conv_w, ffn_conv_b,
                 w_down):
    h = rms_norm(x, attn_norm)
    proj = h @ w_in
    xr = proj[..., :OFF_Y]
    yg = proj[..., OFF_Y:OFF_CQ]
    xf = depthwise_conv(xr, lru_conv_w[0], lru_conv_b[0], CONV_WIDTH - 1, 0)
    xb = depthwise_conv(xr, lru_conv_w[1], lru_conv_b[1], 0, CONV_WIDTH - 1)
    hf = rg_lru(xf, lru_w_a[0], lru_b_a[0], lru_w_i[0], lru_b_i[0], lru_lambda[0], False)
    hb = rg_lru(xb, lru_w_a[1], lru_b_a[1], lru_w_i[1], lru_b_i[1], lru_lambda[1], True)
    lru_out = (hf + hb).astype(x.dtype) * jax.nn.gelu(yg)
    mla_out = mla_attention(proj, cos, sin, q_a_norm, w_uq, kv_a_norm, w_ukv, mla_q_norm, mla_k_norm)
    mixed = jnp.concatenate([rms_norm(lru_out, lru_out_norm), rms_norm(mla_out, mla_out_norm)], axis=-1)
    x = x + mixed @ w_out
    x = x + memory_cross_attention(rms_norm(x, mem_attn_norm), rms_norm(mem, mem_norm),
                                   w_mem_q, w_mem_kv, mem_q_norm, mem_k_norm, w_mem_o)
    gu = rms_norm(x, ffn_norm) @ w_up
    gu = depthwise_conv(gu, ffn_conv_w, ffn_conv_b, FFN_CONV // 2, FFN_CONV // 2)
    g, u = gu[..., :D_FF], gu[..., D_FF:]
    x = x + (jax.nn.silu(g) * u) @ w_down
    return x


def setup_inputs(seed: int = 0) -> dict:
    key = jax.random.key(seed)
    ks = iter(jax.random.split(key, 40))
    f32 = jnp.float32

    def w(shape, fan_in):
        return jax.random.normal(next(ks), (DEPTH,) + shape, f32) * (fan_in ** -0.5)

    def gain(shape):
        return 1.0 + 0.05 * jax.random.normal(next(ks), (DEPTH,) + shape, f32)

    def bias(shape):
        return 0.01 * jax.random.normal(next(ks), (DEPTH,) + shape, f32)

    x = jax.random.normal(next(ks), (BATCH, SEQ, D_MODEL), f32)
    mem = jax.random.normal(next(ks), (BATCH, MEM_LEN, D_MODEL), f32)
    positions = jnp.broadcast_to(jnp.arange(SEQ, dtype=jnp.int32)[None, :], (BATCH, SEQ))
    u = jax.random.uniform(next(ks), (DEPTH, 2, LRU_WIDTH), f32, 0.9, 0.999)
    s = u ** (1.0 / LRU_C)
    lru_lambda = jnp.log(s) - jnp.log1p(-s)
    return {
        "x": x,
        "mem": mem,
        "positions": positions,
        "attn_norm": gain((D_MODEL,)),
        "w_in": w((D_MODEL, IN_COLS), D_MODEL),
        "lru_conv_w": w((2, CONV_WIDTH, LRU_WIDTH), CONV_WIDTH),
        "lru_conv_b": bias((2, LRU_WIDTH)),
        "lru_w_a": w((2, LRU_BLOCKS, LRU_BLOCK, LRU_BLOCK), LRU_BLOCK),
        "lru_b_a": bias((2, LRU_WIDTH)),
        "lru_w_i": w((2, LRU_BLOCKS, LRU_BLOCK, LRU_BLOCK), LRU_BLOCK),
        "lru_b_i": bias((2, LRU_WIDTH)),
        "lru_lambda": lru_lambda,
        "q_a_norm": gain((Q_LORA,)),
        "w_uq": w((Q_LORA, MLA_HEADS * QK_HEAD), Q_LORA),
        "kv_a_norm": gain((KV_LORA,)),
        "w_ukv": w((KV_LORA, MLA_HEADS * (QK_NOPE + V_DIM)), KV_LORA),
        "mla_q_norm": gain((QK_HEAD,)),
        "mla_k_norm": gain((QK_HEAD,)),
        "lru_out_norm": gain((LRU_WIDTH,)),
        "mla_out_norm": gain((MLA_WIDTH,)),
        "w_out": w((MIX_WIDTH, D_MODEL), MIX_WIDTH),
        "mem_attn_norm": gain((D_MODEL,)),
        "mem_norm": gain((D_MODEL,)),
        "w_mem_q": w((D_MODEL, MEM_WIDTH), D_MODEL),
        "w_mem_kv": w((D_MODEL, 2 * MEM_WIDTH), D_MODEL),
        "mem_q_norm": gain((MEM_HEAD_DIM,)),
        "mem_k_norm": gain((MEM_HEAD_DIM,)),
        "w_mem_o": w((MEM_WIDTH, D_MODEL), MEM_WIDTH),
        "ffn_norm": gain((D_MODEL,)),
        "w_up": w((D_MODEL, 2 * D_FF), D_MODEL),
        "ffn_conv_w": w((FFN_CONV, 2 * D_FF), FFN_CONV),
        "ffn_conv_b": bias((2 * D_FF,)),
        "w_down": w((D_FF, D_MODEL), D_FF),
    }


def reference(x, mem, positions, attn_norm, w_in, lru_conv_w, lru_conv_b, lru_w_a, lru_b_a,
              lru_w_i, lru_b_i, lru_lambda, q_a_norm, w_uq, kv_a_norm, w_ukv, mla_q_norm,
              mla_k_norm, lru_out_norm, mla_out_norm, w_out, mem_attn_norm, mem_norm, w_mem_q,
              w_mem_kv, mem_q_norm, mem_k_norm, w_mem_o, ffn_norm, w_up, ffn_conv_w, ffn_conv_b,
              w_down):
    cos, sin = rope_tables(positions)
    for l in range(DEPTH):
        x = hybrid_layer(x, mem, cos, sin, attn_norm[l], w_in[l], lru_conv_w[l], lru_conv_b[l],
                         lru_w_a[l], lru_b_a[l], lru_w_i[l], lru_b_i[l], lru_lambda[l],
                         q_a_norm[l], w_uq[l], kv_a_norm[l], w_ukv[l], mla_q_norm[l],
                         mla_k_norm[l], lru_out_norm[l], mla_out_norm[l], w_out[l],
                         mem_attn_norm[l], mem_norm[l], w_mem_q[l], w_mem_kv[l], mem_q_norm[l],
                         mem_k_norm[l], w_mem_o[l], ffn_norm[l], w_up[l], ffn_conv_w[l],
                         ffn_conv_b[l], w_down[l])
    return x
```

```python
import functools
import math

import jax
import jax.numpy as jnp
from jax import lax
from jax.experimental import pallas as pl
from jax.experimental.pallas import tpu as pltpu

F32 = jnp.float32
BF16 = jnp.bfloat16

EPS = 1e-6
LRU_WIDTH = 512
LRU_BLOCKS = 8
CONV_WIDTH = 4
LRU_C = 8.0
MLA_HEADS = 8
QK_NOPE = 64
QK_ROPE = 32
QK_HEAD = QK_NOPE + QK_ROPE
V_DIM = 64
Q_LORA = 256
KV_LORA = 128
MLA_WIDTH = MLA_HEADS * V_DIM
ROPE_THETA = 10000.0
MEM_HEADS = 4
MEM_HEAD_DIM = 128
MEM_WIDTH = MEM_HEADS * MEM_HEAD_DIM
FFN_CONV = 3

LANES = 128
SUBLANES = 8
HEAD_PAD = LANES

PROJ_ROWS = 512
LRU_ROWS = 256
ATTN_Q = 512
ATTN_K = 512
MIX_ROWS = 512
FFN_ROWS = 512
FFN_SPLIT = 2
HALO = SUBLANES

VMEM_LIMIT = 56 * 1024 * 1024


def _rms(x, g):
    return x * lax.rsqrt(jnp.mean(x * x, axis=-1, keepdims=True) + EPS) * g


def _dot(a, b):
    return jnp.dot(a, b, preferred_element_type=F32)


def _dot_nt(a, b):
    return lax.dot_general(a, b, (((1,), (1,)), ((), ())), preferred_element_type=F32)


def _proj_kernel(x_ref, pos_ref, an_ref, win_ref, qan_ref, kvan_ref, wuq_ref, wuk_ref, wuv_ref,
                 gq_ref, gk_ref, inv_ref, sgn_ref,
                 xr_ref, yg_ref, q_ref, k_ref, v_ref):
    x = x_ref[0]
    h = _rms(x, an_ref[...]).astype(BF16)
    proj = _dot(h, win_ref[...])
    xr_ref[0] = proj[:, 0:LRU_WIDTH]
    yg_ref[0] = proj[:, LRU_WIDTH:2 * LRU_WIDTH]
    o = 2 * LRU_WIDTH
    c_q = _rms(proj[:, o:o + Q_LORA], qan_ref[...]).astype(BF16)
    o += Q_LORA
    c_kv = _rms(proj[:, o:o + KV_LORA], kvan_ref[...]).astype(BF16)
    o += KV_LORA
    kr = proj[:, o:o + HEAD_PAD]

    ang = pos_ref[0].astype(F32) * inv_ref[...]
    cos = jnp.cos(ang)
    sin = jnp.sin(ang) * sgn_ref[...]
    lane = lax.broadcasted_iota(jnp.int32, cos.shape, 1)
    first_half = lane < QK_NOPE + QK_ROPE // 2
    half = QK_ROPE // 2

    def norm_rope(t, g):
        ss = jnp.sum(t * t, axis=-1, keepdims=True) * (1.0 / QK_HEAD)
        tn = t * lax.rsqrt(ss + EPS) * g
        partner = jnp.where(first_half, pltpu.roll(tn, HEAD_PAD - half, 1), pltpu.roll(tn, half, 1))
        return tn * cos + partner * sin

    scale = QK_HEAD ** -0.5
    for hh in range(MLA_HEADS):
        qh = norm_rope(_dot(c_q, wuq_ref[hh]), gq_ref[...])
        q_ref[0, hh] = (qh * scale).astype(BF16)
        kh = norm_rope(_dot(c_kv, wuk_ref[hh]) + kr, gk_ref[...])
        k_ref[0, hh] = kh.astype(BF16)
        v_ref[0, hh] = _dot(c_kv, wuv_ref[hh]).astype(BF16)


def _proj_call(x, pos3, an, win, qan, kvan, wuq, wuk, wuv, gq, gk, inv, sgn):
    B, S, D = x.shape
    T = PROJ_ROWS
    H = MLA_HEADS
    full = lambda a: pl.BlockSpec(a.shape, lambda b, i: (0,) * a.ndim)
    row = lambda w: pl.BlockSpec((1, T, w), lambda b, i: (b, i, 0))
    head = pl.BlockSpec((1, H, T, HEAD_PAD), lambda b, i: (b, 0, i, 0))
    return pl.pallas_call(
        _proj_kernel,
        grid=(B, S // T),
        in_specs=[row(D), row(1)] + [full(a) for a in (an, win, qan, kvan, wuq, wuk, wuv, gq, gk, inv, sgn)],
        out_specs=[row(LRU_WIDTH), row(LRU_WIDTH), head, head, head],
        out_shape=[jax.ShapeDtypeStruct((B, S, LRU_WIDTH), F32),
                   jax.ShapeDtypeStruct((B, S, LRU_WIDTH), F32),
                   jax.ShapeDtypeStruct((B, H, S, HEAD_PAD), BF16),
                   jax.ShapeDtypeStruct((B, H, S, HEAD_PAD), BF16),
                   jax.ShapeDtypeStruct((B, H, S, HEAD_PAD), BF16)],
        compiler_params=pltpu.CompilerParams(
            dimension_semantics=("parallel", "parallel"), vmem_limit_bytes=VMEM_LIMIT),
        name="proj",
    )(x, pos3, an, win, qan, kvan, wuq, wuk, wuv, gq, gk, inv, sgn)


def _lru_kernel(x_ref, cw_ref, cb_ref, wa_ref, ba_ref, wi_ref, bi_ref, lam_ref,
                h_ref, ext_ref, halo_ref, carry_ref):
    T = LRU_ROWS
    d = pl.program_id(1)
    i = pl.program_id(2)

    @pl.when(i == 0)
    def _():
        halo_ref[...] = jnp.zeros_like(halo_ref)
        carry_ref[...] = jnp.zeros_like(carry_ref)

    def run(reverse):
        x = x_ref[0]
        if not reverse:
            ext_ref[0:HALO] = halo_ref[...]
            ext_ref[HALO:HALO + T] = x
            halo_ref[...] = x[T - HALO:T]
            offs = tuple(HALO - (CONV_WIDTH - 1) + k for k in range(CONV_WIDTH))
        else:
            ext_ref[0:T] = x
            ext_ref[T:T + HALO] = halo_ref[...]
            halo_ref[...] = x[0:HALO]
            offs = tuple(range(CONV_WIDTH))
        cw = cw_ref[0]
        xc = cb_ref[0] + cw[0:1] * ext_ref[pl.ds(offs[0], T), :]
        for k in range(1, CONV_WIDTH):
            xc = xc + cw[k:k + 1] * ext_ref[pl.ds(offs[k], T), :]

        xb = xc.astype(BF16)
        r = jax.nn.sigmoid(_dot(xb, wa_ref[0]) + ba_ref[0])
        ig = jax.nn.sigmoid(_dot(xb, wi_ref[0]) + bi_ref[0])
        z = -lam_ref[0]
        sp = jnp.maximum(z, 0.0) + jnp.log1p(jnp.exp(-jnp.abs(z)))
        log_a = (-LRU_C) * r * sp
        a = jnp.exp(log_a)
        mult = jnp.sqrt(1.0 - a * a)
        bb = mult * (ig * xc)

        rowi = lax.broadcasted_iota(jnp.int32, a.shape, 0)
        s = 1
        while s < T:
            if not reverse:
                keep = rowi >= s
                a_sh = jnp.where(keep, pltpu.roll(a, s, 0), 1.0)
                b_sh = jnp.where(keep, pltpu.roll(bb, s, 0), 0.0)
            else:
                keep = rowi < T - s
                a_sh = jnp.where(keep, pltpu.roll(a, T - s, 0), 1.0)
                b_sh = jnp.where(keep, pltpu.roll(bb, T - s, 0), 0.0)
            bb = a * b_sh + bb
            a = a * a_sh
            s *= 2
        hprev = carry_ref[0:1, :]
        hcur = a * hprev + bb
        h_ref[0, 0] = hcur
        edge = hcur[0:1, :] if reverse else hcur[T - 1:T, :]
        carry_ref[...] = jnp.broadcast_to(edge, carry_ref.shape)

    @pl.when(d == 0)
    def _():
        run(False)

    @pl.when(d == 1)
    def _():
        run(True)


def _lru_call(xr, cw, cb, wa, ba, wi, bi, lam):
    B, S, W = xr.shape
    T = LRU_ROWS
    n = S // T
    tile = lambda d, i: i + d * (n - 1 - 2 * i)
    par = lambda a: pl.BlockSpec((1,) + a.shape[1:], lambda b, d, i: (d,) + (0,) * (a.ndim - 1))
    return pl.pallas_call(
        _lru_kernel,
        grid=(B, 2, n),
        in_specs=[pl.BlockSpec((1, T, W), lambda b, d, i: (b, tile(d, i), 0))]
        + [par(a) for a in (cw, cb, wa, ba, wi, bi, lam)],
        out_specs=pl.BlockSpec((1, 1, T, W), lambda b, d, i: (d, b, tile(d, i), 0)),
        out_shape=jax.ShapeDtypeStruct((2, B, S, W), F32),
        scratch_shapes=[pltpu.VMEM((T + HALO, W), F32), pltpu.VMEM((HALO, W), F32),
                        pltpu.VMEM((SUBLANES, W), F32)],
        compiler_params=pltpu.CompilerParams(
            dimension_semantics=("parallel", "arbitrary", "arbitrary"), vmem_limit_bytes=VMEM_LIMIT),
        name="lru",
    )(xr, cw, cb, wa, ba, wi, bi, lam)


def _attn_kernel(q_ref, k_ref, v_ref, o_ref):
    TQ, TK = ATTN_Q, ATTN_K
    S = k_ref.shape[2]
    out = None
    for hh in range(2):
        q = q_ref[0, hh]

        def body(j, carry, hh=hh, q=q):
            m, l, acc = carry
            start = pl.multiple_of(j * TK, TK)
            kc = k_ref[0, hh, pl.ds(start, TK), :]
            vc = v_ref[0, hh, pl.ds(start, TK), :]
            s = _dot_nt(q, kc)
            m_new = jnp.maximum(m, jnp.max(s, axis=-1, keepdims=True))
            alpha = jnp.exp(m - m_new)
            p = jnp.exp(s - m_new)
            l = alpha * l + jnp.sum(p, axis=-1, keepdims=True)
            acc = alpha * acc + _dot(p.astype(BF16), vc)
            return m_new, l, acc

        init = (jnp.full((TQ, 1), -jnp.inf, F32), jnp.zeros((TQ, 1), F32), jnp.zeros((TQ, HEAD_PAD), F32))
        m, l, acc = lax.fori_loop(0, S // TK, body, init)
        oh = acc / l
        out = oh if out is None else out + oh
    o_ref[0] = out


def _attn_call(q, k, v):
    B, H, S, _ = q.shape
    TQ = ATTN_Q
    return pl.pallas_call(
        _attn_kernel,
        grid=(B, H // 2, S // TQ),
        in_specs=[pl.BlockSpec((1, 2, TQ, HEAD_PAD), lambda b, h, i: (b, h, i, 0)),
                  pl.BlockSpec((1, 2, S, HEAD_PAD), lambda b, h, i: (b, h, 0, 0)),
                  pl.BlockSpec((1, 2, S, HEAD_PAD), lambda b, h, i: (b, h, 0, 0))],
        out_specs=pl.BlockSpec((1, TQ, 2 * V_DIM), lambda b, h, i: (b, i, h)),
        out_shape=jax.ShapeDtypeStruct((B, S, MLA_WIDTH), F32),
        compiler_params=pltpu.CompilerParams(
            dimension_semantics=("parallel", "parallel", "arbitrary"), vmem_limit_bytes=VMEM_LIMIT),
        name="attn",
    )(q, k, v)


def _memkv_kernel(mem_ref, mn_ref, wkv_ref, gk_ref, k_ref, v_ref):
    mn = _rms(mem_ref[0], mn_ref[...]).astype(BF16)
    kv = _dot(mn, wkv_ref[...])
    for hh in range(MEM_HEADS):
        kh = kv[:, hh * MEM_HEAD_DIM:(hh + 1) * MEM_HEAD_DIM]
        k_ref[0, hh] = _rms(kh, gk_ref[...]).astype(BF16)
        v_ref[0, hh] = kv[:, MEM_WIDTH + hh * MEM_HEAD_DIM:MEM_WIDTH + (hh + 1) * MEM_HEAD_DIM].astype(BF16)


def _memkv_call(mem, mn, wkv, gk):
    B, M, D = mem.shape
    full = lambda a: pl.BlockSpec(a.shape, lambda b: (0,) * a.ndim)
    hb = pl.BlockSpec((1, MEM_HEADS, M, MEM_HEAD_DIM), lambda b: (b, 0, 0, 0))
    return pl.pallas_call(
        _memkv_kernel,
        grid=(B,),
        in_specs=[pl.BlockSpec((1, M, D), lambda b: (b, 0, 0)), full(mn), full(wkv), full(gk)],
        out_specs=[hb, hb],
        out_shape=[jax.ShapeDtypeStruct((B, MEM_HEADS, M, MEM_HEAD_DIM), BF16)] * 2,
        compiler_params=pltpu.CompilerParams(
            dimension_semantics=("parallel",), vmem_limit_bytes=VMEM_LIMIT),
        name="memkv",
    )(mem, mn, wkv, gk)


def _gelu_tanh(x):
    c = math.sqrt(2.0 / math.pi)
    return x * (0.5 * (1.0 + jnp.tanh(c * (x + 0.044715 * (x * x * x)))))


def _mix_kernel(x_ref, hf_ref, hb_ref, yg_ref, mla_ref, ln_ref, mnn_ref, wol_ref, wom_ref,
                man_ref, wmq_ref, gmq_ref, km_ref, vm_ref, wmo_ref, o_ref):
    x = x_ref[0]
    lru = (hf_ref[0, 0] + hb_ref[0, 0]) * _gelu_tanh(yg_ref[0])
    lru_n = _rms(lru, ln_ref[...]).astype(BF16)
    mla_n = _rms(mla_ref[0], mnn_ref[...]).astype(BF16)
    x1 = x + _dot(lru_n, wol_ref[...]) + _dot(mla_n, wom_ref[...])

    hq = _rms(x1, man_ref[...]).astype(BF16)
    qm = _dot(hq, wmq_ref[...])
    scale = MEM_HEAD_DIM ** -0.5
    outs = []
    for hh in range(MEM_HEADS):
        qh = _rms(qm[:, hh * MEM_HEAD_DIM:(hh + 1) * MEM_HEAD_DIM], gmq_ref[...]) * scale
        s = _dot_nt(qh.astype(BF16), km_ref[0, hh])
        p = jnp.exp(s - jnp.max(s, axis=-1, keepdims=True))
        l = jnp.sum(p, axis=-1, keepdims=True)
        outs.append(_dot(p.astype(BF16), vm_ref[0, hh]) / l)
    om = jnp.concatenate(outs, axis=-1).astype(BF16)
    o_ref[0] = x1 + _dot(om, wmo_ref[...])


def _mix_call(x, h2, yg, mla, ln, mnn, wol, wom, man, wmq, gmq, km, vm, wmo):
    B, S, D = x.shape
    T = MIX_ROWS
    M = km.shape[2]
    full = lambda a: pl.BlockSpec(a.shape, lambda b, i: (0,) * a.ndim)
    row = lambda w: pl.BlockSpec((1, T, w), lambda b, i: (b, i, 0))
    hspec = lambda d: pl.BlockSpec((1, 1, T, LRU_WIDTH), lambda b, i: (d, b, i, 0))
    mspec = pl.BlockSpec((1, MEM_HEADS, M, MEM_HEAD_DIM), lambda b, i: (b, 0, 0, 0))
    return pl.pallas_call(
        _mix_kernel,
        grid=(B, S // T),
        in_specs=[row(D), hspec(0), hspec(1), row(LRU_WIDTH), row(MLA_WIDTH), full(ln), full(mnn),
                  full(wol), full(wom), full(man), full(wmq), full(gmq), mspec, mspec, full(wmo)],
        out_specs=row(D),
        out_shape=jax.ShapeDtypeStruct((B, S, D), F32),
        compiler_params=pltpu.CompilerParams(
            dimension_semantics=("parallel", "parallel"), vmem_limit_bytes=VMEM_LIMIT),
        name="mix",
    )(x, h2, h2, yg, mla, ln, mnn, wol, wom, man, wmq, gmq, km, vm, wmo)


def _ffn_kernel(x_ref, xp_ref, xn_ref, fn_ref, wg_ref, wu_ref, cwg_ref, cwu_ref, cbg_ref, cbu_ref, wd_ref,
                o_ref, xe_ref, g_ref, u_ref, acc_ref):
    T = FFN_ROWS
    i = pl.program_id(1)
    j = pl.program_id(2)

    @pl.when(j == 0)
    def _():
        prev = jnp.where(i == 0, 0.0, xp_ref[0])
        nxt = jnp.where(i == pl.num_programs(1) - 1, 0.0, xn_ref[0])
        g = fn_ref[...]
        xe_ref[0:HALO] = _rms(prev, g).astype(BF16)
        xe_ref[HALO:HALO + T] = _rms(x_ref[0], g).astype(BF16)
        xe_ref[HALO + T:HALO + T + HALO] = _rms(nxt, g).astype(BF16)

    xe = xe_ref[...]
    g_ref[...] = _dot(xe, wg_ref[...])
    u_ref[...] = _dot(xe, wu_ref[...])

    def conv(ref, cw, cb):
        out = cb + cw[0:1] * ref[pl.ds(HALO - 1, T), :]
        out = out + cw[1:2] * ref[pl.ds(HALO, T), :]
        return out + cw[2:3] * ref[pl.ds(HALO + 1, T), :]

    gc = conv(g_ref, cwg_ref[...], cbg_ref[...])
    uc = conv(u_ref, cwu_ref[...], cbu_ref[...])
    act = (gc * jax.nn.sigmoid(gc) * uc).astype(BF16)
    part = _dot(act, wd_ref[...])

    @pl.when(j == 0)
    def _():
        acc_ref[...] = part

    @pl.when(j > 0)
    def _():
        acc_ref[...] += part

    @pl.when(j == pl.num_programs(2) - 1)
    def _():
        o_ref[0] = x_ref[0] + acc_ref[...]


def _ffn_call(x, fn, wup, cw, cb, wd):
    B, S, D = x.shape
    T = FFN_ROWS
    dff = wd.shape[0]
    nf = FFN_SPLIT
    tf = dff // nf
    hb = T // HALO
    last = S // HALO - 1
    return pl.pallas_call(
        _ffn_kernel,
        grid=(B, S // T, nf),
        in_specs=[pl.BlockSpec((1, T, D), lambda b, i, j: (b, i, 0)),
                  pl.BlockSpec((1, HALO, D), lambda b, i, j: (b, jnp.maximum(i * hb - 1, 0), 0)),
                  pl.BlockSpec((1, HALO, D), lambda b, i, j: (b, jnp.minimum((i + 1) * hb, last), 0)),
                  pl.BlockSpec(fn.shape, lambda b, i, j: (0, 0)),
                  pl.BlockSpec((D, tf), lambda b, i, j: (0, j)),
                  pl.BlockSpec((D, tf), lambda b, i, j: (0, j + nf)),
                  pl.BlockSpec((FFN_CONV, tf), lambda b, i, j: (0, j)),
                  pl.BlockSpec((FFN_CONV, tf), lambda b, i, j: (0, j + nf)),
                  pl.BlockSpec((1, tf), lambda b, i, j: (0, j)),
                  pl.BlockSpec((1, tf), lambda b, i, j: (0, j + nf)),
                  pl.BlockSpec((tf, D), lambda b, i, j: (j, 0))],
        out_specs=pl.BlockSpec((1, T, D), lambda b, i, j: (b, i, 0)),
        out_shape=jax.ShapeDtypeStruct((B, S, D), F32),
        scratch_shapes=[pltpu.VMEM((T + 2 * HALO, D), BF16),
                        pltpu.VMEM((T + 2 * HALO, tf), F32),
                        pltpu.VMEM((T + 2 * HALO, tf), F32),
                        pltpu.VMEM((T, D), F32)],
        compiler_params=pltpu.CompilerParams(
            dimension_semantics=("parallel", "parallel", "arbitrary"), vmem_limit_bytes=VMEM_LIMIT),
        name="ffn",
    )(x, x, x, fn, wup, wup, cw, cw, cb, cb, wd)


def _block_diag_dense(w):
    nb, bs, _ = w.shape
    eye = jnp.eye(nb, dtype=w.dtype)
    return (eye[:, None, :, None] * w[:, :, None, :]).reshape(nb * bs, nb * bs)


def _layer(x, mem, pos3, attn_norm, w_in, lru_conv_w, lru_conv_b, lru_w_a, lru_b_a, lru_w_i, lru_b_i,
           lru_lambda, q_a_norm, w_uq, kv_a_norm, w_ukv, mla_q_norm, mla_k_norm, lru_out_norm,
           mla_out_norm, w_out, mem_attn_norm, mem_norm, w_mem_q, w_mem_kv, mem_q_norm, mem_k_norm,
           w_mem_o, ffn_norm, w_up, ffn_conv_w, ffn_conv_b, w_down):
    D = x.shape[-1]
    H = MLA_HEADS
    row = lambda a: a.reshape(1, -1)
    off_kr = 2 * LRU_WIDTH + Q_LORA + KV_LORA
    pad_cols = lambda a, lo, hi: jnp.pad(a, ((0, 0),) * (a.ndim - 1) + ((lo, hi),))

    win = jnp.concatenate([w_in[:, :off_kr], pad_cols(w_in[:, off_kr:], QK_NOPE, HEAD_PAD - QK_HEAD)],
                          axis=1).astype(BF16)
    wuq = pad_cols(w_uq.reshape(Q_LORA, H, QK_HEAD).transpose(1, 0, 2), 0, HEAD_PAD - QK_HEAD).astype(BF16)
    wkv = w_ukv.reshape(KV_LORA, H, QK_NOPE + V_DIM).transpose(1, 0, 2)
    wuk = pad_cols(wkv[:, :, :QK_NOPE], 0, HEAD_PAD - QK_NOPE).astype(BF16)
    wv = wkv[:, :, QK_NOPE:]
    odd = (jnp.arange(H) % 2 == 1)[:, None, None]
    wuv = jnp.where(odd, pad_cols(wv, V_DIM, 0), pad_cols(wv, 0, V_DIM)).astype(BF16)
    gq = pad_cols(row(mla_q_norm), 0, HEAD_PAD - QK_HEAD)
    gk = pad_cols(row(mla_k_norm), 0, HEAD_PAD - QK_HEAD)
    inv = ROPE_THETA ** (-jnp.arange(0, QK_ROPE, 2, dtype=F32) / QK_ROPE)
    inv128 = pad_cols(row(jnp.concatenate([inv, inv])), QK_NOPE, HEAD_PAD - QK_HEAD)
    half = QK_ROPE // 2
    sgn128 = pad_cols(row(jnp.concatenate([-jnp.ones((half,), F32), jnp.ones((half,), F32)])),
                      QK_NOPE, HEAD_PAD - QK_HEAD)

    xr, yg, q, k, v = _proj_call(x, pos3, row(attn_norm), win, row(q_a_norm), row(kv_a_norm),
                                 wuq, wuk, wuv, gq, gk, inv128, sgn128)

    wa = jax.vmap(_block_diag_dense)(lru_w_a).astype(BF16)
    wi = jax.vmap(_block_diag_dense)(lru_w_i).astype(BF16)
    r3 = lambda a: a.reshape(2, 1, LRU_WIDTH)
    h2 = _lru_call(xr, lru_conv_w, r3(lru_conv_b), wa, r3(lru_b_a), wi, r3(lru_b_i), r3(lru_lambda))

    mla = _attn_call(q, k, v)
    km, vm = _memkv_call(mem, row(mem_norm), w_mem_kv.astype(BF16), row(mem_k_norm))

    wo = w_out.astype(BF16)
    x2 = _mix_call(x, h2, yg, mla, row(lru_out_norm), row(mla_out_norm), wo[:LRU_WIDTH], wo[LRU_WIDTH:],
                   row(mem_attn_norm), w_mem_q.astype(BF16), row(mem_q_norm), km, vm, w_mem_o.astype(BF16))

    return _ffn_call(x2, row(ffn_norm), w_up.astype(BF16), ffn_conv_w, row(ffn_conv_b), w_down.astype(BF16))


def kernel(x, mem, positions, attn_norm, w_in, lru_conv_w, lru_conv_b, lru_w_a, lru_b_a, lru_w_i, lru_b_i, lru_lambda, q_a_norm, w_uq, kv_a_norm, w_ukv, mla_q_norm, mla_k_norm, lru_out_norm, mla_out_norm, w_out, mem_attn_norm, mem_norm, w_mem_q, w_mem_kv, mem_q_norm, mem_k_norm, w_mem_o, ffn_norm, w_up, ffn_conv_w, ffn_conv_b, w_down):
    pos3 = positions[..., None]
    params = (attn_norm, w_in, lru_conv_w, lru_conv_b, lru_w_a, lru_b_a, lru_w_i, lru_b_i, lru_lambda,
              q_a_norm, w_uq, kv_a_norm, w_ukv, mla_q_norm, mla_k_norm, lru_out_norm, mla_out_norm, w_out,
              mem_attn_norm, mem_norm, w_mem_q, w_mem_kv, mem_q_norm, mem_k_norm, w_mem_o, ffn_norm, w_up,
              ffn_conv_w, ffn_conv_b, w_down)
    for l in range(attn_norm.shape[0]):
        x = _layer(x, mem, pos3, *[p[l] for p in params])
    return x
```

```python
import functools
import math

import jax
import jax.numpy as jnp
from jax import lax
from jax.experimental import pallas as pl
from jax.experimental.pallas import tpu as pltpu

F32 = jnp.float32
BF16 = jnp.bfloat16

EPS = 1e-6
LRU_WIDTH = 512
LRU_BLOCKS = 8
CONV_WIDTH = 4
LRU_C = 8.0
MLA_HEADS = 8
QK_NOPE = 64
QK_ROPE = 32
QK_HEAD = QK_NOPE + QK_ROPE
V_DIM = 64
Q_LORA = 256
KV_LORA = 128
MLA_WIDTH = MLA_HEADS * V_DIM
ROPE_THETA = 10000.0
MEM_HEADS = 4
MEM_HEAD_DIM = 128
MEM_WIDTH = MEM_HEADS * MEM_HEAD_DIM
FFN_CONV = 3

LANES = 128
SUBLANES = 8
HEAD_PAD = LANES
BF16_ROWS = 16
V_ROWS = V_DIM + BF16_ROWS
LOG2E = 1.4426950408889634

PROJ_ROWS = 512
LRU_ROWS = 256
ATTN_Q = 512
ATTN_K = 512
MIX_ROWS = 512
FFN_ROWS = 512
FFN_SPLIT = 2
HALO = SUBLANES

VMEM_LIMIT = 56 * 1024 * 1024


def _rms(x, g):
    return x * lax.rsqrt(jnp.mean(x * x, axis=-1, keepdims=True) + EPS) * g


def _dot(a, b):
    return jnp.dot(a, b, preferred_element_type=F32)


def _dot_nt(a, b):
    return lax.dot_general(a, b, (((1,), (1,)), ((), ())), preferred_element_type=F32)


def _proj_kernel(x_ref, pos_ref, an_ref, win_ref, qan_ref, kvan_ref, wuq_ref, wuk_ref, wuv_ref,
                 gq_ref, gk_ref, inv_ref,
                 xr_ref, yg_ref, q_ref, k_ref, v_ref):
    x = x_ref[0]
    h = _rms(x, an_ref[...]).astype(BF16)
    proj = _dot(h, win_ref[...])
    xr_ref[0] = proj[:, 0:LRU_WIDTH]
    yg_ref[0] = proj[:, LRU_WIDTH:2 * LRU_WIDTH]
    o = 2 * LRU_WIDTH
    cq_f = _rms(proj[:, o:o + Q_LORA], qan_ref[...])
    o += Q_LORA
    ckv_f = _rms(proj[:, o:o + KV_LORA], kvan_ref[...])
    o += KV_LORA
    kr = proj[:, o:o + HEAD_PAD]
    cq_t = cq_f.T.astype(BF16)
    ckv_t = ckv_f.T.astype(BF16)
    c_kv = ckv_f.astype(BF16)
    T = x.shape[0]
    half = QK_ROPE // 2
    r0, r1, r2 = QK_NOPE, QK_NOPE + half, QK_HEAD

    ang = inv_ref[...] * pos_ref[0].astype(F32)
    cos_t = jnp.cos(ang)
    sin_t = jnp.sin(ang)
    ones = jnp.ones((r0, T), F32)
    zpad = jnp.zeros((HEAD_PAD - r2, T), F32)
    cos_rm = jnp.concatenate([ones, cos_t, cos_t, zpad], axis=0).T
    sin_rm = jnp.concatenate([0.0 * ones, -sin_t, sin_t, zpad], axis=0).T
    lane = lax.broadcasted_iota(jnp.int32, cos_rm.shape, 1)
    first_half = lane < r1

    vrow = lax.broadcasted_iota(jnp.int32, (V_ROWS, T), 0)
    qscale = (QK_HEAD ** -0.5) * LOG2E

    for hh in range(MLA_HEADS):
        qt = _dot(wuq_ref[hh], cq_t)
        ss = jnp.sum(qt * qt, axis=0, keepdims=True) * (1.0 / QK_HEAD)
        tn = qt * lax.rsqrt(ss + EPS) * gq_ref[...]
        t1, t2 = tn[r0:r1], tn[r1:r2]
        qo = jnp.concatenate([tn[0:r0], t1 * cos_t - t2 * sin_t, t1 * sin_t + t2 * cos_t, tn[r2:]], axis=0)
        q_ref[0, hh] = (qo * qscale).astype(BF16)
        kt = _dot(c_kv, wuk_ref[hh]) + kr
        ks = jnp.sum(kt * kt, axis=-1, keepdims=True) * (1.0 / QK_HEAD)
        kn = kt * lax.rsqrt(ks + EPS) * gk_ref[...]
        partner = jnp.where(first_half, pltpu.roll(kn, HEAD_PAD - half, 1), pltpu.roll(kn, half, 1))
        k_ref[0, hh] = (kn * cos_rm + partner * sin_rm).astype(BF16)
        vt = _dot(wuv_ref[hh], ckv_t)
        v_ref[0, hh, 0] = jnp.where(vrow == V_DIM, 1.0, vt).astype(BF16)


def _proj_call(x, pos3, an, win, qan, kvan, wuq, wuk, wuv, gq, gk, inv):
    B, S, D = x.shape
    T = PROJ_ROWS
    H = MLA_HEADS
    full = lambda a: pl.BlockSpec(a.shape, lambda b, i: (0,) * a.ndim)
    row = lambda w: pl.BlockSpec((1, T, w), lambda b, i: (b, i, 0))
    return pl.pallas_call(
        _proj_kernel,
        grid=(B, S // T),
        in_specs=[row(D), pl.BlockSpec((1, 1, T), lambda b, i: (b, 0, i))]
        + [full(a) for a in (an, win, qan, kvan, wuq, wuk, wuv, gq, gk, inv)],
        out_specs=[row(LRU_WIDTH), row(LRU_WIDTH),
                   pl.BlockSpec((1, H, HEAD_PAD, T), lambda b, i: (b, 0, 0, i)),
                   pl.BlockSpec((1, H, T, HEAD_PAD), lambda b, i: (b, 0, i, 0)),
                   pl.BlockSpec((1, H, 1, V_ROWS, T), lambda b, i: (b, 0, i, 0, 0))],
        out_shape=[jax.ShapeDtypeStruct((B, S, LRU_WIDTH), F32),
                   jax.ShapeDtypeStruct((B, S, LRU_WIDTH), F32),
                   jax.ShapeDtypeStruct((B, H, HEAD_PAD, S), BF16),
                   jax.ShapeDtypeStruct((B, H, S, HEAD_PAD), BF16),
                   jax.ShapeDtypeStruct((B, H, S // T, V_ROWS, T), BF16)],
        compiler_params=pltpu.CompilerParams(
            dimension_semantics=("parallel", "parallel"), vmem_limit_bytes=VMEM_LIMIT),
        name="proj",
    )(x, pos3, an, win, qan, kvan, wuq, wuk, wuv, gq, gk, inv)


def _lru_kernel(x_ref, cw_ref, cb_ref, wa_ref, ba_ref, wi_ref, bi_ref, lam_ref,
                h_ref, ext_ref, halo_ref, carry_ref):
    T = LRU_ROWS
    d = pl.program_id(1)
    i = pl.program_id(2)

    @pl.when(i == 0)
    def _():
        halo_ref[...] = jnp.zeros_like(halo_ref)
        carry_ref[...] = jnp.zeros_like(carry_ref)

    def run(reverse):
        x = x_ref[0]
        if not reverse:
            ext_ref[0:HALO] = halo_ref[...]
            ext_ref[HALO:HALO + T] = x
            halo_ref[...] = x[T - HALO:T]
            offs = tuple(HALO - (CONV_WIDTH - 1) + k for k in range(CONV_WIDTH))
        else:
            ext_ref[0:T] = x
            ext_ref[T:T + HALO] = halo_ref[...]
            halo_ref[...] = x[0:HALO]
            offs = tuple(range(CONV_WIDTH))
        cw = cw_ref[0]
        xc = cb_ref[0] + cw[0:1] * ext_ref[pl.ds(offs[0], T), :]
        for k in range(1, CONV_WIDTH):
            xc = xc + cw[k:k + 1] * ext_ref[pl.ds(offs[k], T), :]

        xb = xc.astype(BF16)
        r = jax.nn.sigmoid(_dot(xb, wa_ref[0]) + ba_ref[0])
        ig = jax.nn.sigmoid(_dot(xb, wi_ref[0]) + bi_ref[0])
        z = -lam_ref[0]
        sp = jnp.maximum(z, 0.0) + jnp.log1p(jnp.exp(-jnp.abs(z)))
        log_a = (-LRU_C) * r * sp
        a = jnp.exp(log_a)
        mult = jnp.sqrt(1.0 - a * a)
        bb = mult * (ig * xc)

        rowi = lax.broadcasted_iota(jnp.int32, a.shape, 0)
        s = 1
        while s < T:
            if not reverse:
                keep = rowi >= s
                a_sh = jnp.where(keep, pltpu.roll(a, s, 0), 1.0)
                b_sh = jnp.where(keep, pltpu.roll(bb, s, 0), 0.0)
            else:
                keep = rowi < T - s
                a_sh = jnp.where(keep, pltpu.roll(a, T - s, 0), 1.0)
                b_sh = jnp.where(keep, pltpu.roll(bb, T - s, 0), 0.0)
            bb = a * b_sh + bb
            a = a * a_sh
            s *= 2
        hprev = carry_ref[0:1, :]
        hcur = a * hprev + bb
        h_ref[0, 0] = hcur
        edge = hcur[0:1, :] if reverse else hcur[T - 1:T, :]
        carry_ref[...] = jnp.broadcast_to(edge, carry_ref.shape)

    @pl.when(d == 0)
    def _():
        run(False)

    @pl.when(d == 1)
    def _():
        run(True)


def _lru_call(xr, cw, cb, wa, ba, wi, bi, lam):
    B, S, W = xr.shape
    T = LRU_ROWS
    n = S // T
    tile = lambda d, i: i + d * (n - 1 - 2 * i)
    par = lambda a: pl.BlockSpec((1,) + a.shape[1:], lambda b, d, i: (d,) + (0,) * (a.ndim - 1))
    return pl.pallas_call(
        _lru_kernel,
        grid=(B, 2, n),
        in_specs=[pl.BlockSpec((1, T, W), lambda b, d, i: (b, tile(d, i), 0))]
        + [par(a) for a in (cw, cb, wa, ba, wi, bi, lam)],
        out_specs=pl.BlockSpec((1, 1, T, W), lambda b, d, i: (d, b, tile(d, i), 0)),
        out_shape=jax.ShapeDtypeStruct((2, B, S, W), F32),
        scratch_shapes=[pltpu.VMEM((T + HALO, W), F32), pltpu.VMEM((HALO, W), F32),
                        pltpu.VMEM((SUBLANES, W), F32)],
        compiler_params=pltpu.CompilerParams(
            dimension_semantics=("parallel", "arbitrary", "arbitrary"), vmem_limit_bytes=VMEM_LIMIT),
        name="lru",
    )(xr, cw, cb, wa, ba, wi, bi, lam)


def _attn_kernel(q_ref, k_ref, v_ref, o_ref, s_ref):
    TQ, TK = ATTN_Q, ATTN_K
    n = k_ref.shape[2] // TK
    qt = q_ref[0, 0]

    def scores(j, slot):
        s_ref[slot] = _dot(k_ref[0, 0, j * TK:(j + 1) * TK, :], qt)

    def consume(j, slot, m, acc):
        s = s_ref[slot]
        m_new = jnp.maximum(m, jnp.max(s, axis=0, keepdims=True))
        alpha = jnp.exp2(m - m_new)
        p = jnp.exp2(s - m_new).astype(BF16)
        return m_new, alpha * acc + _dot(v_ref[0, 0, j], p)

    m, acc = jnp.full((1, TQ), -jnp.inf, F32), jnp.zeros((V_ROWS, TQ), F32)
    scores(0, 0)
    for j in range(n):
        if j + 1 < n:
            scores(j + 1, (j + 1) % 2)
        m, acc = consume(j, j % 2, m, acc)
    o_ref[0] = acc[0:V_DIM] / acc[V_DIM:V_DIM + 1]


def _attn_call(q, k, v):
    B, H, _, S = q.shape
    TQ = ATTN_Q
    assert v.shape[2] * ATTN_K == S and v.shape[4] == ATTN_K
    return pl.pallas_call(
        _attn_kernel,
        grid=(B, H, S // TQ),
        in_specs=[pl.BlockSpec((1, 1, HEAD_PAD, TQ), lambda b, h, i: (b, h, 0, i)),
                  pl.BlockSpec((1, 1, S, HEAD_PAD), lambda b, h, i: (b, h, 0, 0)),
                  pl.BlockSpec((1, 1) + v.shape[2:], lambda b, h, i: (b, h, 0, 0, 0))],
        out_specs=pl.BlockSpec((1, V_DIM, TQ), lambda b, h, i: (b, h, i)),
        out_shape=jax.ShapeDtypeStruct((B, MLA_WIDTH, S), F32),
        scratch_shapes=[pltpu.VMEM((2, ATTN_K, TQ), F32)],
        compiler_params=pltpu.CompilerParams(
            dimension_semantics=("parallel", "parallel", "arbitrary"), vmem_limit_bytes=VMEM_LIMIT),
        name="attn",
    )(q, k, v)


def _memkv_kernel(mem_ref, mn_ref, wkv_ref, gk_ref, k_ref, v_ref):
    mn = _rms(mem_ref[0], mn_ref[...]).astype(BF16)
    kv = _dot(mn, wkv_ref[...])
    for hh in range(MEM_HEADS):
        kh = kv[:, hh * MEM_HEAD_DIM:(hh + 1) * MEM_HEAD_DIM]
        k_ref[0, hh] = _rms(kh, gk_ref[...]).astype(BF16)
        v_ref[0, hh] = kv[:, MEM_WIDTH + hh * MEM_HEAD_DIM:MEM_WIDTH + (hh + 1) * MEM_HEAD_DIM].astype(BF16)


def _memkv_call(mem, mn, wkv, gk):
    B, M, D = mem.shape
    full = lambda a: pl.BlockSpec(a.shape, lambda b: (0,) * a.ndim)
    hb = pl.BlockSpec((1, MEM_HEADS, M, MEM_HEAD_DIM), lambda b: (b, 0, 0, 0))
    return pl.pallas_call(
        _memkv_kernel,
        grid=(B,),
        in_specs=[pl.BlockSpec((1, M, D), lambda b: (b, 0, 0)), full(mn), full(wkv), full(gk)],
        out_specs=[hb, hb],
        out_shape=[jax.ShapeDtypeStruct((B, MEM_HEADS, M, MEM_HEAD_DIM), BF16)] * 2,
        compiler_params=pltpu.CompilerParams(
            dimension_semantics=("parallel",), vmem_limit_bytes=VMEM_LIMIT),
        name="memkv",
    )(mem, mn, wkv, gk)


def _gelu_tanh(x):
    c = math.sqrt(2.0 / math.pi)
    return x * (0.5 * (1.0 + jnp.tanh(c * (x + 0.044715 * (x * x * x)))))


def _mix_kernel(x_ref, hf_ref, hb_ref, yg_ref, mla_ref, ln_ref, mnn_ref, wol_ref, wom_ref,
                man_ref, wmq_ref, gmq_ref, km_ref, vm_ref, wmo_ref, o_ref):
    x = x_ref[0]
    lru = (hf_ref[0, 0] + hb_ref[0, 0]) * _gelu_tanh(yg_ref[0])
    lru_n = _rms(lru, ln_ref[...]).astype(BF16)
    mt = mla_ref[0]
    mt = mt * lax.rsqrt(jnp.mean(mt * mt, axis=0, keepdims=True) + EPS) * mnn_ref[...]
    mla_n = mt.T.astype(BF16)
    x1 = x + _dot(lru_n, wol_ref[...]) + _dot(mla_n, wom_ref[...])

    hq = _rms(x1, man_ref[...]).astype(BF16)
    qm = _dot(hq, wmq_ref[...])
    scale = MEM_HEAD_DIM ** -0.5
    outs = []
    for hh in range(MEM_HEADS):
        qh = _rms(qm[:, hh * MEM_HEAD_DIM:(hh + 1) * MEM_HEAD_DIM], gmq_ref[...]) * scale
        s = _dot_nt(qh.astype(BF16), km_ref[0, hh])
        p = jnp.exp(s - jnp.max(s, axis=-1, keepdims=True))
        l = jnp.sum(p, axis=-1, keepdims=True)
        outs.append(_dot(p.astype(BF16), vm_ref[0, hh]) / l)
    om = jnp.concatenate(outs, axis=-1).astype(BF16)
    o_ref[0] = x1 + _dot(om, wmo_ref[...])


def _mix_call(x, h2, yg, mla, ln, mnn, wol, wom, man, wmq, gmq, km, vm, wmo):
    B, S, D = x.shape
    T = MIX_ROWS
    M = km.shape[2]
    full = lambda a: pl.BlockSpec(a.shape, lambda b, i: (0,) * a.ndim)
    row = lambda w: pl.BlockSpec((1, T, w), lambda b, i: (b, i, 0))
    hspec = lambda d: pl.BlockSpec((1, 1, T, LRU_WIDTH), lambda b, i: (d, b, i, 0))
    mspec = pl.BlockSpec((1, MEM_HEADS, M, MEM_HEAD_DIM), lambda b, i: (b, 0, 0, 0))
    return pl.pallas_call(
        _mix_kernel,
        grid=(B, S // T),
        in_specs=[row(D), hspec(0), hspec(1), row(LRU_WIDTH),
                  pl.BlockSpec((1, MLA_WIDTH, T), lambda b, i: (b, 0, i)), full(ln), full(mnn),
                  full(wol), full(wom), full(man), full(wmq), full(gmq), mspec, mspec, full(wmo)],
        out_specs=row(D),
        out_shape=jax.ShapeDtypeStruct((B, S, D), F32),
        compiler_params=pltpu.CompilerParams(
            dimension_semantics=("parallel", "parallel"), vmem_limit_bytes=VMEM_LIMIT),
        name="mix",
    )(x, h2, h2, yg, mla, ln, mnn, wol, wom, man, wmq, gmq, km, vm, wmo)


def _ffn_kernel(x_ref, xp_ref, xn_ref, fn_ref, wg_ref, wu_ref, cwg_ref, cwu_ref, cbg_ref, cbu_ref, wd_ref,
                o_ref, xe_ref, g_ref, u_ref, acc_ref):
    T = FFN_ROWS
    i = pl.program_id(1)
    j = pl.program_id(2)

    @pl.when(j == 0)
    def _():
        prev = jnp.where(i == 0, 0.0, xp_ref[0])
        nxt = jnp.where(i == pl.num_programs(1) - 1, 0.0, xn_ref[0])
        g = fn_ref[...]
        xe_ref[0:HALO] = _rms(prev, g).astype(BF16)
        xe_ref[HALO:HALO + T] = _rms(x_ref[0], g).astype(BF16)
        xe_ref[HALO + T:HALO + T + HALO] = _rms(nxt, g).astype(BF16)

    xe = xe_ref[...]
    g_ref[...] = _dot(xe, wg_ref[...])
    u_ref[...] = _dot(xe, wu_ref[...])

    def conv(ref, cw, cb):
        out = cb + cw[0:1] * ref[pl.ds(HALO - 1, T), :]
        out = out + cw[1:2] * ref[pl.ds(HALO, T), :]
        return out + cw[2:3] * ref[pl.ds(HALO + 1, T), :]

    gc = conv(g_ref, cwg_ref[...], cbg_ref[...])
    uc = conv(u_ref, cwu_ref[...], cbu_ref[...])
    act = (gc * jax.nn.sigmoid(gc) * uc).astype(BF16)
    part = _dot(act, wd_ref[...])

    @pl.when(j == 0)
    def _():
        acc_ref[...] = part

    @pl.when(j > 0)
    def _():
        acc_ref[...] += part

    @pl.when(j == pl.num_programs(2) - 1)
    def _():
        o_ref[0] = x_ref[0] + acc_ref[...]


def _ffn_call(x, fn, wup, cw, cb, wd):
    B, S, D = x.shape
    T = FFN_ROWS
    dff = wd.shape[0]
    nf = FFN_SPLIT
    tf = dff // nf
    hb = T // HALO
    last = S // HALO - 1
    return pl.pallas_call(
        _ffn_kernel,
        grid=(B, S // T, nf),
        in_specs=[pl.BlockSpec((1, T, D), lambda b, i, j: (b, i, 0)),
                  pl.BlockSpec((1, HALO, D), lambda b, i, j: (b, jnp.maximum(i * hb - 1, 0), 0)),
                  pl.BlockSpec((1, HALO, D), lambda b, i, j: (b, jnp.minimum((i + 1) * hb, last), 0)),
                  pl.BlockSpec(fn.shape, lambda b, i, j: (0, 0)),
                  pl.BlockSpec((D, tf), lambda b, i, j: (0, j)),
                  pl.BlockSpec((D, tf), lambda b, i, j: (0, j + nf)),
                  pl.BlockSpec((FFN_CONV, tf), lambda b, i, j: (0, j)),
                  pl.BlockSpec((FFN_CONV, tf), lambda b, i, j: (0, j + nf)),
                  pl.BlockSpec((1, tf), lambda b, i, j: (0, j)),
                  pl.BlockSpec((1, tf), lambda b, i, j: (0, j + nf)),
                  pl.BlockSpec((tf, D), lambda b, i, j: (j, 0))],
        out_specs=pl.BlockSpec((1, T, D), lambda b, i, j: (b, i, 0)),
        out_shape=jax.ShapeDtypeStruct((B, S, D), F32),
        scratch_shapes=[pltpu.VMEM((T + 2 * HALO, D), BF16),
                        pltpu.VMEM((T + 2 * HALO, tf), F32),
                        pltpu.VMEM((T + 2 * HALO, tf), F32),
                        pltpu.VMEM((T, D), F32)],
        compiler_params=pltpu.CompilerParams(
            dimension_semantics=("parallel", "parallel", "arbitrary"), vmem_limit_bytes=VMEM_LIMIT),
        name="ffn",
    )(x, x, x, fn, wup, wup, cw, cw, cb, cb, wd)


def _block_diag_dense(w):
    nb, bs, _ = w.shape
    eye = jnp.eye(nb, dtype=w.dtype)
    return (eye[:, None, :, None] * w[:, :, None, :]).reshape(nb * bs, nb * bs)


def _layer(x, mem, pos3, attn_norm, w_in, lru_conv_w, lru_conv_b, lru_w_a, lru_b_a, lru_w_i, lru_b_i,
           lru_lambda, q_a_norm, w_uq, kv_a_norm, w_ukv, mla_q_norm, mla_k_norm, lru_out_norm,
           mla_out_norm, w_out, mem_attn_norm, mem_norm, w_mem_q, w_mem_kv, mem_q_norm, mem_k_norm,
           w_mem_o, ffn_norm, w_up, ffn_conv_w, ffn_conv_b, w_down):
    D = x.shape[-1]
    H = MLA_HEADS
    row = lambda a: a.reshape(1, -1)
    off_kr = 2 * LRU_WIDTH + Q_LORA + KV_LORA
    pad_cols = lambda a, lo, hi: jnp.pad(a, ((0, 0),) * (a.ndim - 1) + ((lo, hi),))

    win = jnp.concatenate([w_in[:, :off_kr], pad_cols(w_in[:, off_kr:], QK_NOPE, HEAD_PAD - QK_HEAD)],
                          axis=1).astype(BF16)
    wuq = pad_cols(w_uq.reshape(Q_LORA, H, QK_HEAD).transpose(1, 0, 2),
                   0, HEAD_PAD - QK_HEAD).transpose(0, 2, 1).astype(BF16)
    wkv = w_ukv.reshape(KV_LORA, H, QK_NOPE + V_DIM).transpose(1, 0, 2)
    wuk = pad_cols(wkv[:, :, :QK_NOPE], 0, HEAD_PAD - QK_NOPE).astype(BF16)
    wuv = pad_cols(wkv[:, :, QK_NOPE:], 0, V_ROWS - V_DIM).transpose(0, 2, 1).astype(BF16)
    gq = pad_cols(row(mla_q_norm), 0, HEAD_PAD - QK_HEAD).reshape(HEAD_PAD, 1)
    gk = pad_cols(row(mla_k_norm), 0, HEAD_PAD - QK_HEAD)
    inv = (ROPE_THETA ** (-jnp.arange(0, QK_ROPE, 2, dtype=F32) / QK_ROPE)).reshape(QK_ROPE // 2, 1)

    xr, yg, q, k, v = _proj_call(x, pos3, row(attn_norm), win, row(q_a_norm), row(kv_a_norm),
                                 wuq, wuk, wuv, gq, gk, inv)

    wa = jax.vmap(_block_diag_dense)(lru_w_a).astype(BF16)
    wi = jax.vmap(_block_diag_dense)(lru_w_i).astype(BF16)
    r3 = lambda a: a.reshape(2, 1, LRU_WIDTH)
    h2 = _lru_call(xr, lru_conv_w, r3(lru_conv_b), wa, r3(lru_b_a), wi, r3(lru_b_i), r3(lru_lambda))

    mla = _attn_call(q, k, v)
    km, vm = _memkv_call(mem, row(mem_norm), w_mem_kv.astype(BF16), row(mem_k_norm))

    wo = w_out.astype(BF16)
    x2 = _mix_call(x, h2, yg, mla, row(lru_out_norm), mla_out_norm.reshape(-1, 1), wo[:LRU_WIDTH], wo[LRU_WIDTH:],
                   row(mem_attn_norm), w_mem_q.astype(BF16), row(mem_q_norm), km, vm, w_mem_o.astype(BF16))

    return _ffn_call(x2, row(ffn_norm), w_up.astype(BF16), ffn_conv_w, row(ffn_conv_b), w_down.astype(BF16))


def kernel(x, mem, positions, attn_norm, w_in, lru_conv_w, lru_conv_b, lru_w_a, lru_b_a, lru_w_i, lru_b_i, lru_lambda, q_a_norm, w_uq, kv_a_norm, w_ukv, mla_q_norm, mla_k_norm, lru_out_norm, mla_out_norm, w_out, mem_attn_norm, mem_norm, w_mem_q, w_mem_kv, mem_q_norm, mem_k_norm, w_mem_o, ffn_norm, w_up, ffn_conv_w, ffn_conv_b, w_down):
    pos3 = positions[:, None, :]
    params = (attn_norm, w_in, lru_conv_w, lru_conv_b, lru_w_a, lru_b_a, lru_w_i, lru_b_i, lru_lambda,
              q_a_norm, w_uq, kv_a_norm, w_ukv, mla_q_norm, mla_k_norm, lru_out_norm, mla_out_norm, w_out,
              mem_attn_norm, mem_norm, w_mem_q, w_mem_kv, mem_q_norm, mem_k_norm, w_mem_o, ffn_norm, w_up,
              ffn_conv_w, ffn_conv_b, w_down)
    for l in range(attn_norm.shape[0]):
        x = _layer(x, mem, pos3, *[p[l] for p in params])
    return x
```

```python
import functools
import math

import jax
import jax.numpy as jnp
from jax import lax
from jax.experimental import pallas as pl
from jax.experimental.pallas import tpu as pltpu

F32 = jnp.float32
BF16 = jnp.bfloat16

EPS = 1e-6
LRU_WIDTH = 512
LRU_BLOCKS = 8
CONV_WIDTH = 4
LRU_C = 8.0
MLA_HEADS = 8
QK_NOPE = 64
QK_ROPE = 32
QK_HEAD = QK_NOPE + QK_ROPE
V_DIM = 64
Q_LORA = 256
KV_LORA = 128
MLA_WIDTH = MLA_HEADS * V_DIM
ROPE_THETA = 10000.0
MEM_HEADS = 4
MEM_HEAD_DIM = 128
MEM_WIDTH = MEM_HEADS * MEM_HEAD_DIM
FFN_CONV = 3

LANES = 128
SUBLANES = 8
MXU_DIM = 256
HEAD_PAD = LANES
BF16_ROWS = 16
V_ROWS = V_DIM + BF16_ROWS
LOG2E = 1.4426950408889634
EXP2_SAFE_RANGE = 100.0

PROJ_ROWS = 512
LRU_ROWS = 256
ATTN_Q = 512
ATTN_K = 512
ATTN_AHEAD = 2
ATTN_SLOTS = ATTN_AHEAD + 1
MIX_ROWS = 512
FFN_ROWS = 512
FFN_SPLIT = 2
HALO = SUBLANES

VMEM_LIMIT = 56 * 1024 * 1024


def _rms(x, g):
    return x * lax.rsqrt(jnp.mean(x * x, axis=-1, keepdims=True) + EPS) * g


def _dot(a, b):
    return jnp.dot(a, b, preferred_element_type=F32)


def _dot_nt(a, b):
    return lax.dot_general(a, b, (((1,), (1,)), ((), ())), preferred_element_type=F32)


def _proj_kernel(x_ref, pos_ref, an_ref, win_ref, qan_ref, kvan_ref, wuq_ref, wuk_ref, wuv_ref,
                 gq_ref, gk_ref, inv_ref,
                 xr_ref, yg_ref, q_ref, k_ref, v_ref):
    x = x_ref[0]
    h = _rms(x, an_ref[...]).astype(BF16)
    proj = _dot(h, win_ref[...])
    xr_ref[0] = proj[:, 0:LRU_WIDTH]
    yg_ref[0] = proj[:, LRU_WIDTH:2 * LRU_WIDTH]
    o = 2 * LRU_WIDTH
    cq_f = _rms(proj[:, o:o + Q_LORA], qan_ref[...])
    o += Q_LORA
    ckv_f = _rms(proj[:, o:o + KV_LORA], kvan_ref[...])
    o += KV_LORA
    kr = proj[:, o:o + HEAD_PAD]
    cq_t = cq_f.T.astype(BF16)
    ckv_t = ckv_f.T.astype(BF16)
    c_kv = ckv_f.astype(BF16)
    T = x.shape[0]
    half = QK_ROPE // 2
    r0, r1, r2 = QK_NOPE, QK_NOPE + half, QK_HEAD

    ang = inv_ref[...] * pos_ref[0].astype(F32)
    cos_t = jnp.cos(ang)
    sin_t = jnp.sin(ang)
    ones = jnp.ones((r0, T), F32)
    zpad = jnp.zeros((HEAD_PAD - r2, T), F32)
    cos_rm = jnp.concatenate([ones, cos_t, cos_t, zpad], axis=0).T
    sin_rm = jnp.concatenate([0.0 * ones, -sin_t, sin_t, zpad], axis=0).T
    lane = lax.broadcasted_iota(jnp.int32, cos_rm.shape, 1)
    first_half = lane < r1

    vrow = lax.broadcasted_iota(jnp.int32, (V_ROWS, T), 0)
    qrow = lax.broadcasted_iota(jnp.int32, (HEAD_PAD, T), 0)
    knorm_max = math.sqrt(QK_HEAD) * jnp.max(jnp.abs(gk_ref[...]), axis=-1, keepdims=True)
    qscale = (QK_HEAD ** -0.5) * LOG2E

    for hh in range(MLA_HEADS):
        qt = _dot(wuq_ref[hh], cq_t)
        ss = jnp.sum(qt * qt, axis=0, keepdims=True) * (1.0 / QK_HEAD)
        tn = qt * lax.rsqrt(ss + EPS) * gq_ref[...]
        t1, t2 = tn[r0:r1], tn[r1:r2]
        qo = jnp.concatenate([tn[0:r0], t1 * cos_t - t2 * sin_t, t1 * sin_t + t2 * cos_t, tn[r2:]], axis=0)
        bound = jnp.sqrt(jnp.sum(tn * tn, axis=0, keepdims=True)) * knorm_max
        q_ref[0, hh] = (jnp.where(qrow == QK_HEAD, -bound, qo) * qscale).astype(BF16)
        kt = _dot(c_kv, wuk_ref[hh]) + kr
        ks = jnp.sum(kt * kt, axis=-1, keepdims=True) * (1.0 / QK_HEAD)
        kn = kt * lax.rsqrt(ks + EPS) * gk_ref[...]
        partner = jnp.where(first_half, pltpu.roll(kn, HEAD_PAD - half, 1), pltpu.roll(kn, half, 1))
        k_ref[0, hh] = jnp.where(lane == QK_HEAD, 1.0, kn * cos_rm + partner * sin_rm).astype(BF16)
        vt = _dot(wuv_ref[hh], ckv_t)
        v_ref[0, hh, 0] = jnp.where(vrow == V_DIM, 1.0, vt).astype(BF16)


def _proj_call(x, pos3, an, win, qan, kvan, wuq, wuk, wuv, gq, gk, inv):
    B, S, D = x.shape
    T = PROJ_ROWS
    H = MLA_HEADS
    full = lambda a: pl.BlockSpec(a.shape, lambda b, i: (0,) * a.ndim)
    row = lambda w: pl.BlockSpec((1, T, w), lambda b, i: (b, i, 0))
    return pl.pallas_call(
        _proj_kernel,
        grid=(B, S // T),
        in_specs=[row(D), pl.BlockSpec((1, 1, T), lambda b, i: (b, 0, i))]
        + [full(a) for a in (an, win, qan, kvan, wuq, wuk, wuv, gq, gk, inv)],
        out_specs=[row(LRU_WIDTH), row(LRU_WIDTH),
                   pl.BlockSpec((1, H, HEAD_PAD, T), lambda b, i: (b, 0, 0, i)),
                   pl.BlockSpec((1, H, T, HEAD_PAD), lambda b, i: (b, 0, i, 0)),
                   pl.BlockSpec((1, H, 1, V_ROWS, T), lambda b, i: (b, 0, i, 0, 0))],
        out_shape=[jax.ShapeDtypeStruct((B, S, LRU_WIDTH), F32),
                   jax.ShapeDtypeStruct((B, S, LRU_WIDTH), F32),
                   jax.ShapeDtypeStruct((B, H, HEAD_PAD, S), BF16),
                   jax.ShapeDtypeStruct((B, H, S, HEAD_PAD), BF16),
                   jax.ShapeDtypeStruct((B, H, S // T, V_ROWS, T), BF16)],
        compiler_params=pltpu.CompilerParams(
            dimension_semantics=("parallel", "parallel"), vmem_limit_bytes=VMEM_LIMIT),
        name="proj",
    )(x, pos3, an, win, qan, kvan, wuq, wuk, wuv, gq, gk, inv)


def _lru_kernel(x_ref, cw_ref, cb_ref, wa_ref, ba_ref, wi_ref, bi_ref, lam_ref,
                h_ref, ext_ref, halo_ref, carry_ref):
    T = LRU_ROWS
    d = pl.program_id(1)
    i = pl.program_id(2)

    @pl.when(i == 0)
    def _():
        halo_ref[...] = jnp.zeros_like(halo_ref)
        carry_ref[...] = jnp.zeros_like(carry_ref)

    def run(reverse):
        x = x_ref[0]
        if not reverse:
            ext_ref[0:HALO] = halo_ref[...]
            ext_ref[HALO:HALO + T] = x
            halo_ref[...] = x[T - HALO:T]
            offs = tuple(HALO - (CONV_WIDTH - 1) + k for k in range(CONV_WIDTH))
        else:
            ext_ref[0:T] = x
            ext_ref[T:T + HALO] = halo_ref[...]
            halo_ref[...] = x[0:HALO]
            offs = tuple(range(CONV_WIDTH))
        cw = cw_ref[0]
        xc = cb_ref[0] + cw[0:1] * ext_ref[pl.ds(offs[0], T), :]
        for k in range(1, CONV_WIDTH):
            xc = xc + cw[k:k + 1] * ext_ref[pl.ds(offs[k], T), :]

        xb = xc.astype(BF16)
        r = jax.nn.sigmoid(_dot(xb, wa_ref[0]) + ba_ref[0])
        ig = jax.nn.sigmoid(_dot(xb, wi_ref[0]) + bi_ref[0])
        z = -lam_ref[0]
        sp = jnp.maximum(z, 0.0) + jnp.log1p(jnp.exp(-jnp.abs(z)))
        log_a = (-LRU_C) * r * sp
        a = jnp.exp(log_a)
        mult = jnp.sqrt(1.0 - a * a)
        bb = mult * (ig * xc)

        rowi = lax.broadcasted_iota(jnp.int32, a.shape, 0)
        s = 1
        while s < T:
            if not reverse:
                keep = rowi >= s
                a_sh = jnp.where(keep, pltpu.roll(a, s, 0), 1.0)
                b_sh = jnp.where(keep, pltpu.roll(bb, s, 0), 0.0)
            else:
                keep = rowi < T - s
                a_sh = jnp.where(keep, pltpu.roll(a, T - s, 0), 1.0)
                b_sh = jnp.where(keep, pltpu.roll(bb, T - s, 0), 0.0)
            bb = a * b_sh + bb
            a = a * a_sh
            s *= 2
        hprev = carry_ref[0:1, :]
        hcur = a * hprev + bb
        h_ref[0, 0] = hcur
        edge = hcur[0:1, :] if reverse else hcur[T - 1:T, :]
        carry_ref[...] = jnp.broadcast_to(edge, carry_ref.shape)

    @pl.when(d == 0)
    def _():
        run(False)

    @pl.when(d == 1)
    def _():
        run(True)


def _lru_call(xr, cw, cb, wa, ba, wi, bi, lam):
    B, S, W = xr.shape
    T = LRU_ROWS
    n = S // T
    tile = lambda d, i: i + d * (n - 1 - 2 * i)
    par = lambda a: pl.BlockSpec((1,) + a.shape[1:], lambda b, d, i: (d,) + (0,) * (a.ndim - 1))
    return pl.pallas_call(
        _lru_kernel,
        grid=(B, 2, n),
        in_specs=[pl.BlockSpec((1, T, W), lambda b, d, i: (b, tile(d, i), 0))]
        + [par(a) for a in (cw, cb, wa, ba, wi, bi, lam)],
        out_specs=pl.BlockSpec((1, 1, T, W), lambda b, d, i: (d, b, tile(d, i), 0)),
        out_shape=jax.ShapeDtypeStruct((2, B, S, W), F32),
        scratch_shapes=[pltpu.VMEM((T + HALO, W), F32), pltpu.VMEM((HALO, W), F32),
                        pltpu.VMEM((SUBLANES, W), F32)],
        compiler_params=pltpu.CompilerParams(
            dimension_semantics=("parallel", "arbitrary", "arbitrary"), vmem_limit_bytes=VMEM_LIMIT),
        name="lru",
    )(xr, cw, cb, wa, ba, wi, bi, lam)


def _attn_kernel(bounded_ref, q_ref, k_ref, v_ref, o_ref, s_ref):
    TQ, TK = ATTN_Q, ATTN_K
    n = k_ref.shape[2] // TK
    qt = q_ref[0, 0]

    def finish(acc):
        o_ref[0] = acc[0:V_DIM] / acc[V_DIM:V_DIM + 1]

    @pl.when(bounded_ref[0] != 0)
    def _():
        acc = None
        s = _dot(k_ref[0, 0, 0:TK, :], qt)
        for j in range(n):
            s_next = _dot(k_ref[0, 0, (j + 1) * TK:(j + 2) * TK, :], qt) if j + 1 < n else None
            d = _dot(v_ref[0, 0, j], jnp.exp2(s).astype(BF16))
            acc = d if acc is None else acc + d
            s = s_next
        finish(acc)

    @pl.when(bounded_ref[0] == 0)
    def _():
        def scores(j, slot):
            s_ref[slot] = _dot(k_ref[0, 0, j * TK:(j + 1) * TK, :], qt)

        def consume(j, slot, m, acc):
            s = s_ref[slot]
            m_new = jnp.maximum(m, jnp.max(s, axis=0, keepdims=True))
            alpha = jnp.exp2(m - m_new)
            p = jnp.exp2(s - m_new).astype(BF16)
            return m_new, alpha * acc + _dot(v_ref[0, 0, j], p)

        m, acc = jnp.full((1, TQ), -jnp.inf, F32), jnp.zeros((V_ROWS, TQ), F32)
        for j in range(min(ATTN_AHEAD, n)):
            scores(j, j % ATTN_SLOTS)
        for j in range(n):
            if j + ATTN_AHEAD < n:
                scores(j + ATTN_AHEAD, (j + ATTN_AHEAD) % ATTN_SLOTS)
            m, acc = consume(j, j % ATTN_SLOTS, m, acc)
        finish(acc)


def _attn_call(bounded, q, k, v):
    B, H, _, S = q.shape
    TQ = ATTN_Q
    assert v.shape[2] * ATTN_K == S and v.shape[4] == ATTN_K
    return pl.pallas_call(
        _attn_kernel,
        grid_spec=pltpu.PrefetchScalarGridSpec(
            num_scalar_prefetch=1,
            grid=(B, H, S // TQ),
            in_specs=[pl.BlockSpec((1, 1, HEAD_PAD, TQ), lambda b, h, i, f: (b, h, 0, i)),
                      pl.BlockSpec((1, 1, S, HEAD_PAD), lambda b, h, i, f: (b, h, 0, 0)),
                      pl.BlockSpec((1, 1) + v.shape[2:], lambda b, h, i, f: (b, h, 0, 0, 0))],
            out_specs=pl.BlockSpec((1, V_DIM, TQ), lambda b, h, i, f: (b, h, i)),
            scratch_shapes=[pltpu.VMEM((ATTN_SLOTS, ATTN_K, TQ), F32)]),
        out_shape=jax.ShapeDtypeStruct((B, MLA_WIDTH, S), F32),
        compiler_params=pltpu.CompilerParams(
            dimension_semantics=("parallel", "parallel", "arbitrary"), vmem_limit_bytes=VMEM_LIMIT),
        name="attn",
    )(bounded, q, k, v)


def _memkv_kernel(mem_ref, mn_ref, wkv_ref, gk_ref, k_ref, v_ref):
    mn = _rms(mem_ref[0], mn_ref[...]).astype(BF16)
    kv = _dot(mn, wkv_ref[...])
    for hh in range(MEM_HEADS):
        kh = kv[:, hh * MEM_HEAD_DIM:(hh + 1) * MEM_HEAD_DIM]
        k_ref[0, hh] = _rms(kh, gk_ref[...]).astype(BF16)
        v_ref[0, hh] = kv[:, MEM_WIDTH + hh * MEM_HEAD_DIM:MEM_WIDTH + (hh + 1) * MEM_HEAD_DIM].astype(BF16)


def _memkv_call(mem, mn, wkv, gk):
    B, M, D = mem.shape
    full = lambda a: pl.BlockSpec(a.shape, lambda b: (0,) * a.ndim)
    hb = pl.BlockSpec((1, MEM_HEADS, M, MEM_HEAD_DIM), lambda b: (b, 0, 0, 0))
    return pl.pallas_call(
        _memkv_kernel,
        grid=(B,),
        in_specs=[pl.BlockSpec((1, M, D), lambda b: (b, 0, 0)), full(mn), full(wkv), full(gk)],
        out_specs=[hb, hb],
        out_shape=[jax.ShapeDtypeStruct((B, MEM_HEADS, M, MEM_HEAD_DIM), BF16)] * 2,
        compiler_params=pltpu.CompilerParams(
            dimension_semantics=("parallel",), vmem_limit_bytes=VMEM_LIMIT),
        name="memkv",
    )(mem, mn, wkv, gk)


def _gelu_tanh(x):
    c = math.sqrt(2.0 / math.pi)
    return x * (0.5 * (1.0 + jnp.tanh(c * (x + 0.044715 * (x * x * x)))))


def _mix_kernel(x_ref, hf_ref, hb_ref, yg_ref, mla_ref, ln_ref, mnn_ref, wol_ref, wom_ref,
                man_ref, wmq_ref, gmq_ref, km_ref, vm_ref, wmo_ref, o_ref):
    x = x_ref[0]
    lru = (hf_ref[0, 0] + hb_ref[0, 0]) * _gelu_tanh(yg_ref[0])
    lru_n = _rms(lru, ln_ref[...]).astype(BF16)
    mt = mla_ref[0]
    mt = mt * lax.rsqrt(jnp.mean(mt * mt, axis=0, keepdims=True) + EPS) * mnn_ref[...]
    mla_n = mt.T.astype(BF16)
    x1 = x + _dot(lru_n, wol_ref[...]) + _dot(mla_n, wom_ref[...])

    hq = _rms(x1, man_ref[...]).astype(BF16)
    qm = _dot(hq, wmq_ref[...])
    scale = MEM_HEAD_DIM ** -0.5
    outs = []
    for hh in range(MEM_HEADS):
        qh = _rms(qm[:, hh * MEM_HEAD_DIM:(hh + 1) * MEM_HEAD_DIM], gmq_ref[...]) * scale
        s = _dot_nt(qh.astype(BF16), km_ref[0, hh])
        p = jnp.exp(s - jnp.max(s, axis=-1, keepdims=True))
        l = jnp.sum(p, axis=-1, keepdims=True)
        outs.append(_dot(p.astype(BF16), vm_ref[0, hh]) / l)
    om = jnp.concatenate(outs, axis=-1).astype(BF16)
    o_ref[0] = x1 + _dot(om, wmo_ref[...])


def _mix_call(x, h2, yg, mla, ln, mnn, wol, wom, man, wmq, gmq, km, vm, wmo):
    B, S, D = x.shape
    T = MIX_ROWS
    M = km.shape[2]
    full = lambda a: pl.BlockSpec(a.shape, lambda b, i: (0,) * a.ndim)
    row = lambda w: pl.BlockSpec((1, T, w), lambda b, i: (b, i, 0))
    hspec = lambda d: pl.BlockSpec((1, 1, T, LRU_WIDTH), lambda b, i: (d, b, i, 0))
    mspec = pl.BlockSpec((1, MEM_HEADS, M, MEM_HEAD_DIM), lambda b, i: (b, 0, 0, 0))
    return pl.pallas_call(
        _mix_kernel,
        grid=(B, S // T),
        in_specs=[row(D), hspec(0), hspec(1), row(LRU_WIDTH),
                  pl.BlockSpec((1, MLA_WIDTH, T), lambda b, i: (b, 0, i)), full(ln), full(mnn),
                  full(wol), full(wom), full(man), full(wmq), full(gmq), mspec, mspec, full(wmo)],
        out_specs=row(D),
        out_shape=jax.ShapeDtypeStruct((B, S, D), F32),
        compiler_params=pltpu.CompilerParams(
            dimension_semantics=("parallel", "parallel"), vmem_limit_bytes=VMEM_LIMIT),
        name="mix",
    )(x, h2, h2, yg, mla, ln, mnn, wol, wom, man, wmq, gmq, km, vm, wmo)


def _ffn_kernel(x_ref, xp_ref, xn_ref, fn_ref, wg_ref, wu_ref, cwg_ref, cwu_ref, cbg_ref, cbu_ref, wd_ref,
                o_ref, xe_ref, g_ref, u_ref, acc_ref):
    T = FFN_ROWS
    i = pl.program_id(1)
    j = pl.program_id(2)

    @pl.when(j == 0)
    def _():
        prev = jnp.where(i == 0, 0.0, xp_ref[0])
        nxt = jnp.where(i == pl.num_programs(1) - 1, 0.0, xn_ref[0])
        g = fn_ref[...]
        xe_ref[0:HALO] = _rms(prev, g).astype(BF16)
        xe_ref[HALO:HALO + T] = _rms(x_ref[0], g).astype(BF16)
        xe_ref[HALO + T:HALO + T + HALO] = _rms(nxt, g).astype(BF16)

    xe = xe_ref[...]
    g_ref[...] = _dot(xe, wg_ref[...])
    u_ref[...] = _dot(xe, wu_ref[...])

    def conv(ref, cw, cb):
        out = cb + cw[0:1] * ref[pl.ds(HALO - 1, T), :]
        out = out + cw[1:2] * ref[pl.ds(HALO, T), :]
        return out + cw[2:3] * ref[pl.ds(HALO + 1, T), :]

    gc = conv(g_ref, cwg_ref[...], cbg_ref[...])
    uc = conv(u_ref, cwu_ref[...], cbu_ref[...])
    act = (gc * jax.nn.sigmoid(gc) * uc).astype(BF16)
    part = _dot(act, wd_ref[...])

    @pl.when(j == 0)
    def _():
        acc_ref[...] = part

    @pl.when(j > 0)
    def _():
        acc_ref[...] += part

    @pl.when(j == pl.num_programs(2) - 1)
    def _():
        o_ref[0] = x_ref[0] + acc_ref[...]


def _ffn_call(x, fn, wup, cw, cb, wd):
    B, S, D = x.shape
    T = FFN_ROWS
    dff = wd.shape[0]
    nf = FFN_SPLIT
    tf = dff // nf
    hb = T // HALO
    last = S // HALO - 1
    return pl.pallas_call(
        _ffn_kernel,
        grid=(B, S // T, nf),
        in_specs=[pl.BlockSpec((1, T, D), lambda b, i, j: (b, i, 0)),
                  pl.BlockSpec((1, HALO, D), lambda b, i, j: (b, jnp.maximum(i * hb - 1, 0), 0)),
                  pl.BlockSpec((1, HALO, D), lambda b, i, j: (b, jnp.minimum((i + 1) * hb, last), 0)),
                  pl.BlockSpec(fn.shape, lambda b, i, j: (0, 0)),
                  pl.BlockSpec((D, tf), lambda b, i, j: (0, j)),
                  pl.BlockSpec((D, tf), lambda b, i, j: (0, j + nf)),
                  pl.BlockSpec((FFN_CONV, tf), lambda b, i, j: (0, j)),
                  pl.BlockSpec((FFN_CONV, tf), lambda b, i, j: (0, j + nf)),
                  pl.BlockSpec((1, tf), lambda b, i, j: (0, j)),
                  pl.BlockSpec((1, tf), lambda b, i, j: (0, j + nf)),
                  pl.BlockSpec((tf, D), lambda b, i, j: (j, 0))],
        out_specs=pl.BlockSpec((1, T, D), lambda b, i, j: (b, i, 0)),
        out_shape=jax.ShapeDtypeStruct((B, S, D), F32),
        scratch_shapes=[pltpu.VMEM((T + 2 * HALO, D), BF16),
                        pltpu.VMEM((T + 2 * HALO, tf), F32),
                        pltpu.VMEM((T + 2 * HALO, tf), F32),
                        pltpu.VMEM((T, D), F32)],
        compiler_params=pltpu.CompilerParams(
            dimension_semantics=("parallel", "parallel", "arbitrary"), vmem_limit_bytes=VMEM_LIMIT),
        name="ffn",
    )(x, x, x, fn, wup, wup, cw, cw, cb, cb, wd)


def _block_diag_dense(w):
    nb, bs, _ = w.shape
    eye = jnp.eye(nb, dtype=w.dtype)
    return (eye[:, None, :, None] * w[:, :, None, :]).reshape(nb * bs, nb * bs)


def _layer(x, mem, pos3, attn_norm, w_in, lru_conv_w, lru_conv_b, lru_w_a, lru_b_a, lru_w_i, lru_b_i,
           lru_lambda, q_a_norm, w_uq, kv_a_norm, w_ukv, mla_q_norm, mla_k_norm, lru_out_norm,
           mla_out_norm, w_out, mem_attn_norm, mem_norm, w_mem_q, w_mem_kv, mem_q_norm, mem_k_norm,
           w_mem_o, ffn_norm, w_up, ffn_conv_w, ffn_conv_b, w_down):
    D = x.shape[-1]
    H = MLA_HEADS
    row = lambda a: a.reshape(1, -1)
    off_kr = 2 * LRU_WIDTH + Q_LORA + KV_LORA
    pad_cols = lambda a, lo, hi: jnp.pad(a, ((0, 0),) * (a.ndim - 1) + ((lo, hi),))

    win = jnp.concatenate([w_in[:, :off_kr], pad_cols(w_in[:, off_kr:], QK_NOPE, HEAD_PAD - QK_HEAD)],
                          axis=1).astype(BF16)
    wuq = pad_cols(w_uq.reshape(Q_LORA, H, QK_HEAD).transpose(1, 0, 2),
                   0, HEAD_PAD - QK_HEAD).transpose(0, 2, 1).astype(BF16)
    wkv = w_ukv.reshape(KV_LORA, H, QK_NOPE + V_DIM).transpose(1, 0, 2)
    wuk = pad_cols(wkv[:, :, :QK_NOPE], 0, HEAD_PAD - QK_NOPE).astype(BF16)
    wuv = pad_cols(wkv[:, :, QK_NOPE:], 0, V_ROWS - V_DIM).transpose(0, 2, 1).astype(BF16)
    gq = pad_cols(row(mla_q_norm), 0, HEAD_PAD - QK_HEAD).reshape(HEAD_PAD, 1)
    gk = pad_cols(row(mla_k_norm), 0, HEAD_PAD - QK_HEAD)
    inv = (ROPE_THETA ** (-jnp.arange(0, QK_ROPE, 2, dtype=F32) / QK_ROPE)).reshape(QK_ROPE // 2, 1)

    xr, yg, q, k, v = _proj_call(x, pos3, row(attn_norm), win, row(q_a_norm), row(kv_a_norm),
                                 wuq, wuk, wuv, gq, gk, inv)

    wa = jax.vmap(_block_diag_dense)(lru_w_a).astype(BF16)
    wi = jax.vmap(_block_diag_dense)(lru_w_i).astype(BF16)
    r3 = lambda a: a.reshape(2, 1, LRU_WIDTH)
    h2 = _lru_call(xr, lru_conv_w, r3(lru_conv_b), wa, r3(lru_b_a), wi, r3(lru_b_i), r3(lru_lambda))

    score_bound = math.sqrt(QK_HEAD) * LOG2E * jnp.max(jnp.abs(mla_q_norm)) * jnp.max(jnp.abs(mla_k_norm))
    bounded = (2.0 * score_bound <= EXP2_SAFE_RANGE).astype(jnp.int32).reshape(1)
    mla = _attn_call(bounded, q, k, v)
    km, vm = _memkv_call(mem, row(mem_norm), w_mem_kv.astype(BF16), row(mem_k_norm))

    wo = w_out.astype(BF16)
    x2 = _mix_call(x, h2, yg, mla, row(lru_out_norm), mla_out_norm.reshape(-1, 1), wo[:LRU_WIDTH], wo[LRU_WIDTH:],
                   row(mem_attn_norm), w_mem_q.astype(BF16), row(mem_q_norm), km, vm, w_mem_o.astype(BF16))

    return _ffn_call(x2, row(ffn_norm), w_up.astype(BF16), ffn_conv_w, row(ffn_conv_b), w_down.astype(BF16))


def kernel(x, mem, positions, attn_norm, w_in, lru_conv_w, lru_conv_b, lru_w_a, lru_b_a, lru_w_i, lru_b_i, lru_lambda, q_a_norm, w_uq, kv_a_norm, w_ukv, mla_q_norm, mla_k_norm, lru_out_norm, mla_out_norm, w_out, mem_attn_norm, mem_norm, w_mem_q, w_mem_kv, mem_q_norm, mem_k_norm, w_mem_o, ffn_norm, w_up, ffn_conv_w, ffn_conv_b, w_down):
    pos3 = positions[:, None, :]
    params = (attn_norm, w_in, lru_conv_w, lru_conv_b, lru_w_a, lru_b_a, lru_w_i, lru_b_i, lru_lambda,
              q_a_norm, w_uq, kv_a_norm, w_ukv, mla_q_norm, mla_k_norm, lru_out_norm, mla_out_norm, w_out,
              mem_attn_norm, mem_norm, w_mem_q, w_mem_kv, mem_q_norm, mem_k_norm, w_mem_o, ffn_norm, w_up,
              ffn_conv_w, ffn_conv_b, w_down)
    for l in range(attn_norm.shape[0]):
        x = _layer(x, mem, pos3, *[p[l] for p in params])
    return x
```

```python
import functools
import math

import jax
import jax.numpy as jnp
from jax import lax
from jax.experimental import pallas as pl
from jax.experimental.pallas import tpu as pltpu

F32 = jnp.float32
BF16 = jnp.bfloat16

EPS = 1e-6
LRU_WIDTH = 512
LRU_BLOCKS = 8
CONV_WIDTH = 4
LRU_C = 8.0
MLA_HEADS = 8
QK_NOPE = 64
QK_ROPE = 32
QK_HEAD = QK_NOPE + QK_ROPE
V_DIM = 64
Q_LORA = 256
KV_LORA = 128
MLA_WIDTH = MLA_HEADS * V_DIM
ROPE_THETA = 10000.0
MEM_HEADS = 4
MEM_HEAD_DIM = 128
MEM_WIDTH = MEM_HEADS * MEM_HEAD_DIM
FFN_CONV = 3

LANES = 128
SUBLANES = 8
MXU_DIM = 256
HEAD_PAD = LANES
BF16_ROWS = 16
V_ROWS = V_DIM + BF16_ROWS
LOG2E = 1.4426950408889634
EXP2_SAFE_RANGE = 100.0

PROJ_ROWS = 512
LRU_ROWS = 512
ATTN_Q = 512
ATTN_K = 512
ATTN_AHEAD = 2
ATTN_SLOTS = ATTN_AHEAD + 1
MIX_ROWS = 512
FFN_ROWS = 512
FFN_SPLIT = 2
HALO = SUBLANES

VMEM_LIMIT = 56 * 1024 * 1024


def _rms(x, g):
    return x * lax.rsqrt(jnp.mean(x * x, axis=-1, keepdims=True) + EPS) * g


def _dot(a, b):
    return jnp.dot(a, b, preferred_element_type=F32)


def _dot_nt(a, b):
    return lax.dot_general(a, b, (((1,), (1,)), ((), ())), preferred_element_type=F32)


def _proj_kernel(x_ref, pos_ref, an_ref, win_ref, qan_ref, kvan_ref, wuq_ref, wuk_ref, wuv_ref,
                 gq_ref, gk_ref, inv_ref,
                 xr_ref, yg_ref, q_ref, k_ref, v_ref):
    x = x_ref[0]
    h = _rms(x, an_ref[...]).astype(BF16)
    proj = _dot(h, win_ref[...])
    xr_ref[0] = proj[:, 0:LRU_WIDTH]
    yg_ref[0] = proj[:, LRU_WIDTH:2 * LRU_WIDTH]
    o = 2 * LRU_WIDTH
    cq_f = _rms(proj[:, o:o + Q_LORA], qan_ref[...])
    o += Q_LORA
    ckv_f = _rms(proj[:, o:o + KV_LORA], kvan_ref[...])
    o += KV_LORA
    kr = proj[:, o:o + HEAD_PAD]
    cq_t = cq_f.T.astype(BF16)
    ckv_t = ckv_f.T.astype(BF16)
    c_kv = ckv_f.astype(BF16)
    T = x.shape[0]
    half = QK_ROPE // 2
    r0, r1, r2 = QK_NOPE, QK_NOPE + half, QK_HEAD

    ang = inv_ref[...] * pos_ref[0].astype(F32)
    cos_t = jnp.cos(ang)
    sin_t = jnp.sin(ang)
    ones = jnp.ones((r0, T), F32)
    zpad = jnp.zeros((HEAD_PAD - r2, T), F32)
    cos_rm = jnp.concatenate([ones, cos_t, cos_t, zpad], axis=0).T
    sin_rm = jnp.concatenate([0.0 * ones, -sin_t, sin_t, zpad], axis=0).T
    lane = lax.broadcasted_iota(jnp.int32, cos_rm.shape, 1)
    first_half = lane < r1

    vrow = lax.broadcasted_iota(jnp.int32, (V_ROWS, T), 0)
    qrow = lax.broadcasted_iota(jnp.int32, (HEAD_PAD, T), 0)
    knorm_max = math.sqrt(QK_HEAD) * jnp.max(jnp.abs(gk_ref[...]), axis=-1, keepdims=True)
    qscale = (QK_HEAD ** -0.5) * LOG2E

    for hh in range(MLA_HEADS):
        qt = _dot(wuq_ref[hh], cq_t)
        ss = jnp.sum(qt * qt, axis=0, keepdims=True) * (1.0 / QK_HEAD)
        tn = qt * lax.rsqrt(ss + EPS) * gq_ref[...]
        t1, t2 = tn[r0:r1], tn[r1:r2]
        qo = jnp.concatenate([tn[0:r0], t1 * cos_t - t2 * sin_t, t1 * sin_t + t2 * cos_t, tn[r2:]], axis=0)
        bound = jnp.sqrt(jnp.sum(tn * tn, axis=0, keepdims=True)) * knorm_max
        q_ref[0, hh] = (jnp.where(qrow == QK_HEAD, -bound, qo) * qscale).astype(BF16)
        kt = _dot(c_kv, wuk_ref[hh]) + kr
        ks = jnp.sum(kt * kt, axis=-1, keepdims=True) * (1.0 / QK_HEAD)
        kn = kt * lax.rsqrt(ks + EPS) * gk_ref[...]
        partner = jnp.where(first_half, pltpu.roll(kn, HEAD_PAD - half, 1), pltpu.roll(kn, half, 1))
        k_ref[0, hh] = jnp.where(lane == QK_HEAD, 1.0, kn * cos_rm + partner * sin_rm).astype(BF16)
        vt = _dot(wuv_ref[hh], ckv_t)
        v_ref[0, hh, 0] = jnp.where(vrow == V_DIM, 1.0, vt).astype(BF16)


def _proj_call(x, pos3, an, win, qan, kvan, wuq, wuk, wuv, gq, gk, inv):
    B, S, D = x.shape
    T = PROJ_ROWS
    H = MLA_HEADS
    full = lambda a: pl.BlockSpec(a.shape, lambda b, i: (0,) * a.ndim)
    row = lambda w: pl.BlockSpec((1, T, w), lambda b, i: (b, i, 0))
    return pl.pallas_call(
        _proj_kernel,
        grid=(B, S // T),
        in_specs=[row(D), pl.BlockSpec((1, 1, T), lambda b, i: (b, 0, i))]
        + [full(a) for a in (an, win, qan, kvan, wuq, wuk, wuv, gq, gk, inv)],
        out_specs=[row(LRU_WIDTH), row(LRU_WIDTH),
                   pl.BlockSpec((1, H, HEAD_PAD, T), lambda b, i: (b, 0, 0, i)),
                   pl.BlockSpec((1, H, T, HEAD_PAD), lambda b, i: (b, 0, i, 0)),
                   pl.BlockSpec((1, H, 1, V_ROWS, T), lambda b, i: (b, 0, i, 0, 0))],
        out_shape=[jax.ShapeDtypeStruct((B, S, LRU_WIDTH), F32),
                   jax.ShapeDtypeStruct((B, S, LRU_WIDTH), F32),
                   jax.ShapeDtypeStruct((B, H, HEAD_PAD, S), BF16),
                   jax.ShapeDtypeStruct((B, H, S, HEAD_PAD), BF16),
                   jax.ShapeDtypeStruct((B, H, S // T, V_ROWS, T), BF16)],
        compiler_params=pltpu.CompilerParams(
            dimension_semantics=("parallel", "parallel"), vmem_limit_bytes=VMEM_LIMIT),
        name="proj",
    )(x, pos3, an, win, qan, kvan, wuq, wuk, wuv, gq, gk, inv)


def _lru_kernel(x_ref, cw_ref, cb_ref, wa_ref, ba_ref, wi_ref, bi_ref, lam_ref,
                h_ref, px_ref, ph_ref, halo_ref, carry_ref):
    T = LRU_ROWS
    K = T // SUBLANES
    PS = K + SUBLANES
    W = x_ref.shape[2]
    NC = W // LANES
    NH = CONV_WIDTH - 1
    d = pl.program_id(1)
    i = pl.program_id(2)

    @pl.when(i == 0)
    def _():
        halo_ref[...] = jnp.zeros_like(halo_ref)
        carry_ref[...] = jnp.zeros_like(carry_ref)

    def run(reverse):
        x = x_ref[0]
        for c in range(SUBLANES):
            for j in range(NC):
                px_ref[j, c * PS:c * PS + K] = x[c * K:(c + 1) * K, j * LANES:(j + 1) * LANES]
        slabs = [jnp.concatenate([px_ref[j, pl.ds(k, SUBLANES, stride=PS), :] for j in range(NC)], axis=1)
                 for k in range(K)]
        rowi = lax.broadcasted_iota(jnp.int32, (SUBLANES, W), 0)
        if not reverse:
            edge = [slabs[K - NH + m] for m in range(NH)]
            pre = [pltpu.roll(jnp.where(rowi == SUBLANES - 1, halo_ref[m], edge[m]), 1, 0) for m in range(NH)]
            xext = jnp.concatenate(pre + slabs, axis=0)
        else:
            edge = [slabs[m] for m in range(NH)]
            post = [pltpu.roll(jnp.where(rowi == 0, halo_ref[m], edge[m]), SUBLANES - 1, 0) for m in range(NH)]
            xext = jnp.concatenate(slabs + post, axis=0)
        for m in range(NH):
            halo_ref[m] = edge[m]
        cw = cw_ref[0]
        xc = cb_ref[0] + cw[0:1] * xext[0:T]
        for k in range(1, CONV_WIDTH):
            xc = xc + cw[k:k + 1] * xext[SUBLANES * k:SUBLANES * k + T]

        xb = xc.astype(BF16)
        r = jax.nn.sigmoid(_dot(xb, wa_ref[0]) + ba_ref[0])
        ig = jax.nn.sigmoid(_dot(xb, wi_ref[0]) + bi_ref[0])
        z = -lam_ref[0]
        sp = jnp.maximum(z, 0.0) + jnp.log1p(jnp.exp(-jnp.abs(z)))
        a = jnp.exp2(r * ((-LRU_C * LOG2E) * sp))
        mult = jnp.sqrt(1.0 - a * a)
        bb = mult * (ig * xc)

        hk = jnp.zeros((SUBLANES, W), F32)
        pk = jnp.ones((SUBLANES, W), F32)
        h_loc, p_loc = [None] * K, [None] * K
        for k in (range(K - 1, -1, -1) if reverse else range(K)):
            ak = a[SUBLANES * k:SUBLANES * (k + 1)]
            hk = ak * hk + bb[SUBLANES * k:SUBLANES * (k + 1)]
            pk = ak * pk
            h_loc[k], p_loc[k] = hk, pk
        s = 1
        while s < SUBLANES:
            if not reverse:
                keep = rowi >= s
                p_sh = jnp.where(keep, pltpu.roll(pk, s, 0), 1.0)
                h_sh = jnp.where(keep, pltpu.roll(hk, s, 0), 0.0)
            else:
                keep = rowi < SUBLANES - s
                p_sh = jnp.where(keep, pltpu.roll(pk, SUBLANES - s, 0), 1.0)
                h_sh = jnp.where(keep, pltpu.roll(hk, SUBLANES - s, 0), 0.0)
            hk = pk * h_sh + hk
            pk = pk * p_sh
            s *= 2
        carry = carry_ref[...]
        leave = pk * carry + hk
        if not reverse:
            enter = jnp.where(rowi == 0, carry, pltpu.roll(leave, 1, 0))
            carry_ref[...] = jnp.broadcast_to(leave[SUBLANES - 1:SUBLANES], carry.shape)
        else:
            enter = jnp.where(rowi == SUBLANES - 1, carry, pltpu.roll(leave, SUBLANES - 1, 0))
            carry_ref[...] = jnp.broadcast_to(leave[0:1], carry.shape)
        for k in range(K):
            hfull = h_loc[k] + p_loc[k] * enter
            for j in range(NC):
                ph_ref[j, pl.ds(k, SUBLANES, stride=PS), :] = hfull[:, j * LANES:(j + 1) * LANES]
        for c in range(SUBLANES):
            for j in range(NC):
                h_ref[0, 0, c * K:(c + 1) * K, j * LANES:(j + 1) * LANES] = ph_ref[j, c * PS:c * PS + K]

    @pl.when(d == 0)
    def _():
        run(False)

    @pl.when(d == 1)
    def _():
        run(True)


def _lru_call(xr, cw, cb, wa, ba, wi, bi, lam):
    B, S, W = xr.shape
    T = LRU_ROWS
    n = S // T
    tile = lambda d, i: i + d * (n - 1 - 2 * i)
    par = lambda a: pl.BlockSpec((1,) + a.shape[1:], lambda b, d, i: (d,) + (0,) * (a.ndim - 1))
    return pl.pallas_call(
        _lru_kernel,
        grid=(B, 2, n),
        in_specs=[pl.BlockSpec((1, T, W), lambda b, d, i: (b, tile(d, i), 0))]
        + [par(a) for a in (cw, cb, wa, ba, wi, bi, lam)],
        out_specs=pl.BlockSpec((1, 1, T, W), lambda b, d, i: (d, b, tile(d, i), 0)),
        out_shape=jax.ShapeDtypeStruct((2, B, S, W), F32),
        scratch_shapes=[pltpu.VMEM((W // LANES, T + SUBLANES * SUBLANES, LANES), F32),
                        pltpu.VMEM((W // LANES, T + SUBLANES * SUBLANES, LANES), F32),
                        pltpu.VMEM((CONV_WIDTH - 1, SUBLANES, W), F32),
                        pltpu.VMEM((SUBLANES, W), F32)],
        compiler_params=pltpu.CompilerParams(
            dimension_semantics=("parallel", "arbitrary", "arbitrary"), vmem_limit_bytes=VMEM_LIMIT),
        name="lru",
    )(xr, cw, cb, wa, ba, wi, bi, lam)


def _attn_kernel(bounded_ref, q_ref, k_ref, v_ref, o_ref, s_ref):
    TQ, TK = ATTN_Q, ATTN_K
    n = k_ref.shape[2] // TK
    qt = q_ref[0, 0]

    def finish(acc):
        o_ref[0] = acc[0:V_DIM] / acc[V_DIM:V_DIM + 1]

    @pl.when(bounded_ref[0] != 0)
    def _():
        acc = None
        s = _dot(k_ref[0, 0, 0:TK, :], qt)
        for j in range(n):
            s_next = _dot(k_ref[0, 0, (j + 1) * TK:(j + 2) * TK, :], qt) if j + 1 < n else None
            d = _dot(v_ref[0, 0, j], jnp.exp2(s).astype(BF16))
            acc = d if acc is None else acc + d
            s = s_next
        finish(acc)

    @pl.when(bounded_ref[0] == 0)
    def _():
        def scores(j, slot):
            s_ref[slot] = _dot(k_ref[0, 0, j * TK:(j + 1) * TK, :], qt)

        def consume(j, slot, m, acc):
            s = s_ref[slot]
            m_new = jnp.maximum(m, jnp.max(s, axis=0, keepdims=True))
            alpha = jnp.exp2(m - m_new)
            p = jnp.exp2(s - m_new).astype(BF16)
            return m_new, alpha * acc + _dot(v_ref[0, 0, j], p)

        m, acc = jnp.full((1, TQ), -jnp.inf, F32), jnp.zeros((V_ROWS, TQ), F32)
        for j in range(min(ATTN_AHEAD, n)):
            scores(j, j % ATTN_SLOTS)
        for j in range(n):
            if j + ATTN_AHEAD < n:
                scores(j + ATTN_AHEAD, (j + ATTN_AHEAD) % ATTN_SLOTS)
            m, acc = consume(j, j % ATTN_SLOTS, m, acc)
        finish(acc)


def _attn_call(bounded, q, k, v):
    B, H, _, S = q.shape
    TQ = ATTN_Q
    assert v.shape[2] * ATTN_K == S and v.shape[4] == ATTN_K
    return pl.pallas_call(
        _attn_kernel,
        grid_spec=pltpu.PrefetchScalarGridSpec(
            num_scalar_prefetch=1,
            grid=(B, H, S // TQ),
            in_specs=[pl.BlockSpec((1, 1, HEAD_PAD, TQ), lambda b, h, i, f: (b, h, 0, i)),
                      pl.BlockSpec((1, 1, S, HEAD_PAD), lambda b, h, i, f: (b, h, 0, 0)),
                      pl.BlockSpec((1, 1) + v.shape[2:], lambda b, h, i, f: (b, h, 0, 0, 0))],
            out_specs=pl.BlockSpec((1, V_DIM, TQ), lambda b, h, i, f: (b, h, i)),
            scratch_shapes=[pltpu.VMEM((ATTN_SLOTS, ATTN_K, TQ), F32)]),
        out_shape=jax.ShapeDtypeStruct((B, MLA_WIDTH, S), F32),
        compiler_params=pltpu.CompilerParams(
            dimension_semantics=("parallel", "parallel", "arbitrary"), vmem_limit_bytes=VMEM_LIMIT),
        name="attn",
    )(bounded, q, k, v)


def _memkv_kernel(mem_ref, mn_ref, wkv_ref, gk_ref, k_ref, v_ref):
    mn = _rms(mem_ref[0], mn_ref[...]).astype(BF16)
    kv = _dot(mn, wkv_ref[...])
    for hh in range(MEM_HEADS):
        kh = kv[:, hh * MEM_HEAD_DIM:(hh + 1) * MEM_HEAD_DIM]
        k_ref[0, hh] = _rms(kh, gk_ref[...]).astype(BF16)
        v_ref[0, hh] = kv[:, MEM_WIDTH + hh * MEM_HEAD_DIM:MEM_WIDTH + (hh + 1) * MEM_HEAD_DIM].astype(BF16)


def _memkv_call(mem, mn, wkv, gk):
    B, M, D = mem.shape
    full = lambda a: pl.BlockSpec(a.shape, lambda b: (0,) * a.ndim)
    hb = pl.BlockSpec((1, MEM_HEADS, M, MEM_HEAD_DIM), lambda b: (b, 0, 0, 0))
    return pl.pallas_call(
        _memkv_kernel,
        grid=(B,),
        in_specs=[pl.BlockSpec((1, M, D), lambda b: (b, 0, 0)), full(mn), full(wkv), full(gk)],
        out_specs=[hb, hb],
        out_shape=[jax.ShapeDtypeStruct((B, MEM_HEADS, M, MEM_HEAD_DIM), BF16)] * 2,
        compiler_params=pltpu.CompilerParams(
            dimension_semantics=("parallel",), vmem_limit_bytes=VMEM_LIMIT),
        name="memkv",
    )(mem, mn, wkv, gk)


def _gelu_tanh(x):
    c = math.sqrt(2.0 / math.pi)
    return x * (0.5 * (1.0 + jnp.tanh(c * (x + 0.044715 * (x * x * x)))))


def _mix_kernel(x_ref, hf_ref, hb_ref, yg_ref, mla_ref, ln_ref, mnn_ref, wol_ref, wom_ref,
                man_ref, wmq_ref, gmq_ref, km_ref, vm_ref, wmo_ref, o_ref):
    x = x_ref[0]
    lru = (hf_ref[0, 0] + hb_ref[0, 0]) * _gelu_tanh(yg_ref[0])
    lru_n = _rms(lru, ln_ref[...]).astype(BF16)
    mt = mla_ref[0]
    mt = mt * lax.rsqrt(jnp.mean(mt * mt, axis=0, keepdims=True) + EPS) * mnn_ref[...]
    mla_n = mt.T.astype(BF16)
    x1 = x + _dot(lru_n, wol_ref[...]) + _dot(mla_n, wom_ref[...])

    hq = _rms(x1, man_ref[...]).astype(BF16)
    qm = _dot(hq, wmq_ref[...])
    scale = MEM_HEAD_DIM ** -0.5
    outs = []
    for hh in range(MEM_HEADS):
        qh = _rms(qm[:, hh * MEM_HEAD_DIM:(hh + 1) * MEM_HEAD_DIM], gmq_ref[...]) * scale
        s = _dot_nt(qh.astype(BF16), km_ref[0, hh])
        p = jnp.exp(s - jnp.max(s, axis=-1, keepdims=True))
        l = jnp.sum(p, axis=-1, keepdims=True)
        outs.append(_dot(p.astype(BF16), vm_ref[0, hh]) / l)
    om = jnp.concatenate(outs, axis=-1).astype(BF16)
    o_ref[0] = x1 + _dot(om, wmo_ref[...])


def _mix_call(x, h2, yg, mla, ln, mnn, wol, wom, man, wmq, gmq, km, vm, wmo):
    B, S, D = x.shape
    T = MIX_ROWS
    M = km.shape[2]
    full = lambda a: pl.BlockSpec(a.shape, lambda b, i: (0,) * a.ndim)
    row = lambda w: pl.BlockSpec((1, T, w), lambda b, i: (b, i, 0))
    hspec = lambda d: pl.BlockSpec((1, 1, T, LRU_WIDTH), lambda b, i: (d, b, i, 0))
    mspec = pl.BlockSpec((1, MEM_HEADS, M, MEM_HEAD_DIM), lambda b, i: (b, 0, 0, 0))
    return pl.pallas_call(
        _mix_kernel,
        grid=(B, S // T),
        in_specs=[row(D), hspec(0), hspec(1), row(LRU_WIDTH),
                  pl.BlockSpec((1, MLA_WIDTH, T), lambda b, i: (b, 0, i)), full(ln), full(mnn),
                  full(wol), full(wom), full(man), full(wmq), full(gmq), mspec, mspec, full(wmo)],
        out_specs=row(D),
        out_shape=jax.ShapeDtypeStruct((B, S, D), F32),
        compiler_params=pltpu.CompilerParams(
            dimension_semantics=("parallel", "parallel"), vmem_limit_bytes=VMEM_LIMIT),
        name="mix",
    )(x, h2, h2, yg, mla, ln, mnn, wol, wom, man, wmq, gmq, km, vm, wmo)


def _ffn_kernel(x_ref, xp_ref, xn_ref, fn_ref, wg_ref, wu_ref, cwg_ref, cwu_ref, cbg_ref, cbu_ref, wd_ref,
                o_ref, xe_ref, g_ref, u_ref, acc_ref):
    T = FFN_ROWS
    i = pl.program_id(1)
    j = pl.program_id(2)

    @pl.when(j == 0)
    def _():
        prev = jnp.where(i == 0, 0.0, xp_ref[0])
        nxt = jnp.where(i == pl.num_programs(1) - 1, 0.0, xn_ref[0])
        g = fn_ref[...]
        xe_ref[0:HALO] = _rms(prev, g).astype(BF16)
        xe_ref[HALO:HALO + T] = _rms(x_ref[0], g).astype(BF16)
        xe_ref[HALO + T:HALO + T + HALO] = _rms(nxt, g).astype(BF16)

    xe = xe_ref[...]
    g_ref[...] = _dot(xe, wg_ref[...])
    u_ref[...] = _dot(xe, wu_ref[...])

    def conv(ref, cw, cb):
        out = cb + cw[0:1] * ref[pl.ds(HALO - 1, T), :]
        out = out + cw[1:2] * ref[pl.ds(HALO, T), :]
        return out + cw[2:3] * ref[pl.ds(HALO + 1, T), :]

    gc = conv(g_ref, cwg_ref[...], cbg_ref[...])
    uc = conv(u_ref, cwu_ref[...], cbu_ref[...])
    act = (gc * jax.nn.sigmoid(gc) * uc).astype(BF16)
    part = _dot(act, wd_ref[...])

    @pl.when(j == 0)
    def _():
        acc_ref[...] = part

    @pl.when(j > 0)
    def _():
        acc_ref[...] += part

    @pl.when(j == pl.num_programs(2) - 1)
    def _():
        o_ref[0] = x_ref[0] + acc_ref[...]


def _ffn_call(x, fn, wup, cw, cb, wd):
    B, S, D = x.shape
    T = FFN_ROWS
    dff = wd.shape[0]
    nf = FFN_SPLIT
    tf = dff // nf
    hb = T // HALO
    last = S // HALO - 1
    return pl.pallas_call(
        _ffn_kernel,
        grid=(B, S // T, nf),
        in_specs=[pl.BlockSpec((1, T, D), lambda b, i, j: (b, i, 0)),
                  pl.BlockSpec((1, HALO, D), lambda b, i, j: (b, jnp.maximum(i * hb - 1, 0), 0)),
                  pl.BlockSpec((1, HALO, D), lambda b, i, j: (b, jnp.minimum((i + 1) * hb, last), 0)),
                  pl.BlockSpec(fn.shape, lambda b, i, j: (0, 0)),
                  pl.BlockSpec((D, tf), lambda b, i, j: (0, j)),
                  pl.BlockSpec((D, tf), lambda b, i, j: (0, j + nf)),
                  pl.BlockSpec((FFN_CONV, tf), lambda b, i, j: (0, j)),
                  pl.BlockSpec((FFN_CONV, tf), lambda b, i, j: (0, j + nf)),
                  pl.BlockSpec((1, tf), lambda b, i, j: (0, j)),
                  pl.BlockSpec((1, tf), lambda b, i, j: (0, j + nf)),
                  pl.BlockSpec((tf, D), lambda b, i, j: (j, 0))],
        out_specs=pl.BlockSpec((1, T, D), lambda b, i, j: (b, i, 0)),
        out_shape=jax.ShapeDtypeStruct((B, S, D), F32),
        scratch_shapes=[pltpu.VMEM((T + 2 * HALO, D), BF16),
                        pltpu.VMEM((T + 2 * HALO, tf), F32),
                        pltpu.VMEM((T + 2 * HALO, tf), F32),
                        pltpu.VMEM((T, D), F32)],
        compiler_params=pltpu.CompilerParams(
            dimension_semantics=("parallel", "parallel", "arbitrary"), vmem_limit_bytes=VMEM_LIMIT),
        name="ffn",
    )(x, x, x, fn, wup, wup, cw, cw, cb, cb, wd)


def _block_diag_dense(w):
    nb, bs, _ = w.shape
    eye = jnp.eye(nb, dtype=w.dtype)
    return (eye[:, None, :, None] * w[:, :, None, :]).reshape(nb * bs, nb * bs)


def _layer(x, mem, pos3, attn_norm, w_in, lru_conv_w, lru_conv_b, lru_w_a, lru_b_a, lru_w_i, lru_b_i,
           lru_lambda, q_a_norm, w_uq, kv_a_norm, w_ukv, mla_q_norm, mla_k_norm, lru_out_norm,
           mla_out_norm, w_out, mem_attn_norm, mem_norm, w_mem_q, w_mem_kv, mem_q_norm, mem_k_norm,
           w_mem_o, ffn_norm, w_up, ffn_conv_w, ffn_conv_b, w_down):
    D = x.shape[-1]
    H = MLA_HEADS
    row = lambda a: a.reshape(1, -1)
    off_kr = 2 * LRU_WIDTH + Q_LORA + KV_LORA
    pad_cols = lambda a, lo, hi: jnp.pad(a, ((0, 0),) * (a.ndim - 1) + ((lo, hi),))

    win = jnp.concatenate([w_in[:, :off_kr], pad_cols(w_in[:, off_kr:], QK_NOPE, HEAD_PAD - QK_HEAD)],
                          axis=1).astype(BF16)
    wuq = pad_cols(w_uq.reshape(Q_LORA, H, QK_HEAD).transpose(1, 0, 2),
                   0, HEAD_PAD - QK_HEAD).transpose(0, 2, 1).astype(BF16)
    wkv = w_ukv.reshape(KV_LORA, H, QK_NOPE + V_DIM).transpose(1, 0, 2)
    wuk = pad_cols(wkv[:, :, :QK_NOPE], 0, HEAD_PAD - QK_NOPE).astype(BF16)
    wuv = pad_cols(wkv[:, :, QK_NOPE:], 0, V_ROWS - V_DIM).transpose(0, 2, 1).astype(BF16)
    gq = pad_cols(row(mla_q_norm), 0, HEAD_PAD - QK_HEAD).reshape(HEAD_PAD, 1)
    gk = pad_cols(row(mla_k_norm), 0, HEAD_PAD - QK_HEAD)
    inv = (ROPE_THETA ** (-jnp.arange(0, QK_ROPE, 2, dtype=F32) / QK_ROPE)).reshape(QK_ROPE // 2, 1)

    xr, yg, q, k, v = _proj_call(x, pos3, row(attn_norm), win, row(q_a_norm), row(kv_a_norm),
                                 wuq, wuk, wuv, gq, gk, inv)

    wa = jax.vmap(_block_diag_dense)(lru_w_a).astype(BF16)
    wi = jax.vmap(_block_diag_dense)(lru_w_i).astype(BF16)
    r3 = lambda a: a.reshape(2, 1, LRU_WIDTH)
    h2 = _lru_call(xr, lru_conv_w, r3(lru_conv_b), wa, r3(lru_b_a), wi, r3(lru_b_i), r3(lru_lambda))

    score_bound = math.sqrt(QK_HEAD) * LOG2E * jnp.max(jnp.abs(mla_q_norm)) * jnp.max(jnp.abs(mla_k_norm))
    bounded = (2.0 * score_bound <= EXP2_SAFE_RANGE).astype(jnp.int32).reshape(1)
    mla = _attn_call(bounded, q, k, v)
    km, vm = _memkv_call(mem, row(mem_norm), w_mem_kv.astype(BF16), row(mem_k_norm))

    wo = w_out.astype(BF16)
    x2 = _mix_call(x, h2, yg, mla, row(lru_out_norm), mla_out_norm.reshape(-1, 1), wo[:LRU_WIDTH], wo[LRU_WIDTH:],
                   row(mem_attn_norm), w_mem_q.astype(BF16), row(mem_q_norm), km, vm, w_mem_o.astype(BF16))

    return _ffn_call(x2, row(ffn_norm), w_up.astype(BF16), ffn_conv_w, row(ffn_conv_b), w_down.astype(BF16))


def kernel(x, mem, positions, attn_norm, w_in, lru_conv_w, lru_conv_b, lru_w_a, lru_b_a, lru_w_i, lru_b_i, lru_lambda, q_a_norm, w_uq, kv_a_norm, w_ukv, mla_q_norm, mla_k_norm, lru_out_norm, mla_out_norm, w_out, mem_attn_norm, mem_norm, w_mem_q, w_mem_kv, mem_q_norm, mem_k_norm, w_mem_o, ffn_norm, w_up, ffn_conv_w, ffn_conv_b, w_down):
    pos3 = positions[:, None, :]
    params = (attn_norm, w_in, lru_conv_w, lru_conv_b, lru_w_a, lru_b_a, lru_w_i, lru_b_i, lru_lambda,
              q_a_norm, w_uq, kv_a_norm, w_ukv, mla_q_norm, mla_k_norm, lru_out_norm, mla_out_norm, w_out,
              mem_attn_norm, mem_norm, w_mem_q, w_mem_kv, mem_q_norm, mem_k_norm, w_mem_o, ffn_norm, w_up,
              ffn_conv_w, ffn_conv_b, w_down)
    for l in range(attn_norm.shape[0]):
        x = _layer(x, mem, pos3, *[p[l] for p in params])
    return x
```

```python
import functools
import math

import jax
import jax.numpy as jnp
from jax import lax
from jax.experimental import pallas as pl
from jax.experimental.pallas import tpu as pltpu

F32 = jnp.float32
BF16 = jnp.bfloat16

EPS = 1e-6
LRU_WIDTH = 512
LRU_BLOCKS = 8
CONV_WIDTH = 4
LRU_C = 8.0
MLA_HEADS = 8
QK_NOPE = 64
QK_ROPE = 32
QK_HEAD = QK_NOPE + QK_ROPE
V_DIM = 64
Q_LORA = 256
KV_LORA = 128
MLA_WIDTH = MLA_HEADS * V_DIM
ROPE_THETA = 10000.0
MEM_HEADS = 4
MEM_HEAD_DIM = 128
MEM_WIDTH = MEM_HEADS * MEM_HEAD_DIM
FFN_CONV = 3

LANES = 128
SUBLANES = 8
MXU_DIM = 256
HEAD_PAD = LANES
BF16_ROWS = 16
V_ROWS = V_DIM + BF16_ROWS
LOG2E = 1.4426950408889634
EXP2_SAFE_RANGE = 100.0

PROJ_ROWS = 512
LRU_ROWS = 512
ATTN_Q = 512
ATTN_K = 512
ATTN_TILES = 4
ATTN_AHEAD = 2
ATTN_SLOTS = ATTN_AHEAD + 1
MIX_ROWS = 512
FFN_ROWS = 512
FFN_SPLIT = 2
HALO = SUBLANES

VMEM_LIMIT = 56 * 1024 * 1024


def _rms(x, g):
    return x * lax.rsqrt(jnp.mean(x * x, axis=-1, keepdims=True) + EPS) * g


def _dot(a, b):
    return jnp.dot(a, b, preferred_element_type=F32)


def _dot_nt(a, b):
    return lax.dot_general(a, b, (((1,), (1,)), ((), ())), preferred_element_type=F32)


def _proj_kernel(x_ref, pos_ref, an_ref, win_ref, qan_ref, kvan_ref, wuq_ref, wuk_ref, wuv_ref,
                 gq_ref, gk_ref, inv_ref,
                 xr_ref, yg_ref, q_ref, k_ref, v_ref):
    x = x_ref[0]
    h = _rms(x, an_ref[...]).astype(BF16)
    proj = _dot(h, win_ref[...])
    xr_ref[0] = proj[:, 0:LRU_WIDTH]
    yg_ref[0] = proj[:, LRU_WIDTH:2 * LRU_WIDTH]
    o = 2 * LRU_WIDTH
    cq_f = _rms(proj[:, o:o + Q_LORA], qan_ref[...])
    o += Q_LORA
    ckv_f = _rms(proj[:, o:o + KV_LORA], kvan_ref[...])
    o += KV_LORA
    kr = proj[:, o:o + HEAD_PAD]
    cq_t = cq_f.T.astype(BF16)
    ckv_t = ckv_f.T.astype(BF16)
    c_kv = ckv_f.astype(BF16)
    T = x.shape[0]
    half = QK_ROPE // 2
    r0, r1, r2 = QK_NOPE, QK_NOPE + half, QK_HEAD

    ang = inv_ref[...] * pos_ref[0].astype(F32)
    cos_t = jnp.cos(ang)
    sin_t = jnp.sin(ang)
    ones = jnp.ones((r0, T), F32)
    zpad = jnp.zeros((HEAD_PAD - r2, T), F32)
    cos_rm = jnp.concatenate([ones, cos_t, cos_t, zpad], axis=0).T
    sin_rm = jnp.concatenate([0.0 * ones, -sin_t, sin_t, zpad], axis=0).T
    lane = lax.broadcasted_iota(jnp.int32, cos_rm.shape, 1)
    first_half = lane < r1

    vrow = lax.broadcasted_iota(jnp.int32, (V_ROWS, T), 0)
    qrow = lax.broadcasted_iota(jnp.int32, (HEAD_PAD, T), 0)
    knorm_max = math.sqrt(QK_HEAD) * jnp.max(jnp.abs(gk_ref[...]), axis=-1, keepdims=True)
    qscale = (QK_HEAD ** -0.5) * LOG2E

    for hh in range(MLA_HEADS):
        qt = _dot(wuq_ref[hh], cq_t)
        ss = jnp.sum(qt * qt, axis=0, keepdims=True) * (1.0 / QK_HEAD)
        tn = qt * lax.rsqrt(ss + EPS) * gq_ref[...]
        t1, t2 = tn[r0:r1], tn[r1:r2]
        qo = jnp.concatenate([tn[0:r0], t1 * cos_t - t2 * sin_t, t1 * sin_t + t2 * cos_t, tn[r2:]], axis=0)
        bound = jnp.sqrt(jnp.sum(tn * tn, axis=0, keepdims=True)) * knorm_max
        q_ref[0, hh, 0] = (jnp.where(qrow == QK_HEAD, -bound, qo) * qscale).astype(BF16)
        kt = _dot(c_kv, wuk_ref[hh]) + kr
        ks = jnp.sum(kt * kt, axis=-1, keepdims=True) * (1.0 / QK_HEAD)
        kn = kt * lax.rsqrt(ks + EPS) * gk_ref[...]
        partner = jnp.where(first_half, pltpu.roll(kn, HEAD_PAD - half, 1), pltpu.roll(kn, half, 1))
        k_ref[0, hh] = jnp.where(lane == QK_HEAD, 1.0, kn * cos_rm + partner * sin_rm).astype(BF16)
        vt = _dot(wuv_ref[hh], ckv_t)
        v_ref[0, hh, 0] = jnp.where(vrow == V_DIM, 1.0, vt).astype(BF16)


def _proj_call(x, pos3, an, win, qan, kvan, wuq, wuk, wuv, gq, gk, inv):
    B, S, D = x.shape
    T = PROJ_ROWS
    H = MLA_HEADS
    full = lambda a: pl.BlockSpec(a.shape, lambda b, i: (0,) * a.ndim)
    row = lambda w: pl.BlockSpec((1, T, w), lambda b, i: (b, i, 0))
    return pl.pallas_call(
        _proj_kernel,
        grid=(B, S // T),
        in_specs=[row(D), pl.BlockSpec((1, 1, T), lambda b, i: (b, 0, i))]
        + [full(a) for a in (an, win, qan, kvan, wuq, wuk, wuv, gq, gk, inv)],
        out_specs=[row(LRU_WIDTH), row(LRU_WIDTH),
                   pl.BlockSpec((1, H, 1, HEAD_PAD, T), lambda b, i: (b, 0, i, 0, 0)),
                   pl.BlockSpec((1, H, T, HEAD_PAD), lambda b, i: (b, 0, i, 0)),
                   pl.BlockSpec((1, H, 1, V_ROWS, T), lambda b, i: (b, 0, i, 0, 0))],
        out_shape=[jax.ShapeDtypeStruct((B, S, LRU_WIDTH), F32),
                   jax.ShapeDtypeStruct((B, S, LRU_WIDTH), F32),
                   jax.ShapeDtypeStruct((B, H, S // T, HEAD_PAD, T), BF16),
                   jax.ShapeDtypeStruct((B, H, S, HEAD_PAD), BF16),
                   jax.ShapeDtypeStruct((B, H, S // T, V_ROWS, T), BF16)],
        compiler_params=pltpu.CompilerParams(
            dimension_semantics=("parallel", "parallel"), vmem_limit_bytes=VMEM_LIMIT),
        name="proj",
    )(x, pos3, an, win, qan, kvan, wuq, wuk, wuv, gq, gk, inv)


def _lru_kernel(x_ref, cw_ref, cb_ref, wa_ref, ba_ref, wi_ref, bi_ref, lam_ref,
                h_ref, px_ref, ph_ref, halo_ref, carry_ref):
    T = LRU_ROWS
    K = T // SUBLANES
    PS = K + SUBLANES
    W = x_ref.shape[2]
    NC = W // LANES
    NH = CONV_WIDTH - 1
    d = pl.program_id(1)
    i = pl.program_id(2)

    @pl.when(i == 0)
    def _():
        halo_ref[...] = jnp.zeros_like(halo_ref)
        carry_ref[...] = jnp.zeros_like(carry_ref)

    def run(reverse):
        x = x_ref[0]
        for c in range(SUBLANES):
            for j in range(NC):
                px_ref[j, c * PS:c * PS + K] = x[c * K:(c + 1) * K, j * LANES:(j + 1) * LANES]
        slabs = [jnp.concatenate([px_ref[j, pl.ds(k, SUBLANES, stride=PS), :] for j in range(NC)], axis=1)
                 for k in range(K)]
        rowi = lax.broadcasted_iota(jnp.int32, (SUBLANES, W), 0)
        if not reverse:
            edge = [slabs[K - NH + m] for m in range(NH)]
            pre = [pltpu.roll(jnp.where(rowi == SUBLANES - 1, halo_ref[m], edge[m]), 1, 0) for m in range(NH)]
            xext = jnp.concatenate(pre + slabs, axis=0)
        else:
            edge = [slabs[m] for m in range(NH)]
            post = [pltpu.roll(jnp.where(rowi == 0, halo_ref[m], edge[m]), SUBLANES - 1, 0) for m in range(NH)]
            xext = jnp.concatenate(slabs + post, axis=0)
        for m in range(NH):
            halo_ref[m] = edge[m]
        cw = cw_ref[0]
        xc = cb_ref[0] + cw[0:1] * xext[0:T]
        for k in range(1, CONV_WIDTH):
            xc = xc + cw[k:k + 1] * xext[SUBLANES * k:SUBLANES * k + T]

        xb = xc.astype(BF16)
        r = jax.nn.sigmoid(_dot(xb, wa_ref[0]) + ba_ref[0])
        ig = jax.nn.sigmoid(_dot(xb, wi_ref[0]) + bi_ref[0])
        z = -lam_ref[0]
        sp = jnp.maximum(z, 0.0) + jnp.log1p(jnp.exp(-jnp.abs(z)))
        a = jnp.exp2(r * ((-LRU_C * LOG2E) * sp))
        mult = jnp.sqrt(1.0 - a * a)
        bb = mult * (ig * xc)

        hk = jnp.zeros((SUBLANES, W), F32)
        pk = jnp.ones((SUBLANES, W), F32)
        h_loc, p_loc = [None] * K, [None] * K
        for k in (range(K - 1, -1, -1) if reverse else range(K)):
            ak = a[SUBLANES * k:SUBLANES * (k + 1)]
            hk = ak * hk + bb[SUBLANES * k:SUBLANES * (k + 1)]
            pk = ak * pk
            h_loc[k], p_loc[k] = hk, pk
        s = 1
        while s < SUBLANES:
            if not reverse:
                keep = rowi >= s
                p_sh = jnp.where(keep, pltpu.roll(pk, s, 0), 1.0)
                h_sh = jnp.where(keep, pltpu.roll(hk, s, 0), 0.0)
            else:
                keep = rowi < SUBLANES - s
                p_sh = jnp.where(keep, pltpu.roll(pk, SUBLANES - s, 0), 1.0)
                h_sh = jnp.where(keep, pltpu.roll(hk, SUBLANES - s, 0), 0.0)
            hk = pk * h_sh + hk
            pk = pk * p_sh
            s *= 2
        carry = carry_ref[...]
        leave = pk * carry + hk
        if not reverse:
            enter = jnp.where(rowi == 0, carry, pltpu.roll(leave, 1, 0))
            carry_ref[...] = jnp.broadcast_to(leave[SUBLANES - 1:SUBLANES], carry.shape)
        else:
            enter = jnp.where(rowi == SUBLANES - 1, carry, pltpu.roll(leave, SUBLANES - 1, 0))
            carry_ref[...] = jnp.broadcast_to(leave[0:1], carry.shape)
        for k in range(K):
            hfull = h_loc[k] + p_loc[k] * enter
            for j in range(NC):
                ph_ref[j, pl.ds(k, SUBLANES, stride=PS), :] = hfull[:, j * LANES:(j + 1) * LANES]
        for c in range(SUBLANES):
            for j in range(NC):
                h_ref[0, 0, c * K:(c + 1) * K, j * LANES:(j + 1) * LANES] = ph_ref[j, c * PS:c * PS + K]

    @pl.when(d == 0)
    def _():
        run(False)

    @pl.when(d == 1)
    def _():
        run(True)


def _lru_call(xr, cw, cb, wa, ba, wi, bi, lam):
    B, S, W = xr.shape
    T = LRU_ROWS
    n = S // T
    tile = lambda d, i: i + d * (n - 1 - 2 * i)
    par = lambda a: pl.BlockSpec((1,) + a.shape[1:], lambda b, d, i: (d,) + (0,) * (a.ndim - 1))
    return pl.pallas_call(
        _lru_kernel,
        grid=(B, 2, n),
        in_specs=[pl.BlockSpec((1, T, W), lambda b, d, i: (b, tile(d, i), 0))]
        + [par(a) for a in (cw, cb, wa, ba, wi, bi, lam)],
        out_specs=pl.BlockSpec((1, 1, T, W), lambda b, d, i: (d, b, tile(d, i), 0)),
        out_shape=jax.ShapeDtypeStruct((2, B, S, W), F32),
        scratch_shapes=[pltpu.VMEM((W // LANES, T + SUBLANES * SUBLANES, LANES), F32),
                        pltpu.VMEM((W // LANES, T + SUBLANES * SUBLANES, LANES), F32),
                        pltpu.VMEM((CONV_WIDTH - 1, SUBLANES, W), F32),
                        pltpu.VMEM((SUBLANES, W), F32)],
        compiler_params=pltpu.CompilerParams(
            dimension_semantics=("parallel", "arbitrary", "arbitrary"), vmem_limit_bytes=VMEM_LIMIT),
        name="lru",
    )(xr, cw, cb, wa, ba, wi, bi, lam)


def _attn_kernel(bounded_ref, q_ref, k_ref, v_ref, o_ref, s_ref):
    TQ, TK = ATTN_Q, ATTN_K
    n = k_ref.shape[2] // TK

    def finish(t, acc):
        o_ref[0, 0, t] = acc[0:V_DIM] / acc[V_DIM:V_DIM + 1]

    def bounded_tile(t, carry):
        qt = q_ref[0, 0, t]
        acc = None
        s = _dot(k_ref[0, 0, 0:TK, :], qt)
        for j in range(n):
            s_next = _dot(k_ref[0, 0, (j + 1) * TK:(j + 2) * TK, :], qt) if j + 1 < n else None
            d = _dot(v_ref[0, 0, j], jnp.exp2(s).astype(BF16))
            acc = d if acc is None else acc + d
            s = s_next
        finish(t, acc)
        return carry

    def online_tile(t, carry):
        qt = q_ref[0, 0, t]

        def scores(j, slot):
            s_ref[slot] = _dot(k_ref[0, 0, j * TK:(j + 1) * TK, :], qt)

        def consume(j, slot, m, acc):
            s = s_ref[slot]
            m_new = jnp.maximum(m, jnp.max(s, axis=0, keepdims=True))
            alpha = jnp.exp2(m - m_new)
            p = jnp.exp2(s - m_new).astype(BF16)
            return m_new, alpha * acc + _dot(v_ref[0, 0, j], p)

        m, acc = jnp.full((1, TQ), -jnp.inf, F32), jnp.zeros((V_ROWS, TQ), F32)
        for j in range(min(ATTN_AHEAD, n)):
            scores(j, j % ATTN_SLOTS)
        for j in range(n):
            if j + ATTN_AHEAD < n:
                scores(j + ATTN_AHEAD, (j + ATTN_AHEAD) % ATTN_SLOTS)
            m, acc = consume(j, j % ATTN_SLOTS, m, acc)
        finish(t, acc)
        return carry

    @pl.when(bounded_ref[0] != 0)
    def _():
        lax.fori_loop(0, ATTN_TILES, bounded_tile, 0)

    @pl.when(bounded_ref[0] == 0)
    def _():
        lax.fori_loop(0, ATTN_TILES, online_tile, 0)


def _attn_call(bounded, q, k, v):
    B, H, NQ, _, TQ = q.shape
    S = NQ * TQ
    assert TQ == ATTN_Q and v.shape[2] * ATTN_K == S and v.shape[4] == ATTN_K
    NT = ATTN_TILES
    return pl.pallas_call(
        _attn_kernel,
        grid_spec=pltpu.PrefetchScalarGridSpec(
            num_scalar_prefetch=1,
            grid=(B, H, NQ // NT),
            in_specs=[pl.BlockSpec((1, 1, NT, HEAD_PAD, TQ), lambda b, h, i, f: (b, h, i, 0, 0)),
                      pl.BlockSpec((1, 1, S, HEAD_PAD), lambda b, h, i, f: (b, h, 0, 0)),
                      pl.BlockSpec((1, 1) + v.shape[2:], lambda b, h, i, f: (b, h, 0, 0, 0))],
            out_specs=pl.BlockSpec((1, 1, NT, V_DIM, TQ), lambda b, h, i, f: (b, h, i, 0, 0)),
            scratch_shapes=[pltpu.VMEM((ATTN_SLOTS, ATTN_K, TQ), F32)]),
        out_shape=jax.ShapeDtypeStruct((B, H, NQ, V_DIM, TQ), F32),
        compiler_params=pltpu.CompilerParams(
            dimension_semantics=("parallel", "parallel", "arbitrary"), vmem_limit_bytes=VMEM_LIMIT),
        name="attn",
    )(bounded, q, k, v)


def _memkv_kernel(mem_ref, mn_ref, wkv_ref, gk_ref, k_ref, v_ref):
    mn = _rms(mem_ref[0], mn_ref[...]).astype(BF16)
    kv = _dot(mn, wkv_ref[...])
    for hh in range(MEM_HEADS):
        kh = kv[:, hh * MEM_HEAD_DIM:(hh + 1) * MEM_HEAD_DIM]
        k_ref[0, hh] = _rms(kh, gk_ref[...]).astype(BF16)
        v_ref[0, hh] = kv[:, MEM_WIDTH + hh * MEM_HEAD_DIM:MEM_WIDTH + (hh + 1) * MEM_HEAD_DIM].astype(BF16)


def _memkv_call(mem, mn, wkv, gk):
    B, M, D = mem.shape
    full = lambda a: pl.BlockSpec(a.shape, lambda b: (0,) * a.ndim)
    hb = pl.BlockSpec((1, MEM_HEADS, M, MEM_HEAD_DIM), lambda b: (b, 0, 0, 0))
    return pl.pallas_call(
        _memkv_kernel,
        grid=(B,),
        in_specs=[pl.BlockSpec((1, M, D), lambda b: (b, 0, 0)), full(mn), full(wkv), full(gk)],
        out_specs=[hb, hb],
        out_shape=[jax.ShapeDtypeStruct((B, MEM_HEADS, M, MEM_HEAD_DIM), BF16)] * 2,
        compiler_params=pltpu.CompilerParams(
            dimension_semantics=("parallel",), vmem_limit_bytes=VMEM_LIMIT),
        name="memkv",
    )(mem, mn, wkv, gk)


def _gelu_tanh(x):
    c = math.sqrt(2.0 / math.pi)
    return x * (0.5 * (1.0 + jnp.tanh(c * (x + 0.044715 * (x * x * x)))))


def _mix_kernel(x_ref, hf_ref, hb_ref, yg_ref, mla_ref, ln_ref, mnn_ref, wol_ref, wom_ref,
                man_ref, wmq_ref, gmq_ref, km_ref, vm_ref, wmo_ref, o_ref):
    x = x_ref[0]
    lru = (hf_ref[0, 0] + hb_ref[0, 0]) * _gelu_tanh(yg_ref[0])
    lru_n = _rms(lru, ln_ref[...]).astype(BF16)
    mt = jnp.concatenate([mla_ref[0, hh, 0] for hh in range(MLA_HEADS)], axis=0)
    mt = mt * lax.rsqrt(jnp.mean(mt * mt, axis=0, keepdims=True) + EPS) * mnn_ref[...]
    mla_n = mt.T.astype(BF16)
    x1 = x + _dot(lru_n, wol_ref[...]) + _dot(mla_n, wom_ref[...])

    hq = _rms(x1, man_ref[...]).astype(BF16)
    qm = _dot(hq, wmq_ref[...])
    scale = MEM_HEAD_DIM ** -0.5
    outs = []
    for hh in range(MEM_HEADS):
        qh = _rms(qm[:, hh * MEM_HEAD_DIM:(hh + 1) * MEM_HEAD_DIM], gmq_ref[...]) * scale
        s = _dot_nt(qh.astype(BF16), km_ref[0, hh])
        p = jnp.exp(s - jnp.max(s, axis=-1, keepdims=True))
        l = jnp.sum(p, axis=-1, keepdims=True)
        outs.append(_dot(p.astype(BF16), vm_ref[0, hh]) / l)
    om = jnp.concatenate(outs, axis=-1).astype(BF16)
    o_ref[0] = x1 + _dot(om, wmo_ref[...])


def _mix_call(x, h2, yg, mla, ln, mnn, wol, wom, man, wmq, gmq, km, vm, wmo):
    B, S, D = x.shape
    T = MIX_ROWS
    M = km.shape[2]
    full = lambda a: pl.BlockSpec(a.shape, lambda b, i: (0,) * a.ndim)
    row = lambda w: pl.BlockSpec((1, T, w), lambda b, i: (b, i, 0))
    hspec = lambda d: pl.BlockSpec((1, 1, T, LRU_WIDTH), lambda b, i: (d, b, i, 0))
    mspec = pl.BlockSpec((1, MEM_HEADS, M, MEM_HEAD_DIM), lambda b, i: (b, 0, 0, 0))
    return pl.pallas_call(
        _mix_kernel,
        grid=(B, S // T),
        in_specs=[row(D), hspec(0), hspec(1), row(LRU_WIDTH),
                  pl.BlockSpec((1, MLA_HEADS, 1, V_DIM, T), lambda b, i: (b, 0, i, 0, 0)), full(ln), full(mnn),
                  full(wol), full(wom), full(man), full(wmq), full(gmq), mspec, mspec, full(wmo)],
        out_specs=row(D),
        out_shape=jax.ShapeDtypeStruct((B, S, D), F32),
        compiler_params=pltpu.CompilerParams(
            dimension_semantics=("parallel", "parallel"), vmem_limit_bytes=VMEM_LIMIT),
        name="mix",
    )(x, h2, h2, yg, mla, ln, mnn, wol, wom, man, wmq, gmq, km, vm, wmo)


def _ffn_kernel(x_ref, xp_ref, xn_ref, fn_ref, wg_ref, wu_ref, cwg_ref, cwu_ref, cbg_ref, cbu_ref, wd_ref,
                o_ref, xe_ref, g_ref, u_ref, acc_ref):
    T = FFN_ROWS
    i = pl.program_id(1)
    j = pl.program_id(2)

    @pl.when(j == 0)
    def _():
        prev = jnp.where(i == 0, 0.0, xp_ref[0])
        nxt = jnp.where(i == pl.num_programs(1) - 1, 0.0, xn_ref[0])
        g = fn_ref[...]
        xe_ref[0:HALO] = _rms(prev, g).astype(BF16)
        xe_ref[HALO:HALO + T] = _rms(x_ref[0], g).astype(BF16)
        xe_ref[HALO + T:HALO + T + HALO] = _rms(nxt, g).astype(BF16)

    xe = xe_ref[...]
    g_ref[...] = _dot(xe, wg_ref[...])
    u_ref[...] = _dot(xe, wu_ref[...])

    def conv(ref, cw, cb):
        out = cb + cw[0:1] * ref[pl.ds(HALO - 1, T), :]
        out = out + cw[1:2] * ref[pl.ds(HALO, T), :]
        return out + cw[2:3] * ref[pl.ds(HALO + 1, T), :]

    gc = conv(g_ref, cwg_ref[...], cbg_ref[...])
    uc = conv(u_ref, cwu_ref[...], cbu_ref[...])
    act = (gc * jax.nn.sigmoid(gc) * uc).astype(BF16)
    part = _dot(act, wd_ref[...])

    @pl.when(j == 0)
    def _():
        acc_ref[...] = part

    @pl.when(j > 0)
    def _():
        acc_ref[...] += part

    @pl.when(j == pl.num_programs(2) - 1)
    def _():
        o_ref[0] = x_ref[0] + acc_ref[...]


def _ffn_call(x, fn, wup, cw, cb, wd):
    B, S, D = x.shape
    T = FFN_ROWS
    dff = wd.shape[0]
    nf = FFN_SPLIT
    tf = dff // nf
    hb = T // HALO
    last = S // HALO - 1
    return pl.pallas_call(
        _ffn_kernel,
        grid=(B, S // T, nf),
        in_specs=[pl.BlockSpec((1, T, D), lambda b, i, j: (b, i, 0)),
                  pl.BlockSpec((1, HALO, D), lambda b, i, j: (b, jnp.maximum(i * hb - 1, 0), 0)),
                  pl.BlockSpec((1, HALO, D), lambda b, i, j: (b, jnp.minimum((i + 1) * hb, last), 0)),
                  pl.BlockSpec(fn.shape, lambda b, i, j: (0, 0)),
                  pl.BlockSpec((D, tf), lambda b, i, j: (0, j)),
                  pl.BlockSpec((D, tf), lambda b, i, j: (0, j + nf)),
                  pl.BlockSpec((FFN_CONV, tf), lambda b, i, j: (0, j)),
                  pl.BlockSpec((FFN_CONV, tf), lambda b, i, j: (0, j + nf)),
                  pl.BlockSpec((1, tf), lambda b, i, j: (0, j)),
                  pl.BlockSpec((1, tf), lambda b, i, j: (0, j + nf)),
                  pl.BlockSpec((tf, D), lambda b, i, j: (j, 0))],
        out_specs=pl.BlockSpec((1, T, D), lambda b, i, j: (b, i, 0)),
        out_shape=jax.ShapeDtypeStruct((B, S, D), F32),
        scratch_shapes=[pltpu.VMEM((T + 2 * HALO, D), BF16),
                        pltpu.VMEM((T + 2 * HALO, tf), F32),
                        pltpu.VMEM((T + 2 * HALO, tf), F32),
                        pltpu.VMEM((T, D), F32)],
        compiler_params=pltpu.CompilerParams(
            dimension_semantics=("parallel", "parallel", "arbitrary"), vmem_limit_bytes=VMEM_LIMIT),
        name="ffn",
    )(x, x, x, fn, wup, wup, cw, cw, cb, cb, wd)


def _block_diag_dense(w):
    nb, bs, _ = w.shape
    eye = jnp.eye(nb, dtype=w.dtype)
    return (eye[:, None, :, None] * w[:, :, None, :]).reshape(nb * bs, nb * bs)


def _layer(x, mem, pos3, attn_norm, w_in, lru_conv_w, lru_conv_b, lru_w_a, lru_b_a, lru_w_i, lru_b_i,
           lru_lambda, q_a_norm, w_uq, kv_a_norm, w_ukv, mla_q_norm, mla_k_norm, lru_out_norm,
           mla_out_norm, w_out, mem_attn_norm, mem_norm, w_mem_q, w_mem_kv, mem_q_norm, mem_k_norm,
           w_mem_o, ffn_norm, w_up, ffn_conv_w, ffn_conv_b, w_down):
    D = x.shape[-1]
    H = MLA_HEADS
    row = lambda a: a.reshape(1, -1)
    off_kr = 2 * LRU_WIDTH + Q_LORA + KV_LORA
    pad_cols = lambda a, lo, hi: jnp.pad(a, ((0, 0),) * (a.ndim - 1) + ((lo, hi),))

    win = jnp.concatenate([w_in[:, :off_kr], pad_cols(w_in[:, off_kr:], QK_NOPE, HEAD_PAD - QK_HEAD)],
                          axis=1).astype(BF16)
    wuq = pad_cols(w_uq.reshape(Q_LORA, H, QK_HEAD).transpose(1, 0, 2),
                   0, HEAD_PAD - QK_HEAD).transpose(0, 2, 1).astype(BF16)
    wkv = w_ukv.reshape(KV_LORA, H, QK_NOPE + V_DIM).transpose(1, 0, 2)
    wuk = pad_cols(wkv[:, :, :QK_NOPE], 0, HEAD_PAD - QK_NOPE).astype(BF16)
    wuv = pad_cols(wkv[:, :, QK_NOPE:], 0, V_ROWS - V_DIM).transpose(0, 2, 1).astype(BF16)
    gq = pad_cols(row(mla_q_norm), 0, HEAD_PAD - QK_HEAD).reshape(HEAD_PAD, 1)
    gk = pad_cols(row(mla_k_norm), 0, HEAD_PAD - QK_HEAD)
    inv = (ROPE_THETA ** (-jnp.arange(0, QK_ROPE, 2, dtype=F32) / QK_ROPE)).reshape(QK_ROPE // 2, 1)

    xr, yg, q, k, v = _proj_call(x, pos3, row(attn_norm), win, row(q_a_norm), row(kv_a_norm),
                                 wuq, wuk, wuv, gq, gk, inv)

    wa = jax.vmap(_block_diag_dense)(lru_w_a).astype(BF16)
    wi = jax.vmap(_block_diag_dense)(lru_w_i).astype(BF16)
    r3 = lambda a: a.reshape(2, 1, LRU_WIDTH)
    h2 = _lru_call(xr, lru_conv_w, r3(lru_conv_b), wa, r3(lru_b_a), wi, r3(lru_b_i), r3(lru_lambda))

    score_bound = math.sqrt(QK_HEAD) * LOG2E * jnp.max(jnp.abs(mla_q_norm)) * jnp.max(jnp.abs(mla_k_norm))
    bounded = (2.0 * score_bound <= EXP2_SAFE_RANGE).astype(jnp.int32).reshape(1)
    mla = _attn_call(bounded, q, k, v)
    km, vm = _memkv_call(mem, row(mem_norm), w_mem_kv.astype(BF16), row(mem_k_norm))

    wo = w_out.astype(BF16)
    x2 = _mix_call(x, h2, yg, mla, row(lru_out_norm), mla_out_norm.reshape(-1, 1), wo[:LRU_WIDTH], wo[LRU_WIDTH:],
                   row(mem_attn_norm), w_mem_q.astype(BF16), row(mem_q_norm), km, vm, w_mem_o.astype(BF16))

    return _ffn_call(x2, row(ffn_norm), w_up.astype(BF16), ffn_conv_w, row(ffn_conv_b), w_down.astype(BF16))


def kernel(x, mem, positions, attn_norm, w_in, lru_conv_w, lru_conv_b, lru_w_a, lru_b_a, lru_w_i, lru_b_i, lru_lambda, q_a_norm, w_uq, kv_a_norm, w_ukv, mla_q_norm, mla_k_norm, lru_out_norm, mla_out_norm, w_out, mem_attn_norm, mem_norm, w_mem_q, w_mem_kv, mem_q_norm, mem_k_norm, w_mem_o, ffn_norm, w_up, ffn_conv_w, ffn_conv_b, w_down):
    pos3 = positions[:, None, :]
    params = (attn_norm, w_in, lru_conv_w, lru_conv_b, lru_w_a, lru_b_a, lru_w_i, lru_b_i, lru_lambda,
              q_a_norm, w_uq, kv_a_norm, w_ukv, mla_q_norm, mla_k_norm, lru_out_norm, mla_out_norm, w_out,
              mem_attn_norm, mem_norm, w_mem_q, w_mem_kv, mem_q_norm, mem_k_norm, w_mem_o, ffn_norm, w_up,
              ffn_conv_w, ffn_conv_b, w_down)
    for l in range(attn_norm.shape[0]):
        x = _layer(x, mem, pos3, *[p[l] for p in params])
    return x
```

```python
import functools
import math

import jax
import jax.numpy as jnp
from jax import lax
from jax.experimental import pallas as pl
from jax.experimental.pallas import tpu as pltpu

F32 = jnp.float32
BF16 = jnp.bfloat16

EPS = 1e-6
LRU_WIDTH = 512
LRU_BLOCKS = 8
CONV_WIDTH = 4
LRU_C = 8.0
MLA_HEADS = 8
QK_NOPE = 64
QK_ROPE = 32
QK_HEAD = QK_NOPE + QK_ROPE
V_DIM = 64
Q_LORA = 256
KV_LORA = 128
MLA_WIDTH = MLA_HEADS * V_DIM
ROPE_THETA = 10000.0
MEM_HEADS = 4
MEM_HEAD_DIM = 128
MEM_WIDTH = MEM_HEADS * MEM_HEAD_DIM
FFN_CONV = 3

LANES = 128
SUBLANES = 8
MXU_DIM = 256
HEAD_PAD = LANES
BF16_ROWS = 16
V_ROWS = V_DIM + BF16_ROWS
LOG2E = 1.4426950408889634
EXP2_SAFE_RANGE = 100.0

PROJ_ROWS = 512
LRU_ROWS = 512
ATTN_Q = 512
ATTN_K = 512
ATTN_TILES = 4
ATTN_AHEAD = 2
ATTN_SLOTS = ATTN_AHEAD + 1
MIX_ROWS = 512
FFN_ROWS = 512
FFN_COLS = MXU_DIM
FFN_SLOTS = 3
HALO = SUBLANES

VMEM_LIMIT = 56 * 1024 * 1024


def _rms(x, g):
    return x * lax.rsqrt(jnp.mean(x * x, axis=-1, keepdims=True) + EPS) * g


def _dot(a, b):
    return jnp.dot(a, b, preferred_element_type=F32)


def _dot_nt(a, b):
    return lax.dot_general(a, b, (((1,), (1,)), ((), ())), preferred_element_type=F32)


def _proj_kernel(x_ref, pos_ref, an_ref, win_ref, qan_ref, kvan_ref, wuq_ref, wuk_ref, wuv_ref,
                 gq_ref, gk_ref, inv_ref,
                 xr_ref, yg_ref, q_ref, k_ref, v_ref):
    x = x_ref[0]
    h = _rms(x, an_ref[...]).astype(BF16)
    proj = _dot(h, win_ref[...])
    xr_ref[0] = proj[:, 0:LRU_WIDTH]
    yg_ref[0] = proj[:, LRU_WIDTH:2 * LRU_WIDTH]
    o = 2 * LRU_WIDTH
    cq_f = _rms(proj[:, o:o + Q_LORA], qan_ref[...])
    o += Q_LORA
    ckv_f = _rms(proj[:, o:o + KV_LORA], kvan_ref[...])
    o += KV_LORA
    kr = proj[:, o:o + HEAD_PAD]
    cq_t = cq_f.T.astype(BF16)
    ckv_t = ckv_f.T.astype(BF16)
    c_kv = ckv_f.astype(BF16)
    T = x.shape[0]
    half = QK_ROPE // 2
    r0, r1, r2 = QK_NOPE, QK_NOPE + half, QK_HEAD

    ang = inv_ref[...] * pos_ref[0].astype(F32)
    cos_t = jnp.cos(ang)
    sin_t = jnp.sin(ang)
    ones = jnp.ones((r0, T), F32)
    zpad = jnp.zeros((HEAD_PAD - r2, T), F32)
    cos_rm = jnp.concatenate([ones, cos_t, cos_t, zpad], axis=0).T
    sin_rm = jnp.concatenate([0.0 * ones, -sin_t, sin_t, zpad], axis=0).T
    lane = lax.broadcasted_iota(jnp.int32, cos_rm.shape, 1)
    first_half = lane < r1

    vrow = lax.broadcasted_iota(jnp.int32, (V_ROWS, T), 0)
    qrow = lax.broadcasted_iota(jnp.int32, (HEAD_PAD, T), 0)
    knorm_max = math.sqrt(QK_HEAD) * jnp.max(jnp.abs(gk_ref[...]), axis=-1, keepdims=True)
    qscale = (QK_HEAD ** -0.5) * LOG2E

    for hh in range(MLA_HEADS):
        qt = _dot(wuq_ref[hh], cq_t)
        ss = jnp.sum(qt * qt, axis=0, keepdims=True) * (1.0 / QK_HEAD)
        tn = qt * lax.rsqrt(ss + EPS) * gq_ref[...]
        t1, t2 = tn[r0:r1], tn[r1:r2]
        qo = jnp.concatenate([tn[0:r0], t1 * cos_t - t2 * sin_t, t1 * sin_t + t2 * cos_t, tn[r2:]], axis=0)
        bound = jnp.sqrt(jnp.sum(tn * tn, axis=0, keepdims=True)) * knorm_max
        q_ref[0, hh, 0] = (jnp.where(qrow == QK_HEAD, -bound, qo) * qscale).astype(BF16)
        kt = _dot(c_kv, wuk_ref[hh]) + kr
        ks = jnp.sum(kt * kt, axis=-1, keepdims=True) * (1.0 / QK_HEAD)
        kn = kt * lax.rsqrt(ks + EPS) * gk_ref[...]
        partner = jnp.where(first_half, pltpu.roll(kn, HEAD_PAD - half, 1), pltpu.roll(kn, half, 1))
        k_ref[0, hh] = jnp.where(lane == QK_HEAD, 1.0, kn * cos_rm + partner * sin_rm).astype(BF16)
        vt = _dot(wuv_ref[hh], ckv_t)
        v_ref[0, hh, 0] = jnp.where(vrow == V_DIM, 1.0, vt).astype(BF16)


def _proj_call(x, pos3, an, win, qan, kvan, wuq, wuk, wuv, gq, gk, inv):
    B, S, D = x.shape
    T = PROJ_ROWS
    H = MLA_HEADS
    full = lambda a: pl.BlockSpec(a.shape, lambda b, i: (0,) * a.ndim)
    row = lambda w: pl.BlockSpec((1, T, w), lambda b, i: (b, i, 0))
    return pl.pallas_call(
        _proj_kernel,
        grid=(B, S // T),
        in_specs=[row(D), pl.BlockSpec((1, 1, T), lambda b, i: (b, 0, i))]
        + [full(a) for a in (an, win, qan, kvan, wuq, wuk, wuv, gq, gk, inv)],
        out_specs=[row(LRU_WIDTH), row(LRU_WIDTH),
                   pl.BlockSpec((1, H, 1, HEAD_PAD, T), lambda b, i: (b, 0, i, 0, 0)),
                   pl.BlockSpec((1, H, T, HEAD_PAD), lambda b, i: (b, 0, i, 0)),
                   pl.BlockSpec((1, H, 1, V_ROWS, T), lambda b, i: (b, 0, i, 0, 0))],
        out_shape=[jax.ShapeDtypeStruct((B, S, LRU_WIDTH), F32),
                   jax.ShapeDtypeStruct((B, S, LRU_WIDTH), F32),
                   jax.ShapeDtypeStruct((B, H, S // T, HEAD_PAD, T), BF16),
                   jax.ShapeDtypeStruct((B, H, S, HEAD_PAD), BF16),
                   jax.ShapeDtypeStruct((B, H, S // T, V_ROWS, T), BF16)],
        compiler_params=pltpu.CompilerParams(
            dimension_semantics=("parallel", "parallel"), vmem_limit_bytes=VMEM_LIMIT),
        name="proj",
    )(x, pos3, an, win, qan, kvan, wuq, wuk, wuv, gq, gk, inv)


def _lru_kernel(x_ref, cw_ref, cb_ref, wa_ref, ba_ref, wi_ref, bi_ref, lam_ref,
                h_ref, px_ref, ph_ref, halo_ref, carry_ref):
    T = LRU_ROWS
    K = T // SUBLANES
    PS = K + SUBLANES
    W = x_ref.shape[2]
    NC = W // LANES
    NH = CONV_WIDTH - 1
    d = pl.program_id(1)
    i = pl.program_id(2)

    @pl.when(i == 0)
    def _():
        halo_ref[...] = jnp.zeros_like(halo_ref)
        carry_ref[...] = jnp.zeros_like(carry_ref)

    def run(reverse):
        x = x_ref[0]
        for c in range(SUBLANES):
            for j in range(NC):
                px_ref[j, c * PS:c * PS + K] = x[c * K:(c + 1) * K, j * LANES:(j + 1) * LANES]
        slabs = [jnp.concatenate([px_ref[j, pl.ds(k, SUBLANES, stride=PS), :] for j in range(NC)], axis=1)
                 for k in range(K)]
        rowi = lax.broadcasted_iota(jnp.int32, (SUBLANES, W), 0)
        if not reverse:
            edge = [slabs[K - NH + m] for m in range(NH)]
            pre = [pltpu.roll(jnp.where(rowi == SUBLANES - 1, halo_ref[m], edge[m]), 1, 0) for m in range(NH)]
            xext = jnp.concatenate(pre + slabs, axis=0)
        else:
            edge = [slabs[m] for m in range(NH)]
            post = [pltpu.roll(jnp.where(rowi == 0, halo_ref[m], edge[m]), SUBLANES - 1, 0) for m in range(NH)]
            xext = jnp.concatenate(slabs + post, axis=0)
        for m in range(NH):
            halo_ref[m] = edge[m]
        cw = cw_ref[0]
        xc = cb_ref[0] + cw[0:1] * xext[0:T]
        for k in range(1, CONV_WIDTH):
            xc = xc + cw[k:k + 1] * xext[SUBLANES * k:SUBLANES * k + T]

        xb = xc.astype(BF16)
        r = jax.nn.sigmoid(_dot(xb, wa_ref[0]) + ba_ref[0])
        ig = jax.nn.sigmoid(_dot(xb, wi_ref[0]) + bi_ref[0])
        z = -lam_ref[0]
        sp = jnp.maximum(z, 0.0) + jnp.log1p(jnp.exp(-jnp.abs(z)))
        a = jnp.exp2(r * ((-LRU_C * LOG2E) * sp))
        mult = jnp.sqrt(1.0 - a * a)
        bb = mult * (ig * xc)

        hk = jnp.zeros((SUBLANES, W), F32)
        pk = jnp.ones((SUBLANES, W), F32)
        h_loc, p_loc = [None] * K, [None] * K
        for k in (range(K - 1, -1, -1) if reverse else range(K)):
            ak = a[SUBLANES * k:SUBLANES * (k + 1)]
            hk = ak * hk + bb[SUBLANES * k:SUBLANES * (k + 1)]
            pk = ak * pk
            h_loc[k], p_loc[k] = hk, pk
        s = 1
        while s < SUBLANES:
            if not reverse:
                keep = rowi >= s
                p_sh = jnp.where(keep, pltpu.roll(pk, s, 0), 1.0)
                h_sh = jnp.where(keep, pltpu.roll(hk, s, 0), 0.0)
            else:
                keep = rowi < SUBLANES - s
                p_sh = jnp.where(keep, pltpu.roll(pk, SUBLANES - s, 0), 1.0)
                h_sh = jnp.where(keep, pltpu.roll(hk, SUBLANES - s, 0), 0.0)
            hk = pk * h_sh + hk
            pk = pk * p_sh
            s *= 2
        carry = carry_ref[...]
        leave = pk * carry + hk
        if not reverse:
            enter = jnp.where(rowi == 0, carry, pltpu.roll(leave, 1, 0))
            carry_ref[...] = jnp.broadcast_to(leave[SUBLANES - 1:SUBLANES], carry.shape)
        else:
            enter = jnp.where(rowi == SUBLANES - 1, carry, pltpu.roll(leave, SUBLANES - 1, 0))
            carry_ref[...] = jnp.broadcast_to(leave[0:1], carry.shape)
        for k in range(K):
            hfull = h_loc[k] + p_loc[k] * enter
            for j in range(NC):
                ph_ref[j, pl.ds(k, SUBLANES, stride=PS), :] = hfull[:, j * LANES:(j + 1) * LANES]
        for c in range(SUBLANES):
            for j in range(NC):
                h_ref[0, 0, c * K:(c + 1) * K, j * LANES:(j + 1) * LANES] = ph_ref[j, c * PS:c * PS + K]

    @pl.when(d == 0)
    def _():
        run(False)

    @pl.when(d == 1)
    def _():
        run(True)


def _lru_call(xr, cw, cb, wa, ba, wi, bi, lam):
    B, S, W = xr.shape
    T = LRU_ROWS
    n = S // T
    tile = lambda d, i: i + d * (n - 1 - 2 * i)
    par = lambda a: pl.BlockSpec((1,) + a.shape[1:], lambda b, d, i: (d,) + (0,) * (a.ndim - 1))
    return pl.pallas_call(
        _lru_kernel,
        grid=(B, 2, n),
        in_specs=[pl.BlockSpec((1, T, W), lambda b, d, i: (b, tile(d, i), 0))]
        + [par(a) for a in (cw, cb, wa, ba, wi, bi, lam)],
        out_specs=pl.BlockSpec((1, 1, T, W), lambda b, d, i: (d, b, tile(d, i), 0)),
        out_shape=jax.ShapeDtypeStruct((2, B, S, W), F32),
        scratch_shapes=[pltpu.VMEM((W // LANES, T + SUBLANES * SUBLANES, LANES), F32),
                        pltpu.VMEM((W // LANES, T + SUBLANES * SUBLANES, LANES), F32),
                        pltpu.VMEM((CONV_WIDTH - 1, SUBLANES, W), F32),
                        pltpu.VMEM((SUBLANES, W), F32)],
        compiler_params=pltpu.CompilerParams(
            dimension_semantics=("parallel", "arbitrary", "arbitrary"), vmem_limit_bytes=VMEM_LIMIT),
        name="lru",
    )(xr, cw, cb, wa, ba, wi, bi, lam)


def _attn_kernel(bounded_ref, q_ref, k_ref, v_ref, o_ref, s_ref):
    TQ, TK = ATTN_Q, ATTN_K
    n = k_ref.shape[2] // TK

    def finish(t, acc):
        o_ref[0, 0, t] = acc[0:V_DIM] / acc[V_DIM:V_DIM + 1]

    def bounded_tile(t, carry):
        qt = q_ref[0, 0, t]
        acc = None
        s = _dot(k_ref[0, 0, 0:TK, :], qt)
        for j in range(n):
            s_next = _dot(k_ref[0, 0, (j + 1) * TK:(j + 2) * TK, :], qt) if j + 1 < n else None
            d = _dot(v_ref[0, 0, j], jnp.exp2(s).astype(BF16))
            acc = d if acc is None else acc + d
            s = s_next
        finish(t, acc)
        return carry

    def online_tile(t, carry):
        qt = q_ref[0, 0, t]

        def scores(j, slot):
            s_ref[slot] = _dot(k_ref[0, 0, j * TK:(j + 1) * TK, :], qt)

        def consume(j, slot, m, acc):
            s = s_ref[slot]
            m_new = jnp.maximum(m, jnp.max(s, axis=0, keepdims=True))
            alpha = jnp.exp2(m - m_new)
            p = jnp.exp2(s - m_new).astype(BF16)
            return m_new, alpha * acc + _dot(v_ref[0, 0, j], p)

        m, acc = jnp.full((1, TQ), -jnp.inf, F32), jnp.zeros((V_ROWS, TQ), F32)
        for j in range(min(ATTN_AHEAD, n)):
            scores(j, j % ATTN_SLOTS)
        for j in range(n):
            if j + ATTN_AHEAD < n:
                scores(j + ATTN_AHEAD, (j + ATTN_AHEAD) % ATTN_SLOTS)
            m, acc = consume(j, j % ATTN_SLOTS, m, acc)
        finish(t, acc)
        return carry

    @pl.when(bounded_ref[0] != 0)
    def _():
        lax.fori_loop(0, ATTN_TILES, bounded_tile, 0)

    @pl.when(bounded_ref[0] == 0)
    def _():
        lax.fori_loop(0, ATTN_TILES, online_tile, 0)


def _attn_call(bounded, q, k, v):
    B, H, NQ, _, TQ = q.shape
    S = NQ * TQ
    assert TQ == ATTN_Q and v.shape[2] * ATTN_K == S and v.shape[4] == ATTN_K
    NT = ATTN_TILES
    return pl.pallas_call(
        _attn_kernel,
        grid_spec=pltpu.PrefetchScalarGridSpec(
            num_scalar_prefetch=1,
            grid=(B, H, NQ // NT),
            in_specs=[pl.BlockSpec((1, 1, NT, HEAD_PAD, TQ), lambda b, h, i, f: (b, h, i, 0, 0)),
                      pl.BlockSpec((1, 1, S, HEAD_PAD), lambda b, h, i, f: (b, h, 0, 0)),
                      pl.BlockSpec((1, 1) + v.shape[2:], lambda b, h, i, f: (b, h, 0, 0, 0))],
            out_specs=pl.BlockSpec((1, 1, NT, V_DIM, TQ), lambda b, h, i, f: (b, h, i, 0, 0)),
            scratch_shapes=[pltpu.VMEM((ATTN_SLOTS, ATTN_K, TQ), F32)]),
        out_shape=jax.ShapeDtypeStruct((B, H, NQ, V_DIM, TQ), F32),
        compiler_params=pltpu.CompilerParams(
            dimension_semantics=("parallel", "parallel", "arbitrary"), vmem_limit_bytes=VMEM_LIMIT),
        name="attn",
    )(bounded, q, k, v)


def _memkv_kernel(mem_ref, mn_ref, wkv_ref, gk_ref, k_ref, v_ref):
    mn = _rms(mem_ref[0], mn_ref[...]).astype(BF16)
    kv = _dot(mn, wkv_ref[...])
    for hh in range(MEM_HEADS):
        kh = kv[:, hh * MEM_HEAD_DIM:(hh + 1) * MEM_HEAD_DIM]
        k_ref[0, hh] = _rms(kh, gk_ref[...]).astype(BF16)
        v_ref[0, hh] = kv[:, MEM_WIDTH + hh * MEM_HEAD_DIM:MEM_WIDTH + (hh + 1) * MEM_HEAD_DIM].astype(BF16)


def _memkv_call(mem, mn, wkv, gk):
    B, M, D = mem.shape
    full = lambda a: pl.BlockSpec(a.shape, lambda b: (0,) * a.ndim)
    hb = pl.BlockSpec((1, MEM_HEADS, M, MEM_HEAD_DIM), lambda b: (b, 0, 0, 0))
    return pl.pallas_call(
        _memkv_kernel,
        grid=(B,),
        in_specs=[pl.BlockSpec((1, M, D), lambda b: (b, 0, 0)), full(mn), full(wkv), full(gk)],
        out_specs=[hb, hb],
        out_shape=[jax.ShapeDtypeStruct((B, MEM_HEADS, M, MEM_HEAD_DIM), BF16)] * 2,
        compiler_params=pltpu.CompilerParams(
            dimension_semantics=("parallel",), vmem_limit_bytes=VMEM_LIMIT),
        name="memkv",
    )(mem, mn, wkv, gk)


def _gelu_tanh(x):
    c = math.sqrt(2.0 / math.pi)
    return x * (0.5 * (1.0 + jnp.tanh(c * (x + 0.044715 * (x * x * x)))))


def _mix_kernel(x_ref, hf_ref, hb_ref, yg_ref, mla_ref, ln_ref, mnn_ref, wol_ref, wom_ref,
                man_ref, wmq_ref, gmq_ref, km_ref, vm_ref, wmo_ref, o_ref):
    x = x_ref[0]
    lru = (hf_ref[0, 0] + hb_ref[0, 0]) * _gelu_tanh(yg_ref[0])
    lru_n = _rms(lru, ln_ref[...]).astype(BF16)
    mt = jnp.concatenate([mla_ref[0, hh, 0] for hh in range(MLA_HEADS)], axis=0)
    mt = mt * lax.rsqrt(jnp.mean(mt * mt, axis=0, keepdims=True) + EPS) * mnn_ref[...]
    mla_n = mt.T.astype(BF16)
    x1 = x + _dot(lru_n, wol_ref[...]) + _dot(mla_n, wom_ref[...])

    hq = _rms(x1, man_ref[...]).astype(BF16)
    qm = _dot(hq, wmq_ref[...])
    scale = MEM_HEAD_DIM ** -0.5
    outs = []
    for hh in range(MEM_HEADS):
        qh = _rms(qm[:, hh * MEM_HEAD_DIM:(hh + 1) * MEM_HEAD_DIM], gmq_ref[...]) * scale
        s = _dot_nt(qh.astype(BF16), km_ref[0, hh])
        p = jnp.exp(s - jnp.max(s, axis=-1, keepdims=True))
        l = jnp.sum(p, axis=-1, keepdims=True)
        outs.append(_dot(p.astype(BF16), vm_ref[0, hh]) / l)
    om = jnp.concatenate(outs, axis=-1).astype(BF16)
    o_ref[0] = x1 + _dot(om, wmo_ref[...])


def _mix_call(x, h2, yg, mla, ln, mnn, wol, wom, man, wmq, gmq, km, vm, wmo):
    B, S, D = x.shape
    T = MIX_ROWS
    M = km.shape[2]
    full = lambda a: pl.BlockSpec(a.shape, lambda b, i: (0,) * a.ndim)
    row = lambda w: pl.BlockSpec((1, T, w), lambda b, i: (b, i, 0))
    hspec = lambda d: pl.BlockSpec((1, 1, T, LRU_WIDTH), lambda b, i: (d, b, i, 0))
    mspec = pl.BlockSpec((1, MEM_HEADS, M, MEM_HEAD_DIM), lambda b, i: (b, 0, 0, 0))
    return pl.pallas_call(
        _mix_kernel,
        grid=(B, S // T),
        in_specs=[row(D), hspec(0), hspec(1), row(LRU_WIDTH),
                  pl.BlockSpec((1, MLA_HEADS, 1, V_DIM, T), lambda b, i: (b, 0, i, 0, 0)), full(ln), full(mnn),
                  full(wol), full(wom), full(man), full(wmq), full(gmq), mspec, mspec, full(wmo)],
        out_specs=row(D),
        out_shape=jax.ShapeDtypeStruct((B, S, D), F32),
        compiler_params=pltpu.CompilerParams(
            dimension_semantics=("parallel", "parallel"), vmem_limit_bytes=VMEM_LIMIT),
        name="mix",
    )(x, h2, h2, yg, mla, ln, mnn, wol, wom, man, wmq, gmq, km, vm, wmo)


def _ffn_kernel(x_ref, xp_ref, xn_ref, fn_ref, wup_ref, cw_ref, cb_ref, wd_ref, o_ref,
                pm_ref, res_ref, xe_ref, g_ref, u_ref, act_ref):
    T = FFN_ROWS
    K = T // SUBLANES
    PS = K + SUBLANES
    CB = FFN_COLS
    NS = FFN_SLOTS
    D = x_ref.shape[2]
    NC = D // LANES
    dff = wd_ref.shape[0]
    nb = dff // CB
    i = pl.program_id(1)

    x = x_ref[0]
    for c in range(SUBLANES):
        for j in range(NC):
            pm_ref[j, c * PS:c * PS + K] = x[c * K:(c + 1) * K, j * LANES:(j + 1) * LANES]
    gain = fn_ref[...]
    for k in range(K):
        xk = jnp.concatenate([pm_ref[j, pl.ds(k, SUBLANES, stride=PS), :] for j in range(NC)], axis=1)
        res_ref[SUBLANES * k:SUBLANES * (k + 1)] = xk
        xe_ref[SUBLANES * k:SUBLANES * (k + 1)] = _rms(xk, gain).astype(BF16)
    prev = jnp.where(i == 0, 0.0, xp_ref[0])
    nxt = jnp.where(i == pl.num_programs(1) - 1, 0.0, xn_ref[0])
    xe_ref[T:T + HALO] = _rms(prev, gain).astype(BF16)
    xe_ref[T + HALO:T + 2 * HALO] = _rms(nxt, gain).astype(BF16)

    def up(c):
        xe = xe_ref[...]
        g_ref[c % NS] = _dot(xe, wup_ref[:, c * CB:(c + 1) * CB])
        u_ref[c % NS] = _dot(xe, wup_ref[:, dff + c * CB:dff + (c + 1) * CB])

    rowi = lax.broadcasted_iota(jnp.int32, (SUBLANES, CB), 0)

    def conv(ref, slot, c0):
        first, last = ref[slot, 0:SUBLANES], ref[slot, T - SUBLANES:T]
        before = jnp.where(rowi == SUBLANES - 1, ref[slot, T:T + HALO], last)
        after = jnp.where(rowi == 0, ref[slot, T + HALO:T + 2 * HALO], first)
        gm = jnp.concatenate([pltpu.roll(before, 1, 0), ref[slot, 0:T - SUBLANES]], axis=0)
        gp = jnp.concatenate([ref[slot, SUBLANES:T], pltpu.roll(after, SUBLANES - 1, 0)], axis=0)
        cw = cw_ref[:, c0:c0 + CB]
        return cb_ref[:, c0:c0 + CB] + cw[0:1] * gm + cw[1:2] * ref[slot, 0:T] + cw[2:3] * gp

    for c in range(min(NS - 1, nb)):
        up(c)
    for c in range(nb):
        if c + NS - 1 < nb:
            up(c + NS - 1)
        gc = conv(g_ref, c % NS, c * CB)
        uc = conv(u_ref, c % NS, dff + c * CB)
        act_ref[:, c * CB:(c + 1) * CB] = (gc * jax.nn.sigmoid(gc) * uc).astype(BF16)
    y = res_ref[...] + _dot(act_ref[...], wd_ref[...])
    for k in range(K):
        for j in range(NC):
            pm_ref[j, pl.ds(k, SUBLANES, stride=PS), :] = y[SUBLANES * k:SUBLANES * (k + 1),
                                                            j * LANES:(j + 1) * LANES]
    for c in range(SUBLANES):
        for j in range(NC):
            o_ref[0, c * K:(c + 1) * K, j * LANES:(j + 1) * LANES] = pm_ref[j, c * PS:c * PS + K]


def _ffn_call(x, fn, wup, cw, cb, wd):
    B, S, D = x.shape
    T = FFN_ROWS
    hb = T // HALO
    last = S // HALO - 1
    full = lambda a: pl.BlockSpec(a.shape, lambda b, i: (0,) * a.ndim, pipeline_mode=pl.Buffered(1))
    return pl.pallas_call(
        _ffn_kernel,
        grid=(B, S // T),
        in_specs=[pl.BlockSpec((1, T, D), lambda b, i: (b, i, 0)),
                  pl.BlockSpec((1, HALO, D), lambda b, i: (b, jnp.maximum(i * hb - 1, 0), 0)),
                  pl.BlockSpec((1, HALO, D), lambda b, i: (b, jnp.minimum((i + 1) * hb, last), 0)),
                  full(fn), full(wup), full(cw), full(cb), full(wd)],
        out_specs=pl.BlockSpec((1, T, D), lambda b, i: (b, i, 0)),
        out_shape=jax.ShapeDtypeStruct((B, S, D), F32),
        scratch_shapes=[pltpu.VMEM((D // LANES, T + SUBLANES * SUBLANES, LANES), F32),
                        pltpu.VMEM((T, D), F32),
                        pltpu.VMEM((T + 2 * HALO, D), BF16),
                        pltpu.VMEM((FFN_SLOTS, T + 2 * HALO, FFN_COLS), F32),
                        pltpu.VMEM((FFN_SLOTS, T + 2 * HALO, FFN_COLS), F32),
                        pltpu.VMEM((T, wd.shape[0]), BF16)],
        compiler_params=pltpu.CompilerParams(
            dimension_semantics=("parallel", "parallel"), vmem_limit_bytes=VMEM_LIMIT),
        name="ffn",
    )(x, x, x, fn, wup, cw, cb, wd)


def _block_diag_dense(w):
    nb, bs, _ = w.shape
    eye = jnp.eye(nb, dtype=w.dtype)
    return (eye[:, None, :, None] * w[:, :, None, :]).reshape(nb * bs, nb * bs)


def _layer(x, mem, pos3, attn_norm, w_in, lru_conv_w, lru_conv_b, lru_w_a, lru_b_a, lru_w_i, lru_b_i,
           lru_lambda, q_a_norm, w_uq, kv_a_norm, w_ukv, mla_q_norm, mla_k_norm, lru_out_norm,
           mla_out_norm, w_out, mem_attn_norm, mem_norm, w_mem_q, w_mem_kv, mem_q_norm, mem_k_norm,
           w_mem_o, ffn_norm, w_up, ffn_conv_w, ffn_conv_b, w_down):
    D = x.shape[-1]
    H = MLA_HEADS
    row = lambda a: a.reshape(1, -1)
    off_kr = 2 * LRU_WIDTH + Q_LORA + KV_LORA
    pad_cols = lambda a, lo, hi: jnp.pad(a, ((0, 0),) * (a.ndim - 1) + ((lo, hi),))

    win = jnp.concatenate([w_in[:, :off_kr], pad_cols(w_in[:, off_kr:], QK_NOPE, HEAD_PAD - QK_HEAD)],
                          axis=1).astype(BF16)
    wuq = pad_cols(w_uq.reshape(Q_LORA, H, QK_HEAD).transpose(1, 0, 2),
                   0, HEAD_PAD - QK_HEAD).transpose(0, 2, 1).astype(BF16)
    wkv = w_ukv.reshape(KV_LORA, H, QK_NOPE + V_DIM).transpose(1, 0, 2)
    wuk = pad_cols(wkv[:, :, :QK_NOPE], 0, HEAD_PAD - QK_NOPE).astype(BF16)
    wuv = pad_cols(wkv[:, :, QK_NOPE:], 0, V_ROWS - V_DIM).transpose(0, 2, 1).astype(BF16)
    gq = pad_cols(row(mla_q_norm), 0, HEAD_PAD - QK_HEAD).reshape(HEAD_PAD, 1)
    gk = pad_cols(row(mla_k_norm), 0, HEAD_PAD - QK_HEAD)
    inv = (ROPE_THETA ** (-jnp.arange(0, QK_ROPE, 2, dtype=F32) / QK_ROPE)).reshape(QK_ROPE // 2, 1)

    xr, yg, q, k, v = _proj_call(x, pos3, row(attn_norm), win, row(q_a_norm), row(kv_a_norm),
                                 wuq, wuk, wuv, gq, gk, inv)

    wa = jax.vmap(_block_diag_dense)(lru_w_a).astype(BF16)
    wi = jax.vmap(_block_diag_dense)(lru_w_i).astype(BF16)
    r3 = lambda a: a.reshape(2, 1, LRU_WIDTH)
    h2 = _lru_call(xr, lru_conv_w, r3(lru_conv_b), wa, r3(lru_b_a), wi, r3(lru_b_i), r3(lru_lambda))

    score_bound = math.sqrt(QK_HEAD) * LOG2E * jnp.max(jnp.abs(mla_q_norm)) * jnp.max(jnp.abs(mla_k_norm))
    bounded = (2.0 * score_bound <= EXP2_SAFE_RANGE).astype(jnp.int32).reshape(1)
    mla = _attn_call(bounded, q, k, v)
    km, vm = _memkv_call(mem, row(mem_norm), w_mem_kv.astype(BF16), row(mem_k_norm))

    wo = w_out.astype(BF16)
    x2 = _mix_call(x, h2, yg, mla, row(lru_out_norm), mla_out_norm.reshape(-1, 1), wo[:LRU_WIDTH], wo[LRU_WIDTH:],
                   row(mem_attn_norm), w_mem_q.astype(BF16), row(mem_q_norm), km, vm, w_mem_o.astype(BF16))

    return _ffn_call(x2, row(ffn_norm), w_up.astype(BF16), ffn_conv_w, row(ffn_conv_b), w_down.astype(BF16))


def kernel(x, mem, positions, attn_norm, w_in, lru_conv_w, lru_conv_b, lru_w_a, lru_b_a, lru_w_i, lru_b_i, lru_lambda, q_a_norm, w_uq, kv_a_norm, w_ukv, mla_q_norm, mla_k_norm, lru_out_norm, mla_out_norm, w_out, mem_attn_norm, mem_norm, w_mem_q, w_mem_kv, mem_q_norm, mem_k_norm, w_mem_o, ffn_norm, w_up, ffn_conv_w, ffn_conv_b, w_down):
    pos3 = positions[:, None, :]
    params = (attn_norm, w_in, lru_conv_w, lru_conv_b, lru_w_a, lru_b_a, lru_w_i, lru_b_i, lru_lambda,
              q_a_norm, w_uq, kv_a_norm, w_ukv, mla_q_norm, mla_k_norm, lru_out_norm, mla_out_norm, w_out,
              mem_attn_norm, mem_norm, w_mem_q, w_mem_kv, mem_q_norm, mem_k_norm, w_mem_o, ffn_norm, w_up,
              ffn_conv_w, ffn_conv_b, w_down)
    for l in range(attn_norm.shape[0]):
        x = _layer(x, mem, pos3, *[p[l] for p in params])
    return x
```

```python
import functools
import math

import jax
import jax.numpy as jnp
from jax import lax
from jax.experimental import pallas as pl
from jax.experimental.pallas import tpu as pltpu

F32 = jnp.float32
BF16 = jnp.bfloat16

EPS = 1e-6
LRU_WIDTH = 512
LRU_BLOCKS = 8
CONV_WIDTH = 4
LRU_C = 8.0
MLA_HEADS = 8
QK_NOPE = 64
QK_ROPE = 32
QK_HEAD = QK_NOPE + QK_ROPE
V_DIM = 64
Q_LORA = 256
KV_LORA = 128
MLA_WIDTH = MLA_HEADS * V_DIM
ROPE_THETA = 10000.0
MEM_HEADS = 4
MEM_HEAD_DIM = 128
MEM_WIDTH = MEM_HEADS * MEM_HEAD_DIM
FFN_CONV = 3

LANES = 128
SUBLANES = 8
MXU_DIM = 256
HEAD_PAD = LANES
BF16_ROWS = 16
V_ROWS = V_DIM + BF16_ROWS
LOG2E = 1.4426950408889634
EXP2_SAFE_RANGE = 100.0

PROJ_ROWS = 512
LRU_ROWS = 512
ATTN_Q = 512
ATTN_K = 512
ATTN_SPLIT = 2
ATTN_FAST_AHEAD = 2
ATTN_TILES = 4
ATTN_AHEAD = 2
ATTN_SLOTS = ATTN_AHEAD + 1
MIX_ROWS = 512
FFN_ROWS = 512
FFN_COLS = MXU_DIM
FFN_SLOTS = 3
HALO = SUBLANES

VMEM_LIMIT = 56 * 1024 * 1024


def _rms(x, g):
    return x * lax.rsqrt(jnp.mean(x * x, axis=-1, keepdims=True) + EPS) * g


def _dot(a, b):
    return jnp.dot(a, b, preferred_element_type=F32)


def _dot_nt(a, b):
    return lax.dot_general(a, b, (((1,), (1,)), ((), ())), preferred_element_type=F32)


def _proj_kernel(x_ref, pos_ref, an_ref, win_ref, qan_ref, kvan_ref, wuq_ref, wuk_ref, wuv_ref,
                 gq_ref, gk_ref, inv_ref,
                 xr_ref, yg_ref, q_ref, k_ref, v_ref):
    x = x_ref[0]
    h = _rms(x, an_ref[...]).astype(BF16)
    proj = _dot(h, win_ref[...])
    xr_ref[0] = proj[:, 0:LRU_WIDTH]
    yg_ref[0] = proj[:, LRU_WIDTH:2 * LRU_WIDTH]
    o = 2 * LRU_WIDTH
    cq_f = _rms(proj[:, o:o + Q_LORA], qan_ref[...])
    o += Q_LORA
    ckv_f = _rms(proj[:, o:o + KV_LORA], kvan_ref[...])
    o += KV_LORA
    kr = proj[:, o:o + HEAD_PAD]
    cq_t = cq_f.T.astype(BF16)
    ckv_t = ckv_f.T.astype(BF16)
    c_kv = ckv_f.astype(BF16)
    T = x.shape[0]
    half = QK_ROPE // 2
    r0, r1, r2 = QK_NOPE, QK_NOPE + half, QK_HEAD

    ang = inv_ref[...] * pos_ref[0].astype(F32)
    cos_t = jnp.cos(ang)
    sin_t = jnp.sin(ang)
    ones = jnp.ones((r0, T), F32)
    zpad = jnp.zeros((HEAD_PAD - r2, T), F32)
    cos_rm = jnp.concatenate([ones, cos_t, cos_t, zpad], axis=0).T
    sin_rm = jnp.concatenate([0.0 * ones, -sin_t, sin_t, zpad], axis=0).T
    lane = lax.broadcasted_iota(jnp.int32, cos_rm.shape, 1)
    first_half = lane < r1

    vrow = lax.broadcasted_iota(jnp.int32, (V_ROWS, T), 0)
    qrow = lax.broadcasted_iota(jnp.int32, (HEAD_PAD, T), 0)
    knorm_max = math.sqrt(QK_HEAD) * jnp.max(jnp.abs(gk_ref[...]), axis=-1, keepdims=True)
    qscale = (QK_HEAD ** -0.5) * LOG2E

    for hh in range(MLA_HEADS):
        qt = _dot(wuq_ref[hh], cq_t)
        ss = jnp.sum(qt * qt, axis=0, keepdims=True) * (1.0 / QK_HEAD)
        tn = qt * lax.rsqrt(ss + EPS) * gq_ref[...]
        t1, t2 = tn[r0:r1], tn[r1:r2]
        qo = jnp.concatenate([tn[0:r0], t1 * cos_t - t2 * sin_t, t1 * sin_t + t2 * cos_t, tn[r2:]], axis=0)
        bound = jnp.sqrt(jnp.sum(tn * tn, axis=0, keepdims=True)) * knorm_max
        q_ref[0, hh, 0] = (jnp.where(qrow == QK_HEAD, -bound, qo) * qscale).astype(BF16)
        kt = _dot(c_kv, wuk_ref[hh]) + kr
        ks = jnp.sum(kt * kt, axis=-1, keepdims=True) * (1.0 / QK_HEAD)
        kn = kt * lax.rsqrt(ks + EPS) * gk_ref[...]
        partner = jnp.where(first_half, pltpu.roll(kn, HEAD_PAD - half, 1), pltpu.roll(kn, half, 1))
        k_ref[0, hh] = jnp.where(lane == QK_HEAD, 1.0, kn * cos_rm + partner * sin_rm).astype(BF16)
        vt = _dot(wuv_ref[hh], ckv_t)
        v_ref[0, hh, 0] = jnp.where(vrow == V_DIM, 1.0, vt).astype(BF16)


def _proj_call(x, pos3, an, win, qan, kvan, wuq, wuk, wuv, gq, gk, inv):
    B, S, D = x.shape
    T = PROJ_ROWS
    H = MLA_HEADS
    full = lambda a: pl.BlockSpec(a.shape, lambda b, i: (0,) * a.ndim)
    row = lambda w: pl.BlockSpec((1, T, w), lambda b, i: (b, i, 0))
    return pl.pallas_call(
        _proj_kernel,
        grid=(B, S // T),
        in_specs=[row(D), pl.BlockSpec((1, 1, T), lambda b, i: (b, 0, i))]
        + [full(a) for a in (an, win, qan, kvan, wuq, wuk, wuv, gq, gk, inv)],
        out_specs=[row(LRU_WIDTH), row(LRU_WIDTH),
                   pl.BlockSpec((1, H, 1, HEAD_PAD, T), lambda b, i: (b, 0, i, 0, 0)),
                   pl.BlockSpec((1, H, T, HEAD_PAD), lambda b, i: (b, 0, i, 0)),
                   pl.BlockSpec((1, H, 1, V_ROWS, T), lambda b, i: (b, 0, i, 0, 0))],
        out_shape=[jax.ShapeDtypeStruct((B, S, LRU_WIDTH), F32),
                   jax.ShapeDtypeStruct((B, S, LRU_WIDTH), F32),
                   jax.ShapeDtypeStruct((B, H, S // T, HEAD_PAD, T), BF16),
                   jax.ShapeDtypeStruct((B, H, S, HEAD_PAD), BF16),
                   jax.ShapeDtypeStruct((B, H, S // T, V_ROWS, T), BF16)],
        compiler_params=pltpu.CompilerParams(
            dimension_semantics=("parallel", "parallel"), vmem_limit_bytes=VMEM_LIMIT),
        name="proj",
    )(x, pos3, an, win, qan, kvan, wuq, wuk, wuv, gq, gk, inv)


def _lru_kernel(x_ref, cw_ref, cb_ref, wa_ref, ba_ref, wi_ref, bi_ref, lam_ref,
                h_ref, px_ref, ph_ref, halo_ref, carry_ref):
    T = LRU_ROWS
    K = T // SUBLANES
    PS = K + SUBLANES
    W = x_ref.shape[2]
    NC = W // LANES
    NH = CONV_WIDTH - 1
    d = pl.program_id(1)
    i = pl.program_id(2)

    @pl.when(i == 0)
    def _():
        halo_ref[...] = jnp.zeros_like(halo_ref)
        carry_ref[...] = jnp.zeros_like(carry_ref)

    def run(reverse):
        x = x_ref[0]
        for c in range(SUBLANES):
            for j in range(NC):
                px_ref[j, c * PS:c * PS + K] = x[c * K:(c + 1) * K, j * LANES:(j + 1) * LANES]
        slabs = [jnp.concatenate([px_ref[j, pl.ds(k, SUBLANES, stride=PS), :] for j in range(NC)], axis=1)
                 for k in range(K)]
        rowi = lax.broadcasted_iota(jnp.int32, (SUBLANES, W), 0)
        if not reverse:
            edge = [slabs[K - NH + m] for m in range(NH)]
            pre = [pltpu.roll(jnp.where(rowi == SUBLANES - 1, halo_ref[m], edge[m]), 1, 0) for m in range(NH)]
            xext = jnp.concatenate(pre + slabs, axis=0)
        else:
            edge = [slabs[m] for m in range(NH)]
            post = [pltpu.roll(jnp.where(rowi == 0, halo_ref[m], edge[m]), SUBLANES - 1, 0) for m in range(NH)]
            xext = jnp.concatenate(slabs + post, axis=0)
        for m in range(NH):
            halo_ref[m] = edge[m]
        cw = cw_ref[0]
        xc = cb_ref[0] + cw[0:1] * xext[0:T]
        for k in range(1, CONV_WIDTH):
            xc = xc + cw[k:k + 1] * xext[SUBLANES * k:SUBLANES * k + T]

        xb = xc.astype(BF16)
        r = jax.nn.sigmoid(_dot(xb, wa_ref[0]) + ba_ref[0])
        ig = jax.nn.sigmoid(_dot(xb, wi_ref[0]) + bi_ref[0])
        z = -lam_ref[0]
        sp = jnp.maximum(z, 0.0) + jnp.log1p(jnp.exp(-jnp.abs(z)))
        a = jnp.exp2(r * ((-LRU_C * LOG2E) * sp))
        mult = jnp.sqrt(1.0 - a * a)
        bb = mult * (ig * xc)

        hk = jnp.zeros((SUBLANES, W), F32)
        pk = jnp.ones((SUBLANES, W), F32)
        h_loc, p_loc = [None] * K, [None] * K
        for k in (range(K - 1, -1, -1) if reverse else range(K)):
            ak = a[SUBLANES * k:SUBLANES * (k + 1)]
            hk = ak * hk + bb[SUBLANES * k:SUBLANES * (k + 1)]
            pk = ak * pk
            h_loc[k], p_loc[k] = hk, pk
        s = 1
        while s < SUBLANES:
            if not reverse:
                keep = rowi >= s
                p_sh = jnp.where(keep, pltpu.roll(pk, s, 0), 1.0)
                h_sh = jnp.where(keep, pltpu.roll(hk, s, 0), 0.0)
            else:
                keep = rowi < SUBLANES - s
                p_sh = jnp.where(keep, pltpu.roll(pk, SUBLANES - s, 0), 1.0)
                h_sh = jnp.where(keep, pltpu.roll(hk, SUBLANES - s, 0), 0.0)
            hk = pk * h_sh + hk
            pk = pk * p_sh
            s *= 2
        carry = carry_ref[...]
        leave = pk * carry + hk
        if not reverse:
            enter = jnp.where(rowi == 0, carry, pltpu.roll(leave, 1, 0))
            carry_ref[...] = jnp.broadcast_to(leave[SUBLANES - 1:SUBLANES], carry.shape)
        else:
            enter = jnp.where(rowi == SUBLANES - 1, carry, pltpu.roll(leave, SUBLANES - 1, 0))
            carry_ref[...] = jnp.broadcast_to(leave[0:1], carry.shape)
        for k in range(K):
            hfull = h_loc[k] + p_loc[k] * enter
            for j in range(NC):
                ph_ref[j, pl.ds(k, SUBLANES, stride=PS), :] = hfull[:, j * LANES:(j + 1) * LANES]
        for c in range(SUBLANES):
            for j in range(NC):
                h_ref[0, 0, c * K:(c + 1) * K, j * LANES:(j + 1) * LANES] = ph_ref[j, c * PS:c * PS + K]

    @pl.when(d == 0)
    def _():
        run(False)

    @pl.when(d == 1)
    def _():
        run(True)


def _lru_call(xr, cw, cb, wa, ba, wi, bi, lam):
    B, S, W = xr.shape
    T = LRU_ROWS
    n = S // T
    tile = lambda d, i: i + d * (n - 1 - 2 * i)
    par = lambda a: pl.BlockSpec((1,) + a.shape[1:], lambda b, d, i: (d,) + (0,) * (a.ndim - 1))
    return pl.pallas_call(
        _lru_kernel,
        grid=(B, 2, n),
        in_specs=[pl.BlockSpec((1, T, W), lambda b, d, i: (b, tile(d, i), 0))]
        + [par(a) for a in (cw, cb, wa, ba, wi, bi, lam)],
        out_specs=pl.BlockSpec((1, 1, T, W), lambda b, d, i: (d, b, tile(d, i), 0)),
        out_shape=jax.ShapeDtypeStruct((2, B, S, W), F32),
        scratch_shapes=[pltpu.VMEM((W // LANES, T + SUBLANES * SUBLANES, LANES), F32),
                        pltpu.VMEM((W // LANES, T + SUBLANES * SUBLANES, LANES), F32),
                        pltpu.VMEM((CONV_WIDTH - 1, SUBLANES, W), F32),
                        pltpu.VMEM((SUBLANES, W), F32)],
        compiler_params=pltpu.CompilerParams(
            dimension_semantics=("parallel", "arbitrary", "arbitrary"), vmem_limit_bytes=VMEM_LIMIT),
        name="lru",
    )(xr, cw, cb, wa, ba, wi, bi, lam)


def _attn_kernel(bounded_ref, q_ref, k_ref, v_ref, o_ref, s_ref):
    TQ, TK = ATTN_Q, ATTN_K
    n = k_ref.shape[2] // TK

    def finish(t, acc):
        o_ref[0, 0, t] = acc[0:V_DIM] / acc[V_DIM:V_DIM + 1]

    def bounded_tile(t, carry):
        qt = q_ref[0, 0, t]
        acc = None
        TG = TK // ATTN_SPLIT
        ng = n * ATTN_SPLIT
        ahead = ATTN_FAST_AHEAD
        queue = [_dot(k_ref[0, 0, j * TG:(j + 1) * TG, :], qt) for j in range(ahead)]
        for j in range(ng):
            if j + ahead < ng:
                queue.append(_dot(k_ref[0, 0, (j + ahead) * TG:(j + ahead + 1) * TG, :], qt))
            s = queue.pop(0)
            part = j % ATTN_SPLIT
            vt = v_ref[0, 0, j // ATTN_SPLIT, :, part * TG:(part + 1) * TG]
            d = _dot(vt, jnp.exp2(s).astype(BF16))
            acc = d if acc is None else acc + d
        finish(t, acc)
        return carry

    def online_tile(t, carry):
        qt = q_ref[0, 0, t]

        def scores(j, slot):
            s_ref[slot] = _dot(k_ref[0, 0, j * TK:(j + 1) * TK, :], qt)

        def consume(j, slot, m, acc):
            s = s_ref[slot]
            m_new = jnp.maximum(m, jnp.max(s, axis=0, keepdims=True))
            alpha = jnp.exp2(m - m_new)
            p = jnp.exp2(s - m_new).astype(BF16)
            return m_new, alpha * acc + _dot(v_ref[0, 0, j], p)

        m, acc = jnp.full((1, TQ), -jnp.inf, F32), jnp.zeros((V_ROWS, TQ), F32)
        for j in range(min(ATTN_AHEAD, n)):
            scores(j, j % ATTN_SLOTS)
        for j in range(n):
            if j + ATTN_AHEAD < n:
                scores(j + ATTN_AHEAD, (j + ATTN_AHEAD) % ATTN_SLOTS)
            m, acc = consume(j, j % ATTN_SLOTS, m, acc)
        finish(t, acc)
        return carry

    @pl.when(bounded_ref[0] != 0)
    def _():
        lax.fori_loop(0, ATTN_TILES, bounded_tile, 0)

    @pl.when(bounded_ref[0] == 0)
    def _():
        lax.fori_loop(0, ATTN_TILES, online_tile, 0)


def _attn_call(bounded, q, k, v):
    B, H, NQ, _, TQ = q.shape
    S = NQ * TQ
    assert TQ == ATTN_Q and v.shape[2] * ATTN_K == S and v.shape[4] == ATTN_K
    NT = ATTN_TILES
    return pl.pallas_call(
        _attn_kernel,
        grid_spec=pltpu.PrefetchScalarGridSpec(
            num_scalar_prefetch=1,
            grid=(B, H, NQ // NT),
            in_specs=[pl.BlockSpec((1, 1, NT, HEAD_PAD, TQ), lambda b, h, i, f: (b, h, i, 0, 0)),
                      pl.BlockSpec((1, 1, S, HEAD_PAD), lambda b, h, i, f: (b, h, 0, 0)),
                      pl.BlockSpec((1, 1) + v.shape[2:], lambda b, h, i, f: (b, h, 0, 0, 0))],
            out_specs=pl.BlockSpec((1, 1, NT, V_DIM, TQ), lambda b, h, i, f: (b, h, i, 0, 0)),
            scratch_shapes=[pltpu.VMEM((ATTN_SLOTS, ATTN_K, TQ), F32)]),
        out_shape=jax.ShapeDtypeStruct((B, H, NQ, V_DIM, TQ), F32),
        compiler_params=pltpu.CompilerParams(
            dimension_semantics=("parallel", "parallel", "arbitrary"), vmem_limit_bytes=VMEM_LIMIT),
        name="attn",
    )(bounded, q, k, v)


def _memkv_kernel(mem_ref, mn_ref, wkv_ref, gk_ref, k_ref, v_ref):
    mn = _rms(mem_ref[0], mn_ref[...]).astype(BF16)
    kv = _dot(mn, wkv_ref[...])
    for hh in range(MEM_HEADS):
        kh = kv[:, hh * MEM_HEAD_DIM:(hh + 1) * MEM_HEAD_DIM]
        k_ref[0, hh] = _rms(kh, gk_ref[...]).astype(BF16)
        v_ref[0, hh] = kv[:, MEM_WIDTH + hh * MEM_HEAD_DIM:MEM_WIDTH + (hh + 1) * MEM_HEAD_DIM].astype(BF16)


def _memkv_call(mem, mn, wkv, gk):
    B, M, D = mem.shape
    full = lambda a: pl.BlockSpec(a.shape, lambda b: (0,) * a.ndim)
    hb = pl.BlockSpec((1, MEM_HEADS, M, MEM_HEAD_DIM), lambda b: (b, 0, 0, 0))
    return pl.pallas_call(
        _memkv_kernel,
        grid=(B,),
        in_specs=[pl.BlockSpec((1, M, D), lambda b: (b, 0, 0)), full(mn), full(wkv), full(gk)],
        out_specs=[hb, hb],
        out_shape=[jax.ShapeDtypeStruct((B, MEM_HEADS, M, MEM_HEAD_DIM), BF16)] * 2,
        compiler_params=pltpu.CompilerParams(
            dimension_semantics=("parallel",), vmem_limit_bytes=VMEM_LIMIT),
        name="memkv",
    )(mem, mn, wkv, gk)


def _gelu_tanh(x):
    c = math.sqrt(2.0 / math.pi)
    return x * (0.5 * (1.0 + jnp.tanh(c * (x + 0.044715 * (x * x * x)))))


def _mix_kernel(x_ref, hf_ref, hb_ref, yg_ref, mla_ref, ln_ref, mnn_ref, wol_ref, wom_ref,
                man_ref, wmq_ref, gmq_ref, km_ref, vm_ref, wmo_ref, o_ref):
    x = x_ref[0]
    lru = (hf_ref[0, 0] + hb_ref[0, 0]) * _gelu_tanh(yg_ref[0])
    lru_n = _rms(lru, ln_ref[...]).astype(BF16)
    mt = jnp.concatenate([mla_ref[0, hh, 0] for hh in range(MLA_HEADS)], axis=0)
    mt = mt * lax.rsqrt(jnp.mean(mt * mt, axis=0, keepdims=True) + EPS) * mnn_ref[...]
    mla_n = mt.T.astype(BF16)
    x1 = x + _dot(lru_n, wol_ref[...]) + _dot(mla_n, wom_ref[...])

    hq = _rms(x1, man_ref[...]).astype(BF16)
    qm = _dot(hq, wmq_ref[...])
    scale = MEM_HEAD_DIM ** -0.5
    outs = []
    for hh in range(MEM_HEADS):
        qh = _rms(qm[:, hh * MEM_HEAD_DIM:(hh + 1) * MEM_HEAD_DIM], gmq_ref[...]) * scale
        s = _dot_nt(qh.astype(BF16), km_ref[0, hh])
        p = jnp.exp(s - jnp.max(s, axis=-1, keepdims=True))
        l = jnp.sum(p, axis=-1, keepdims=True)
        outs.append(_dot(p.astype(BF16), vm_ref[0, hh]) / l)
    om = jnp.concatenate(outs, axis=-1).astype(BF16)
    o_ref[0] = x1 + _dot(om, wmo_ref[...])


def _mix_call(x, h2, yg, mla, ln, mnn, wol, wom, man, wmq, gmq, km, vm, wmo):
    B, S, D = x.shape
    T = MIX_ROWS
    M = km.shape[2]
    full = lambda a: pl.BlockSpec(a.shape, lambda b, i: (0,) * a.ndim)
    row = lambda w: pl.BlockSpec((1, T, w), lambda b, i: (b, i, 0))
    hspec = lambda d: pl.BlockSpec((1, 1, T, LRU_WIDTH), lambda b, i: (d, b, i, 0))
    mspec = pl.BlockSpec((1, MEM_HEADS, M, MEM_HEAD_DIM), lambda b, i: (b, 0, 0, 0))
    return pl.pallas_call(
        _mix_kernel,
        grid=(B, S // T),
        in_specs=[row(D), hspec(0), hspec(1), row(LRU_WIDTH),
                  pl.BlockSpec((1, MLA_HEADS, 1, V_DIM, T), lambda b, i: (b, 0, i, 0, 0)), full(ln), full(mnn),
                  full(wol), full(wom), full(man), full(wmq), full(gmq), mspec, mspec, full(wmo)],
        out_specs=row(D),
        out_shape=jax.ShapeDtypeStruct((B, S, D), F32),
        compiler_params=pltpu.CompilerParams(
            dimension_semantics=("parallel", "parallel"), vmem_limit_bytes=VMEM_LIMIT),
        name="mix",
    )(x, h2, h2, yg, mla, ln, mnn, wol, wom, man, wmq, gmq, km, vm, wmo)


def _ffn_kernel(x_ref, xp_ref, xn_ref, fn_ref, wup_ref, cw_ref, cb_ref, wd_ref, o_ref,
                pm_ref, res_ref, xe_ref, g_ref, u_ref, act_ref):
    T = FFN_ROWS
    K = T // SUBLANES
    PS = K + SUBLANES
    CB = FFN_COLS
    NS = FFN_SLOTS
    D = x_ref.shape[2]
    NC = D // LANES
    dff = wd_ref.shape[0]
    nb = dff // CB
    i = pl.program_id(1)

    x = x_ref[0]
    for c in range(SUBLANES):
        for j in range(NC):
            pm_ref[j, c * PS:c * PS + K] = x[c * K:(c + 1) * K, j * LANES:(j + 1) * LANES]
    gain = fn_ref[...]
    for k in range(K):
        xk = jnp.concatenate([pm_ref[j, pl.ds(k, SUBLANES, stride=PS), :] for j in range(NC)], axis=1)
        res_ref[SUBLANES * k:SUBLANES * (k + 1)] = xk
        xe_ref[SUBLANES * k:SUBLANES * (k + 1)] = _rms(xk, gain).astype(BF16)
    prev = jnp.where(i == 0, 0.0, xp_ref[0])
    nxt = jnp.where(i == pl.num_programs(1) - 1, 0.0, xn_ref[0])
    xe_ref[T:T + HALO] = _rms(prev, gain).astype(BF16)
    xe_ref[T + HALO:T + 2 * HALO] = _rms(nxt, gain).astype(BF16)

    def up(c):
        xe = xe_ref[...]
        g_ref[c % NS] = _dot(xe, wup_ref[:, c * CB:(c + 1) * CB])
        u_ref[c % NS] = _dot(xe, wup_ref[:, dff + c * CB:dff + (c + 1) * CB])

    rowi = lax.broadcasted_iota(jnp.int32, (SUBLANES, CB), 0)

    def conv(ref, slot, c0):
        first, last = ref[slot, 0:SUBLANES], ref[slot, T - SUBLANES:T]
        before = jnp.where(rowi == SUBLANES - 1, ref[slot, T:T + HALO], last)
        after = jnp.where(rowi == 0, ref[slot, T + HALO:T + 2 * HALO], first)
        gm = jnp.concatenate([pltpu.roll(before, 1, 0), ref[slot, 0:T - SUBLANES]], axis=0)
        gp = jnp.concatenate([ref[slot, SUBLANES:T], pltpu.roll(after, SUBLANES - 1, 0)], axis=0)
        cw = cw_ref[:, c0:c0 + CB]
        return cb_ref[:, c0:c0 + CB] + cw[0:1] * gm + cw[1:2] * ref[slot, 0:T] + cw[2:3] * gp

    for c in range(min(NS - 1, nb)):
        up(c)
    for c in range(nb):
        if c + NS - 1 < nb:
            up(c + NS - 1)
        gc = conv(g_ref, c % NS, c * CB)
        uc = conv(u_ref, c % NS, dff + c * CB)
        act_ref[:, c * CB:(c + 1) * CB] = (gc * jax.nn.sigmoid(gc) * uc).astype(BF16)
    y = res_ref[...] + _dot(act_ref[...], wd_ref[...])
    for k in range(K):
        for j in range(NC):
            pm_ref[j, pl.ds(k, SUBLANES, stride=PS), :] = y[SUBLANES * k:SUBLANES * (k + 1),
                                                            j * LANES:(j + 1) * LANES]
    for c in range(SUBLANES):
        for j in range(NC):
            o_ref[0, c * K:(c + 1) * K, j * LANES:(j + 1) * LANES] = pm_ref[j, c * PS:c * PS + K]


def _ffn_call(x, fn, wup, cw, cb, wd):
    B, S, D = x.shape
    T = FFN_ROWS
    hb = T // HALO
    last = S // HALO - 1
    full = lambda a: pl.BlockSpec(a.shape, lambda b, i: (0,) * a.ndim, pipeline_mode=pl.Buffered(1))
    return pl.pallas_call(
        _ffn_kernel,
        grid=(B, S // T),
        in_specs=[pl.BlockSpec((1, T, D), lambda b, i: (b, i, 0)),
                  pl.BlockSpec((1, HALO, D), lambda b, i: (b, jnp.maximum(i * hb - 1, 0), 0)),
                  pl.BlockSpec((1, HALO, D), lambda b, i: (b, jnp.minimum((i + 1) * hb, last), 0)),
                  full(fn), full(wup), full(cw), full(cb), full(wd)],
        out_specs=pl.BlockSpec((1, T, D), lambda b, i: (b, i, 0)),
        out_shape=jax.ShapeDtypeStruct((B, S, D), F32),
        scratch_shapes=[pltpu.VMEM((D // LANES, T + SUBLANES * SUBLANES, LANES), F32),
                        pltpu.VMEM((T, D), F32),
                        pltpu.VMEM((T + 2 * HALO, D), BF16),
                        pltpu.VMEM((FFN_SLOTS, T + 2 * HALO, FFN_COLS), F32),
                        pltpu.VMEM((FFN_SLOTS, T + 2 * HALO, FFN_COLS), F32),
                        pltpu.VMEM((T, wd.shape[0]), BF16)],
        compiler_params=pltpu.CompilerParams(
            dimension_semantics=("parallel", "parallel"), vmem_limit_bytes=VMEM_LIMIT),
        name="ffn",
    )(x, x, x, fn, wup, cw, cb, wd)


def _block_diag_dense(w):
    nb, bs, _ = w.shape
    eye = jnp.eye(nb, dtype=w.dtype)
    return (eye[:, None, :, None] * w[:, :, None, :]).reshape(nb * bs, nb * bs)


def _layer(x, mem, pos3, attn_norm, w_in, lru_conv_w, lru_conv_b, lru_w_a, lru_b_a, lru_w_i, lru_b_i,
           lru_lambda, q_a_norm, w_uq, kv_a_norm, w_ukv, mla_q_norm, mla_k_norm, lru_out_norm,
           mla_out_norm, w_out, mem_attn_norm, mem_norm, w_mem_q, w_mem_kv, mem_q_norm, mem_k_norm,
           w_mem_o, ffn_norm, w_up, ffn_conv_w, ffn_conv_b, w_down):
    D = x.shape[-1]
    H = MLA_HEADS
    row = lambda a: a.reshape(1, -1)
    off_kr = 2 * LRU_WIDTH + Q_LORA + KV_LORA
    pad_cols = lambda a, lo, hi: jnp.pad(a, ((0, 0),) * (a.ndim - 1) + ((lo, hi),))

    win = jnp.concatenate([w_in[:, :off_kr], pad_cols(w_in[:, off_kr:], QK_NOPE, HEAD_PAD - QK_HEAD)],
                          axis=1).astype(BF16)
    wuq = pad_cols(w_uq.reshape(Q_LORA, H, QK_HEAD).transpose(1, 0, 2),
                   0, HEAD_PAD - QK_HEAD).transpose(0, 2, 1).astype(BF16)
    wkv = w_ukv.reshape(KV_LORA, H, QK_NOPE + V_DIM).transpose(1, 0, 2)
    wuk = pad_cols(wkv[:, :, :QK_NOPE], 0, HEAD_PAD - QK_NOPE).astype(BF16)
    wuv = pad_cols(wkv[:, :, QK_NOPE:], 0, V_ROWS - V_DIM).transpose(0, 2, 1).astype(BF16)
    gq = pad_cols(row(mla_q_norm), 0, HEAD_PAD - QK_HEAD).reshape(HEAD_PAD, 1)
    gk = pad_cols(row(mla_k_norm), 0, HEAD_PAD - QK_HEAD)
    inv = (ROPE_THETA ** (-jnp.arange(0, QK_ROPE, 2, dtype=F32) / QK_ROPE)).reshape(QK_ROPE // 2, 1)

    xr, yg, q, k, v = _proj_call(x, pos3, row(attn_norm), win, row(q_a_norm), row(kv_a_norm),
                                 wuq, wuk, wuv, gq, gk, inv)

    wa = jax.vmap(_block_diag_dense)(lru_w_a).astype(BF16)
    wi = jax.vmap(_block_diag_dense)(lru_w_i).astype(BF16)
    r3 = lambda a: a.reshape(2, 1, LRU_WIDTH)
    h2 = _lru_call(xr, lru_conv_w, r3(lru_conv_b), wa, r3(lru_b_a), wi, r3(lru_b_i), r3(lru_lambda))

    score_bound = math.sqrt(QK_HEAD) * LOG2E * jnp.max(jnp.abs(mla_q_norm)) * jnp.max(jnp.abs(mla_k_norm))
    bounded = (2.0 * score_bound <= EXP2_SAFE_RANGE).astype(jnp.int32).reshape(1)
    mla = _attn_call(bounded, q, k, v)
    km, vm = _memkv_call(mem, row(mem_norm), w_mem_kv.astype(BF16), row(mem_k_norm))

    wo = w_out.astype(BF16)
    x2 = _mix_call(x, h2, yg, mla, row(lru_out_norm), mla_out_norm.reshape(-1, 1), wo[:LRU_WIDTH], wo[LRU_WIDTH:],
                   row(mem_attn_norm), w_mem_q.astype(BF16), row(mem_q_norm), km, vm, w_mem_o.astype(BF16))

    return _ffn_call(x2, row(ffn_norm), w_up.astype(BF16), ffn_conv_w, row(ffn_conv_b), w_down.astype(BF16))


def kernel(x, mem, positions, attn_norm, w_in, lru_conv_w, lru_conv_b, lru_w_a, lru_b_a, lru_w_i, lru_b_i, lru_lambda, q_a_norm, w_uq, kv_a_norm, w_ukv, mla_q_norm, mla_k_norm, lru_out_norm, mla_out_norm, w_out, mem_attn_norm, mem_norm, w_mem_q, w_mem_kv, mem_q_norm, mem_k_norm, w_mem_o, ffn_norm, w_up, ffn_conv_w, ffn_conv_b, w_down):
    pos3 = positions[:, None, :]
    params = (attn_norm, w_in, lru_conv_w, lru_conv_b, lru_w_a, lru_b_a, lru_w_i, lru_b_i, lru_lambda,
              q_a_norm, w_uq, kv_a_norm, w_ukv, mla_q_norm, mla_k_norm, lru_out_norm, mla_out_norm, w_out,
              mem_attn_norm, mem_norm, w_mem_q, w_mem_kv, mem_q_norm, mem_k_norm, w_mem_o, ffn_norm, w_up,
              ffn_conv_w, ffn_conv_b, w_down)
    for l in range(attn_norm.shape[0]):
        x = _layer(x, mem, pos3, *[p[l] for p in params])
    return x
```

```python
import functools
import math

import jax
import jax.numpy as jnp
from jax import lax
from jax.experimental import pallas as pl
from jax.experimental.pallas import tpu as pltpu

F32 = jnp.float32
BF16 = jnp.bfloat16

EPS = 1e-6
LRU_WIDTH = 512
LRU_BLOCKS = 8
CONV_WIDTH = 4
LRU_C = 8.0
MLA_HEADS = 8
QK_NOPE = 64
QK_ROPE = 32
QK_HEAD = QK_NOPE + QK_ROPE
V_DIM = 64
Q_LORA = 256
KV_LORA = 128
MLA_WIDTH = MLA_HEADS * V_DIM
ROPE_THETA = 10000.0
MEM_HEADS = 4
MEM_HEAD_DIM = 128
MEM_WIDTH = MEM_HEADS * MEM_HEAD_DIM
FFN_CONV = 3

LANES = 128
SUBLANES = 8
MXU_DIM = 256
HEAD_PAD = LANES
BF16_ROWS = 16
V_ROWS = V_DIM + BF16_ROWS
LOG2E = 1.4426950408889634
EXP2_SAFE_RANGE = 100.0

PROJ_ROWS = 512
LRU_ROWS = 512
ATTN_Q = 512
ATTN_K = 512
ATTN_SPLIT = 2
ATTN_FAST_AHEAD = 2
ATTN_TILES = 4
ATTN_AHEAD = 2
ATTN_SLOTS = ATTN_AHEAD + 1
MIX_ROWS = 512
FFN_ROWS = 512
FFN_COLS = MXU_DIM
FFN_SLOTS = 3
HALO = SUBLANES

VMEM_LIMIT = 56 * 1024 * 1024


def _rms(x, g):
    return x * lax.rsqrt(jnp.mean(x * x, axis=-1, keepdims=True) + EPS) * g


def _dot(a, b):
    return jnp.dot(a, b, preferred_element_type=F32)


def _dot_nt(a, b):
    return lax.dot_general(a, b, (((1,), (1,)), ((), ())), preferred_element_type=F32)


def _proj_kernel(x_ref, pos_ref, an_ref, win_ref, qan_ref, kvan_ref, wuq_ref, wuk_ref, wuv_ref,
                 gq_ref, gk_ref, inv_ref, sb_ref,
                 xr_ref, yg_ref, q_ref, k_ref, v_ref):
    x = x_ref[0]
    h = _rms(x, an_ref[...]).astype(BF16)
    proj = _dot(h, win_ref[...])
    xr_ref[0] = proj[:, 0:LRU_WIDTH]
    yg_ref[0] = proj[:, LRU_WIDTH:2 * LRU_WIDTH]
    o = 2 * LRU_WIDTH
    cq_f = _rms(proj[:, o:o + Q_LORA], qan_ref[...])
    o += Q_LORA
    ckv_f = _rms(proj[:, o:o + KV_LORA], kvan_ref[...])
    o += KV_LORA
    kr_t = proj[:, o:o + HEAD_PAD].T
    cq_t = cq_f.T.astype(BF16)
    ckv_t = ckv_f.T.astype(BF16)
    T = x.shape[0]
    half = QK_ROPE // 2
    r0, r1, r2 = QK_NOPE, QK_NOPE + half, QK_HEAD

    ang = inv_ref[...] * pos_ref[0].astype(F32)
    cos_t = jnp.cos(ang)
    sin_t = jnp.sin(ang)

    row8 = lax.broadcasted_iota(jnp.int32, (SUBLANES, T), 0)
    zpad = jnp.zeros((HEAD_PAD - r2 - SUBLANES, T), F32)
    q_pad = jnp.concatenate([jnp.where(row8 == 0, -sb_ref[...], 0.0), zpad], axis=0)
    k_pad = jnp.concatenate([jnp.where(row8 == 0, 1.0, 0.0), zpad], axis=0)
    vrow = lax.broadcasted_iota(jnp.int32, (V_ROWS, T), 0)
    gq = gq_ref[0:r2] * ((QK_HEAD ** -0.5) * LOG2E)
    gk = gk_ref[0:r2]

    def norm_rope(t, g):
        ss = jnp.sum(t * t, axis=0, keepdims=True) * (1.0 / QK_HEAD)
        tn = t * lax.rsqrt(ss + EPS) * g
        t1, t2 = tn[r0:r1], tn[r1:r2]
        return [tn[0:r0], t1 * cos_t - t2 * sin_t, t1 * sin_t + t2 * cos_t]

    for hh in range(MLA_HEADS):
        qt = _dot(wuq_ref[hh], cq_t)
        q_ref[0, hh, 0] = jnp.concatenate(norm_rope(qt[0:r2], gq) + [q_pad], axis=0).astype(BF16)
        kt = _dot(wuk_ref[hh], ckv_t) + kr_t
        k_ref[0, hh] = jnp.concatenate(norm_rope(kt[0:r2], gk) + [k_pad], axis=0).T.astype(BF16)
        vt = _dot(wuv_ref[hh], ckv_t)
        v_ref[0, hh, 0] = jnp.where(vrow == V_DIM, 1.0, vt).astype(BF16)


def _proj_call(x, pos3, an, win, qan, kvan, wuq, wuk, wuv, gq, gk, inv, sb):
    B, S, D = x.shape
    T = PROJ_ROWS
    H = MLA_HEADS
    full = lambda a: pl.BlockSpec(a.shape, lambda b, i: (0,) * a.ndim)
    row = lambda w: pl.BlockSpec((1, T, w), lambda b, i: (b, i, 0))
    return pl.pallas_call(
        _proj_kernel,
        grid=(B, S // T),
        in_specs=[row(D), pl.BlockSpec((1, 1, T), lambda b, i: (b, 0, i))]
        + [full(a) for a in (an, win, qan, kvan, wuq, wuk, wuv, gq, gk, inv, sb)],
        out_specs=[row(LRU_WIDTH), row(LRU_WIDTH),
                   pl.BlockSpec((1, H, 1, HEAD_PAD, T), lambda b, i: (b, 0, i, 0, 0)),
                   pl.BlockSpec((1, H, T, HEAD_PAD), lambda b, i: (b, 0, i, 0)),
                   pl.BlockSpec((1, H, 1, V_ROWS, T), lambda b, i: (b, 0, i, 0, 0))],
        out_shape=[jax.ShapeDtypeStruct((B, S, LRU_WIDTH), F32),
                   jax.ShapeDtypeStruct((B, S, LRU_WIDTH), F32),
                   jax.ShapeDtypeStruct((B, H, S // T, HEAD_PAD, T), BF16),
                   jax.ShapeDtypeStruct((B, H, S, HEAD_PAD), BF16),
                   jax.ShapeDtypeStruct((B, H, S // T, V_ROWS, T), BF16)],
        compiler_params=pltpu.CompilerParams(
            dimension_semantics=("parallel", "parallel"), vmem_limit_bytes=VMEM_LIMIT),
        name="proj",
    )(x, pos3, an, win, qan, kvan, wuq, wuk, wuv, gq, gk, inv, sb)


def _lru_kernel(x_ref, cw_ref, cb_ref, wa_ref, ba_ref, wi_ref, bi_ref, lam_ref,
                h_ref, px_ref, ph_ref, halo_ref, carry_ref, xc_ref, ga_ref, gi_ref):
    T = LRU_ROWS
    K = T // SUBLANES
    PS = K + SUBLANES
    W = x_ref.shape[2]
    NC = W // LANES
    NH = CONV_WIDTH - 1
    d = pl.program_id(1)
    i = pl.program_id(2)

    @pl.when(i == 0)
    def _():
        halo_ref[...] = jnp.zeros_like(halo_ref)
        carry_ref[...] = jnp.zeros_like(carry_ref)

    def run(reverse):
        x = x_ref[0]
        for c in range(SUBLANES):
            for j in range(NC):
                px_ref[j, c * PS:c * PS + K] = x[c * K:(c + 1) * K, j * LANES:(j + 1) * LANES]
        slabs = [jnp.concatenate([px_ref[j, pl.ds(k, SUBLANES, stride=PS), :] for j in range(NC)], axis=1)
                 for k in range(K)]
        rowi = lax.broadcasted_iota(jnp.int32, (SUBLANES, W), 0)
        if not reverse:
            edge = [slabs[K - NH + m] for m in range(NH)]
            pre = [pltpu.roll(jnp.where(rowi == SUBLANES - 1, halo_ref[m], edge[m]), 1, 0) for m in range(NH)]
            xext = jnp.concatenate(pre + slabs, axis=0)
        else:
            edge = [slabs[m] for m in range(NH)]
            post = [pltpu.roll(jnp.where(rowi == 0, halo_ref[m], edge[m]), SUBLANES - 1, 0) for m in range(NH)]
            xext = jnp.concatenate(slabs + post, axis=0)
        for m in range(NH):
            halo_ref[m] = edge[m]
        cw = cw_ref[0]
        xc = cb_ref[0] + cw[0:1] * xext[0:T]
        for k in range(1, CONV_WIDTH):
            xc = xc + cw[k:k + 1] * xext[SUBLANES * k:SUBLANES * k + T]

        xb = xc.astype(BF16)
        xc_ref[...] = xc
        ga_ref[...] = _dot(xb, wa_ref[0]) + ba_ref[0]
        gi_ref[...] = _dot(xb, wi_ref[0]) + bi_ref[0]
        z = -lam_ref[0]
        sp = jnp.maximum(z, 0.0) + jnp.log1p(jnp.exp(-jnp.abs(z)))
        rate = jnp.broadcast_to((-LRU_C * LOG2E) * sp, (SUBLANES, W))

        hk = jnp.zeros((SUBLANES, W), F32)
        pk = jnp.ones((SUBLANES, W), F32)
        for k in (range(K - 1, -1, -1) if reverse else range(K)):
            rows = slice(SUBLANES * k, SUBLANES * (k + 1))
            ak = jnp.exp2(jax.nn.sigmoid(ga_ref[rows]) * rate)
            mult = jnp.sqrt(1.0 - ak * ak)
            hk = ak * hk + mult * (jax.nn.sigmoid(gi_ref[rows]) * xc_ref[rows])
            pk = ak * pk
            ga_ref[rows] = hk
            gi_ref[rows] = pk
        s = 1
        while s < SUBLANES:
            if not reverse:
                keep = rowi >= s
                p_sh = jnp.where(keep, pltpu.roll(pk, s, 0), 1.0)
                h_sh = jnp.where(keep, pltpu.roll(hk, s, 0), 0.0)
            else:
                keep = rowi < SUBLANES - s
                p_sh = jnp.where(keep, pltpu.roll(pk, SUBLANES - s, 0), 1.0)
                h_sh = jnp.where(keep, pltpu.roll(hk, SUBLANES - s, 0), 0.0)
            hk = pk * h_sh + hk
            pk = pk * p_sh
            s *= 2
        carry = carry_ref[...]
        leave = pk * carry + hk
        if not reverse:
            enter = jnp.where(rowi == 0, carry, pltpu.roll(leave, 1, 0))
            carry_ref[...] = jnp.broadcast_to(leave[SUBLANES - 1:SUBLANES], carry.shape)
        else:
            enter = jnp.where(rowi == SUBLANES - 1, carry, pltpu.roll(leave, SUBLANES - 1, 0))
            carry_ref[...] = jnp.broadcast_to(leave[0:1], carry.shape)
        for k in range(K):
            rows = slice(SUBLANES * k, SUBLANES * (k + 1))
            hfull = ga_ref[rows] + gi_ref[rows] * enter
            for j in range(NC):
                ph_ref[j, pl.ds(k, SUBLANES, stride=PS), :] = hfull[:, j * LANES:(j + 1) * LANES]
        for c in range(SUBLANES):
            for j in range(NC):
                h_ref[0, 0, c * K:(c + 1) * K, j * LANES:(j + 1) * LANES] = ph_ref[j, c * PS:c * PS + K]

    @pl.when(d == 0)
    def _():
        run(False)

    @pl.when(d == 1)
    def _():
        run(True)


def _lru_call(xr, cw, cb, wa, ba, wi, bi, lam):
    B, S, W = xr.shape
    T = LRU_ROWS
    n = S // T
    tile = lambda d, i: i + d * (n - 1 - 2 * i)
    par = lambda a: pl.BlockSpec((1,) + a.shape[1:], lambda b, d, i: (d,) + (0,) * (a.ndim - 1))
    return pl.pallas_call(
        _lru_kernel,
        grid=(B, 2, n),
        in_specs=[pl.BlockSpec((1, T, W), lambda b, d, i: (b, tile(d, i), 0))]
        + [par(a) for a in (cw, cb, wa, ba, wi, bi, lam)],
        out_specs=pl.BlockSpec((1, 1, T, W), lambda b, d, i: (d, b, tile(d, i), 0)),
        out_shape=jax.ShapeDtypeStruct((2, B, S, W), F32),
        scratch_shapes=[pltpu.VMEM((W // LANES, T + SUBLANES * SUBLANES, LANES), F32),
                        pltpu.VMEM((W // LANES, T + SUBLANES * SUBLANES, LANES), F32),
                        pltpu.VMEM((CONV_WIDTH - 1, SUBLANES, W), F32),
                        pltpu.VMEM((SUBLANES, W), F32),
                        pltpu.VMEM((T, W), F32), pltpu.VMEM((T, W), F32), pltpu.VMEM((T, W), F32)],
        compiler_params=pltpu.CompilerParams(
            dimension_semantics=("parallel", "arbitrary", "arbitrary"), vmem_limit_bytes=VMEM_LIMIT),
        name="lru",
    )(xr, cw, cb, wa, ba, wi, bi, lam)


def _attn_kernel(bounded_ref, q_ref, k_ref, v_ref, o_ref, s_ref):
    TQ, TK = ATTN_Q, ATTN_K
    n = k_ref.shape[2] // TK

    def finish(t, acc):
        o_ref[0, 0, t] = acc[0:V_DIM] / acc[V_DIM:V_DIM + 1]

    def bounded_tile(t, carry):
        qt = q_ref[0, 0, t]
        acc = None
        TG = TK // ATTN_SPLIT
        ng = n * ATTN_SPLIT
        ahead = ATTN_FAST_AHEAD
        queue = [_dot(k_ref[0, 0, j * TG:(j + 1) * TG, :], qt) for j in range(ahead)]
        for j in range(ng):
            if j + ahead < ng:
                queue.append(_dot(k_ref[0, 0, (j + ahead) * TG:(j + ahead + 1) * TG, :], qt))
            s = queue.pop(0)
            part = j % ATTN_SPLIT
            vt = v_ref[0, 0, j // ATTN_SPLIT, :, part * TG:(part + 1) * TG]
            d = _dot(vt, jnp.exp2(s).astype(BF16))
            acc = d if acc is None else acc + d
        finish(t, acc)
        return carry

    def online_tile(t, carry):
        qt = q_ref[0, 0, t]

        def scores(j, slot):
            s_ref[slot] = _dot(k_ref[0, 0, j * TK:(j + 1) * TK, :], qt)

        def consume(j, slot, m, acc):
            s = s_ref[slot]
            m_new = jnp.maximum(m, jnp.max(s, axis=0, keepdims=True))
            alpha = jnp.exp2(m - m_new)
            p = jnp.exp2(s - m_new).astype(BF16)
            return m_new, alpha * acc + _dot(v_ref[0, 0, j], p)

        m, acc = jnp.full((1, TQ), -jnp.inf, F32), jnp.zeros((V_ROWS, TQ), F32)
        for j in range(min(ATTN_AHEAD, n)):
            scores(j, j % ATTN_SLOTS)
        for j in range(n):
            if j + ATTN_AHEAD < n:
                scores(j + ATTN_AHEAD, (j + ATTN_AHEAD) % ATTN_SLOTS)
            m, acc = consume(j, j % ATTN_SLOTS, m, acc)
        finish(t, acc)
        return carry

    @pl.when(bounded_ref[0] != 0)
    def _():
        lax.fori_loop(0, ATTN_TILES, bounded_tile, 0)

    @pl.when(bounded_ref[0] == 0)
    def _():
        lax.fori_loop(0, ATTN_TILES, online_tile, 0)


def _attn_call(bounded, q, k, v):
    B, H, NQ, _, TQ = q.shape
    S = NQ * TQ
    assert TQ == ATTN_Q and v.shape[2] * ATTN_K == S and v.shape[4] == ATTN_K
    NT = ATTN_TILES
    return pl.pallas_call(
        _attn_kernel,
        grid_spec=pltpu.PrefetchScalarGridSpec(
            num_scalar_prefetch=1,
            grid=(B, H, NQ // NT),
            in_specs=[pl.BlockSpec((1, 1, NT, HEAD_PAD, TQ), lambda b, h, i, f: (b, h, i, 0, 0)),
                      pl.BlockSpec((1, 1, S, HEAD_PAD), lambda b, h, i, f: (b, h, 0, 0)),
                      pl.BlockSpec((1, 1) + v.shape[2:], lambda b, h, i, f: (b, h, 0, 0, 0))],
            out_specs=pl.BlockSpec((1, 1, NT, V_DIM, TQ), lambda b, h, i, f: (b, h, i, 0, 0)),
            scratch_shapes=[pltpu.VMEM((ATTN_SLOTS, ATTN_K, TQ), F32)]),
        out_shape=jax.ShapeDtypeStruct((B, H, NQ, V_DIM, TQ), F32),
        compiler_params=pltpu.CompilerParams(
            dimension_semantics=("parallel", "parallel", "arbitrary"), vmem_limit_bytes=VMEM_LIMIT),
        name="attn",
    )(bounded, q, k, v)


def _memkv_kernel(mem_ref, mn_ref, wkv_ref, gk_ref, k_ref, v_ref):
    mn = _rms(mem_ref[0], mn_ref[...]).astype(BF16)
    kv = _dot(mn, wkv_ref[...])
    for hh in range(MEM_HEADS):
        kh = kv[:, hh * MEM_HEAD_DIM:(hh + 1) * MEM_HEAD_DIM]
        k_ref[0, hh] = _rms(kh, gk_ref[...]).astype(BF16)
        v_ref[0, hh] = kv[:, MEM_WIDTH + hh * MEM_HEAD_DIM:MEM_WIDTH + (hh + 1) * MEM_HEAD_DIM].astype(BF16)


def _memkv_call(mem, mn, wkv, gk):
    B, M, D = mem.shape
    full = lambda a: pl.BlockSpec(a.shape, lambda b: (0,) * a.ndim)
    hb = pl.BlockSpec((1, MEM_HEADS, M, MEM_HEAD_DIM), lambda b: (b, 0, 0, 0))
    return pl.pallas_call(
        _memkv_kernel,
        grid=(B,),
        in_specs=[pl.BlockSpec((1, M, D), lambda b: (b, 0, 0)), full(mn), full(wkv), full(gk)],
        out_specs=[hb, hb],
        out_shape=[jax.ShapeDtypeStruct((B, MEM_HEADS, M, MEM_HEAD_DIM), BF16)] * 2,
        compiler_params=pltpu.CompilerParams(
            dimension_semantics=("parallel",), vmem_limit_bytes=VMEM_LIMIT),
        name="memkv",
    )(mem, mn, wkv, gk)


def _gelu_tanh(x):
    c = math.sqrt(2.0 / math.pi)
    return x * (0.5 * (1.0 + jnp.tanh(c * (x + 0.044715 * (x * x * x)))))


def _mix_kernel(x_ref, hf_ref, hb_ref, yg_ref, mla_ref, ln_ref, mnn_ref, wol_ref, wom_ref,
                man_ref, wmq_ref, gmq_ref, km_ref, vm_ref, wmo_ref, o_ref):
    x = x_ref[0]
    lru = (hf_ref[0, 0] + hb_ref[0, 0]) * _gelu_tanh(yg_ref[0])
    lru_n = _rms(lru, ln_ref[...]).astype(BF16)
    mt = jnp.concatenate([mla_ref[0, hh, 0] for hh in range(MLA_HEADS)], axis=0)
    mt = mt * lax.rsqrt(jnp.mean(mt * mt, axis=0, keepdims=True) + EPS) * mnn_ref[...]
    mla_n = mt.T.astype(BF16)
    x1 = x + _dot(lru_n, wol_ref[...]) + _dot(mla_n, wom_ref[...])

    hq = _rms(x1, man_ref[...]).astype(BF16)
    qm = _dot(hq, wmq_ref[...])
    scale = MEM_HEAD_DIM ** -0.5
    outs = []
    for hh in range(MEM_HEADS):
        qh = _rms(qm[:, hh * MEM_HEAD_DIM:(hh + 1) * MEM_HEAD_DIM], gmq_ref[...]) * scale
        s = _dot_nt(qh.astype(BF16), km_ref[0, hh])
        p = jnp.exp(s - jnp.max(s, axis=-1, keepdims=True))
        l = jnp.sum(p, axis=-1, keepdims=True)
        outs.append(_dot(p.astype(BF16), vm_ref[0, hh]) / l)
    om = jnp.concatenate(outs, axis=-1).astype(BF16)
    o_ref[0] = x1 + _dot(om, wmo_ref[...])


def _mix_call(x, h2, yg, mla, ln, mnn, wol, wom, man, wmq, gmq, km, vm, wmo):
    B, S, D = x.shape
    T = MIX_ROWS
    M = km.shape[2]
    full = lambda a: pl.BlockSpec(a.shape, lambda b, i: (0,) * a.ndim)
    row = lambda w: pl.BlockSpec((1, T, w), lambda b, i: (b, i, 0))
    hspec = lambda d: pl.BlockSpec((1, 1, T, LRU_WIDTH), lambda b, i: (d, b, i, 0))
    mspec = pl.BlockSpec((1, MEM_HEADS, M, MEM_HEAD_DIM), lambda b, i: (b, 0, 0, 0))
    return pl.pallas_call(
        _mix_kernel,
        grid=(B, S // T),
        in_specs=[row(D), hspec(0), hspec(1), row(LRU_WIDTH),
                  pl.BlockSpec((1, MLA_HEADS, 1, V_DIM, T), lambda b, i: (b, 0, i, 0, 0)), full(ln), full(mnn),
                  full(wol), full(wom), full(man), full(wmq), full(gmq), mspec, mspec, full(wmo)],
        out_specs=row(D),
        out_shape=jax.ShapeDtypeStruct((B, S, D), F32),
        compiler_params=pltpu.CompilerParams(
            dimension_semantics=("parallel", "parallel"), vmem_limit_bytes=VMEM_LIMIT),
        name="mix",
    )(x, h2, h2, yg, mla, ln, mnn, wol, wom, man, wmq, gmq, km, vm, wmo)


def _ffn_kernel(x_ref, xp_ref, xn_ref, fn_ref, wup_ref, cw_ref, cb_ref, wd_ref, o_ref,
                pm_ref, res_ref, xe_ref, g_ref, u_ref, act_ref):
    T = FFN_ROWS
    K = T // SUBLANES
    PS = K + SUBLANES
    CB = FFN_COLS
    NS = FFN_SLOTS
    D = x_ref.shape[2]
    NC = D // LANES
    dff = wd_ref.shape[0]
    nb = dff // CB
    i = pl.program_id(1)

    x = x_ref[0]
    for c in range(SUBLANES):
        for j in range(NC):
            pm_ref[j, c * PS:c * PS + K] = x[c * K:(c + 1) * K, j * LANES:(j + 1) * LANES]
    gain = fn_ref[...]
    for k in range(K):
        xk = jnp.concatenate([pm_ref[j, pl.ds(k, SUBLANES, stride=PS), :] for j in range(NC)], axis=1)
        res_ref[SUBLANES * k:SUBLANES * (k + 1)] = xk
        xe_ref[SUBLANES * k:SUBLANES * (k + 1)] = _rms(xk, gain).astype(BF16)
    prev = jnp.where(i == 0, 0.0, xp_ref[0])
    nxt = jnp.where(i == pl.num_programs(1) - 1, 0.0, xn_ref[0])
    xe_ref[T:T + HALO] = _rms(prev, gain).astype(BF16)
    xe_ref[T + HALO:T + 2 * HALO] = _rms(nxt, gain).astype(BF16)

    def up(c):
        xe = xe_ref[...]
        g_ref[c % NS] = _dot(xe, wup_ref[:, c * CB:(c + 1) * CB])
        u_ref[c % NS] = _dot(xe, wup_ref[:, dff + c * CB:dff + (c + 1) * CB])

    rowi = lax.broadcasted_iota(jnp.int32, (SUBLANES, CB), 0)

    def conv(ref, slot, c0):
        first, last = ref[slot, 0:SUBLANES], ref[slot, T - SUBLANES:T]
        before = jnp.where(rowi == SUBLANES - 1, ref[slot, T:T + HALO], last)
        after = jnp.where(rowi == 0, ref[slot, T + HALO:T + 2 * HALO], first)
        gm = jnp.concatenate([pltpu.roll(before, 1, 0), ref[slot, 0:T - SUBLANES]], axis=0)
        gp = jnp.concatenate([ref[slot, SUBLANES:T], pltpu.roll(after, SUBLANES - 1, 0)], axis=0)
        cw = cw_ref[:, c0:c0 + CB]
        return cb_ref[:, c0:c0 + CB] + cw[0:1] * gm + cw[1:2] * ref[slot, 0:T] + cw[2:3] * gp

    for c in range(min(NS - 1, nb)):
        up(c)
    for c in range(nb):
        if c + NS - 1 < nb:
            up(c + NS - 1)
        gc = conv(g_ref, c % NS, c * CB)
        uc = conv(u_ref, c % NS, dff + c * CB)
        act_ref[:, c * CB:(c + 1) * CB] = (gc * jax.nn.sigmoid(gc) * uc).astype(BF16)
    y = res_ref[...] + _dot(act_ref[...], wd_ref[...])
    for k in range(K):
        for j in range(NC):
            pm_ref[j, pl.ds(k, SUBLANES, stride=PS), :] = y[SUBLANES * k:SUBLANES * (k + 1),
                                                            j * LANES:(j + 1) * LANES]
    for c in range(SUBLANES):
        for j in range(NC):
            o_ref[0, c * K:(c + 1) * K, j * LANES:(j + 1) * LANES] = pm_ref[j, c * PS:c * PS + K]


def _ffn_call(x, fn, wup, cw, cb, wd):
    B, S, D = x.shape
    T = FFN_ROWS
    hb = T // HALO
    last = S // HALO - 1
    full = lambda a: pl.BlockSpec(a.shape, lambda b, i: (0,) * a.ndim, pipeline_mode=pl.Buffered(1))
    return pl.pallas_call(
        _ffn_kernel,
        grid=(B, S // T),
        in_specs=[pl.BlockSpec((1, T, D), lambda b, i: (b, i, 0)),
                  pl.BlockSpec((1, HALO, D), lambda b, i: (b, jnp.maximum(i * hb - 1, 0), 0)),
                  pl.BlockSpec((1, HALO, D), lambda b, i: (b, jnp.minimum((i + 1) * hb, last), 0)),
                  full(fn), full(wup), full(cw), full(cb), full(wd)],
        out_specs=pl.BlockSpec((1, T, D), lambda b, i: (b, i, 0)),
        out_shape=jax.ShapeDtypeStruct((B, S, D), F32),
        scratch_shapes=[pltpu.VMEM((D // LANES, T + SUBLANES * SUBLANES, LANES), F32),
                        pltpu.VMEM((T, D), F32),
                        pltpu.VMEM((T + 2 * HALO, D), BF16),
                        pltpu.VMEM((FFN_SLOTS, T + 2 * HALO, FFN_COLS), F32),
                        pltpu.VMEM((FFN_SLOTS, T + 2 * HALO, FFN_COLS), F32),
                        pltpu.VMEM((T, wd.shape[0]), BF16)],
        compiler_params=pltpu.CompilerParams(
            dimension_semantics=("parallel", "parallel"), vmem_limit_bytes=VMEM_LIMIT),
        name="ffn",
    )(x, x, x, fn, wup, cw, cb, wd)


def _block_diag_dense(w):
    nb, bs, _ = w.shape
    eye = jnp.eye(nb, dtype=w.dtype)
    return (eye[:, None, :, None] * w[:, :, None, :]).reshape(nb * bs, nb * bs)


def _layer(x, mem, pos3, attn_norm, w_in, lru_conv_w, lru_conv_b, lru_w_a, lru_b_a, lru_w_i, lru_b_i,
           lru_lambda, q_a_norm, w_uq, kv_a_norm, w_ukv, mla_q_norm, mla_k_norm, lru_out_norm,
           mla_out_norm, w_out, mem_attn_norm, mem_norm, w_mem_q, w_mem_kv, mem_q_norm, mem_k_norm,
           w_mem_o, ffn_norm, w_up, ffn_conv_w, ffn_conv_b, w_down):
    D = x.shape[-1]
    H = MLA_HEADS
    row = lambda a: a.reshape(1, -1)
    off_kr = 2 * LRU_WIDTH + Q_LORA + KV_LORA
    pad_cols = lambda a, lo, hi: jnp.pad(a, ((0, 0),) * (a.ndim - 1) + ((lo, hi),))

    win = jnp.concatenate([w_in[:, :off_kr], pad_cols(w_in[:, off_kr:], QK_NOPE, HEAD_PAD - QK_HEAD)],
                          axis=1).astype(BF16)
    wuq = pad_cols(w_uq.reshape(Q_LORA, H, QK_HEAD).transpose(1, 0, 2),
                   0, HEAD_PAD - QK_HEAD).transpose(0, 2, 1).astype(BF16)
    wkv = w_ukv.reshape(KV_LORA, H, QK_NOPE + V_DIM).transpose(1, 0, 2)
    wuk = pad_cols(wkv[:, :, :QK_NOPE], 0, HEAD_PAD - QK_NOPE).transpose(0, 2, 1).astype(BF16)
    wuv = pad_cols(wkv[:, :, QK_NOPE:], 0, V_ROWS - V_DIM).transpose(0, 2, 1).astype(BF16)
    gq = pad_cols(row(mla_q_norm), 0, HEAD_PAD - QK_HEAD).reshape(HEAD_PAD, 1)
    gk = pad_cols(row(mla_k_norm), 0, HEAD_PAD - QK_HEAD).reshape(HEAD_PAD, 1)
    inv = (ROPE_THETA ** (-jnp.arange(0, QK_ROPE, 2, dtype=F32) / QK_ROPE)).reshape(QK_ROPE // 2, 1)
    score_bound = math.sqrt(QK_HEAD) * LOG2E * jnp.max(jnp.abs(mla_q_norm)) * jnp.max(jnp.abs(mla_k_norm))

    xr, yg, q, k, v = _proj_call(x, pos3, row(attn_norm), win, row(q_a_norm), row(kv_a_norm),
                                 wuq, wuk, wuv, gq, gk, inv, score_bound.reshape(1, 1))

    wa = jax.vmap(_block_diag_dense)(lru_w_a).astype(BF16)
    wi = jax.vmap(_block_diag_dense)(lru_w_i).astype(BF16)
    r3 = lambda a: a.reshape(2, 1, LRU_WIDTH)
    h2 = _lru_call(xr, lru_conv_w, r3(lru_conv_b), wa, r3(lru_b_a), wi, r3(lru_b_i), r3(lru_lambda))

    bounded = (2.0 * score_bound <= EXP2_SAFE_RANGE).astype(jnp.int32).reshape(1)
    mla = _attn_call(bounded, q, k, v)
    km, vm = _memkv_call(mem, row(mem_norm), w_mem_kv.astype(BF16), row(mem_k_norm))

    wo = w_out.astype(BF16)
    x2 = _mix_call(x, h2, yg, mla, row(lru_out_norm), mla_out_norm.reshape(-1, 1), wo[:LRU_WIDTH], wo[LRU_WIDTH:],
                   row(mem_attn_norm), w_mem_q.astype(BF16), row(mem_q_norm), km, vm, w_mem_o.astype(BF16))

    return _ffn_call(x2, row(ffn_norm), w_up.astype(BF16), ffn_conv_w, row(ffn_conv_b), w_down.astype(BF16))


def kernel(x, mem, positions, attn_norm, w_in, lru_conv_w, lru_conv_b, lru_w_a, lru_b_a, lru_w_i, lru_b_i, lru_lambda, q_a_norm, w_uq, kv_a_norm, w_ukv, mla_q_norm, mla_k_norm, lru_out_norm, mla_out_norm, w_out, mem_attn_norm, mem_norm, w_mem_q, w_mem_kv, mem_q_norm, mem_k_norm, w_mem_o, ffn_norm, w_up, ffn_conv_w, ffn_conv_b, w_down):
    pos3 = positions[:, None, :]
    params = (attn_norm, w_in, lru_conv_w, lru_conv_b, lru_w_a, lru_b_a, lru_w_i, lru_b_i, lru_lambda,
              q_a_norm, w_uq, kv_a_norm, w_ukv, mla_q_norm, mla_k_norm, lru_out_norm, mla_out_norm, w_out,
              mem_attn_norm, mem_norm, w_mem_q, w_mem_kv, mem_q_norm, mem_k_norm, w_mem_o, ffn_norm, w_up,
              ffn_conv_w, ffn_conv_b, w_down)
    for l in range(attn_norm.shape[0]):
        x = _layer(x, mem, pos3, *[p[l] for p in params])
    return x
```

```python
import functools
import math

import jax
import jax.numpy as jnp
from jax import lax
from jax.experimental import pallas as pl
from jax.experimental.pallas import tpu as pltpu

F32 = jnp.float32
BF16 = jnp.bfloat16

EPS = 1e-6
LRU_WIDTH = 512
LRU_BLOCKS = 8
CONV_WIDTH = 4
LRU_C = 8.0
MLA_HEADS = 8
QK_NOPE = 64
QK_ROPE = 32
QK_HEAD = QK_NOPE + QK_ROPE
V_DIM = 64
Q_LORA = 256
KV_LORA = 128
MLA_WIDTH = MLA_HEADS * V_DIM
ROPE_THETA = 10000.0
MEM_HEADS = 4
MEM_HEAD_DIM = 128
MEM_WIDTH = MEM_HEADS * MEM_HEAD_DIM
FFN_CONV = 3

LANES = 128
SUBLANES = 8
MXU_DIM = 256
HEAD_PAD = LANES
BF16_ROWS = 16
V_ROWS = V_DIM + BF16_ROWS
LOG2E = 1.4426950408889634
EXP2_SAFE_RANGE = 100.0

PROJ_ROWS = 512
LRU_ROWS = 512
ATTN_Q = 512
ATTN_K = 512
ATTN_SPLIT = 2
ATTN_FAST_AHEAD = 2
ATTN_TILES = 4
ATTN_AHEAD = 2
ATTN_SLOTS = ATTN_AHEAD + 1
MIX_ROWS = 512
FFN_ROWS = 512
FFN_COLS = MXU_DIM
FFN_SLOTS = 3
HALO = SUBLANES

VMEM_LIMIT = 56 * 1024 * 1024


def _rms(x, g):
    return x * lax.rsqrt(jnp.mean(x * x, axis=-1, keepdims=True) + EPS) * g


def _dot(a, b):
    return jnp.dot(a, b, preferred_element_type=F32)


def _dot_nt(a, b):
    return lax.dot_general(a, b, (((1,), (1,)), ((), ())), preferred_element_type=F32)


def _proj_kernel(x_ref, pos_ref, an_ref, win_ref, qan_ref, kvan_ref, wuq_ref, wuk_ref, wuv_ref,
                 gq_ref, gk_ref, inv_ref, sb_ref,
                 xr_ref, yg_ref, q_ref, k_ref, v_ref):
    x = x_ref[0]
    h = _rms(x, an_ref[...]).astype(BF16)
    proj = _dot(h, win_ref[...])
    xr_ref[0] = proj[:, 0:LRU_WIDTH]
    yg_ref[0] = proj[:, LRU_WIDTH:2 * LRU_WIDTH]
    o = 2 * LRU_WIDTH
    cq_f = _rms(proj[:, o:o + Q_LORA], qan_ref[...])
    o += Q_LORA
    ckv_f = _rms(proj[:, o:o + KV_LORA], kvan_ref[...])
    o += KV_LORA
    kr_t = proj[:, o:o + HEAD_PAD].T
    cq_t = cq_f.T.astype(BF16)
    ckv_t = ckv_f.T.astype(BF16)
    T = x.shape[0]
    half = QK_ROPE // 2
    r0, r1, r2 = QK_NOPE, QK_NOPE + half, QK_HEAD

    ang = inv_ref[...] * pos_ref[0].astype(F32)
    cos_t = jnp.cos(ang)
    sin_t = jnp.sin(ang)

    row8 = lax.broadcasted_iota(jnp.int32, (SUBLANES, T), 0)
    zpad = jnp.zeros((HEAD_PAD - r2 - SUBLANES, T), F32)
    q_pad = jnp.concatenate([jnp.where(row8 == 0, -sb_ref[...], 0.0), zpad], axis=0)
    k_pad = jnp.concatenate([jnp.where(row8 == 0, 1.0, 0.0), zpad], axis=0)
    vrow = lax.broadcasted_iota(jnp.int32, (V_ROWS, T), 0)
    gq = gq_ref[0:r2] * ((QK_HEAD ** -0.5) * LOG2E)
    gk = gk_ref[0:r2]

    def norm_rope(t, g):
        ss = jnp.sum(t * t, axis=0, keepdims=True) * (1.0 / QK_HEAD)
        tn = t * lax.rsqrt(ss + EPS) * g
        t1, t2 = tn[r0:r1], tn[r1:r2]
        return [tn[0:r0], t1 * cos_t - t2 * sin_t, t1 * sin_t + t2 * cos_t]

    for hh in range(MLA_HEADS):
        qt = _dot(wuq_ref[hh], cq_t)
        q_ref[0, hh, 0] = jnp.concatenate(norm_rope(qt, gq) + [q_pad], axis=0).astype(BF16)
        kt = jnp.concatenate([_dot(wuk_ref[hh], ckv_t), kr_t[r0:r2]], axis=0)
        k_ref[0, hh] = jnp.concatenate(norm_rope(kt, gk) + [k_pad], axis=0).T.astype(BF16)
        vt = _dot(wuv_ref[hh], ckv_t)
        v_ref[0, hh, 0] = jnp.where(vrow == V_DIM, 1.0, vt).astype(BF16)


def _proj_call(x, pos3, an, win, qan, kvan, wuq, wuk, wuv, gq, gk, inv, sb):
    B, S, D = x.shape
    T = PROJ_ROWS
    H = MLA_HEADS
    full = lambda a: pl.BlockSpec(a.shape, lambda b, i: (0,) * a.ndim)
    row = lambda w: pl.BlockSpec((1, T, w), lambda b, i: (b, i, 0))
    return pl.pallas_call(
        _proj_kernel,
        grid=(B, S // T),
        in_specs=[row(D), pl.BlockSpec((1, 1, T), lambda b, i: (b, 0, i))]
        + [full(a) for a in (an, win, qan, kvan, wuq, wuk, wuv, gq, gk, inv, sb)],
        out_specs=[row(LRU_WIDTH), row(LRU_WIDTH),
                   pl.BlockSpec((1, H, 1, HEAD_PAD, T), lambda b, i: (b, 0, i, 0, 0)),
                   pl.BlockSpec((1, H, T, HEAD_PAD), lambda b, i: (b, 0, i, 0)),
                   pl.BlockSpec((1, H, 1, V_ROWS, T), lambda b, i: (b, 0, i, 0, 0))],
        out_shape=[jax.ShapeDtypeStruct((B, S, LRU_WIDTH), F32),
                   jax.ShapeDtypeStruct((B, S, LRU_WIDTH), F32),
                   jax.ShapeDtypeStruct((B, H, S // T, HEAD_PAD, T), BF16),
                   jax.ShapeDtypeStruct((B, H, S, HEAD_PAD), BF16),
                   jax.ShapeDtypeStruct((B, H, S // T, V_ROWS, T), BF16)],
        compiler_params=pltpu.CompilerParams(
            dimension_semantics=("parallel", "parallel"), vmem_limit_bytes=VMEM_LIMIT),
        name="proj",
    )(x, pos3, an, win, qan, kvan, wuq, wuk, wuv, gq, gk, inv, sb)


def _lru_kernel(x_ref, cw_ref, cb_ref, wa_ref, ba_ref, wi_ref, bi_ref, lam_ref,
                h_ref, px_ref, ph_ref, halo_ref, carry_ref, xc_ref, ga_ref, gi_ref):
    T = LRU_ROWS
    K = T // SUBLANES
    PS = K + SUBLANES
    W = x_ref.shape[2]
    NC = W // LANES
    NH = CONV_WIDTH - 1
    d = pl.program_id(1)
    i = pl.program_id(2)

    @pl.when(i == 0)
    def _():
        halo_ref[...] = jnp.zeros_like(halo_ref)
        carry_ref[...] = jnp.zeros_like(carry_ref)

    def run(reverse):
        x = x_ref[0]
        for c in range(SUBLANES):
            for j in range(NC):
                px_ref[j, c * PS:c * PS + K] = x[c * K:(c + 1) * K, j * LANES:(j + 1) * LANES]
        slabs = [jnp.concatenate([px_ref[j, pl.ds(k, SUBLANES, stride=PS), :] for j in range(NC)], axis=1)
                 for k in range(K)]
        rowi = lax.broadcasted_iota(jnp.int32, (SUBLANES, W), 0)
        if not reverse:
            edge = [slabs[K - NH + m] for m in range(NH)]
            pre = [pltpu.roll(jnp.where(rowi == SUBLANES - 1, halo_ref[m], edge[m]), 1, 0) for m in range(NH)]
            xext = jnp.concatenate(pre + slabs, axis=0)
        else:
            edge = [slabs[m] for m in range(NH)]
            post = [pltpu.roll(jnp.where(rowi == 0, halo_ref[m], edge[m]), SUBLANES - 1, 0) for m in range(NH)]
            xext = jnp.concatenate(slabs + post, axis=0)
        for m in range(NH):
            halo_ref[m] = edge[m]
        cw = cw_ref[0]
        xc = cb_ref[0] + cw[0:1] * xext[0:T]
        for k in range(1, CONV_WIDTH):
            xc = xc + cw[k:k + 1] * xext[SUBLANES * k:SUBLANES * k + T]

        xb = xc.astype(BF16)
        xc_ref[...] = xc
        for g in range(W // MXU_DIM):
            cols = slice(g * MXU_DIM, (g + 1) * MXU_DIM)
            ga_ref[:, cols] = _dot(xb[:, cols], wa_ref[0, g]) + ba_ref[0, :, cols]
            gi_ref[:, cols] = _dot(xb[:, cols], wi_ref[0, g]) + bi_ref[0, :, cols]
        z = -lam_ref[0]
        sp = jnp.maximum(z, 0.0) + jnp.log1p(jnp.exp(-jnp.abs(z)))
        rate = jnp.broadcast_to((-LRU_C * LOG2E) * sp, (SUBLANES, W))

        hk = jnp.zeros((SUBLANES, W), F32)
        pk = jnp.ones((SUBLANES, W), F32)
        for k in (range(K - 1, -1, -1) if reverse else range(K)):
            rows = slice(SUBLANES * k, SUBLANES * (k + 1))
            ak = jnp.exp2(jax.nn.sigmoid(ga_ref[rows]) * rate)
            mult = jnp.sqrt(1.0 - ak * ak)
            hk = ak * hk + mult * (jax.nn.sigmoid(gi_ref[rows]) * xc_ref[rows])
            pk = ak * pk
            ga_ref[rows] = hk
            gi_ref[rows] = pk
        s = 1
        while s < SUBLANES:
            if not reverse:
                keep = rowi >= s
                p_sh = jnp.where(keep, pltpu.roll(pk, s, 0), 1.0)
                h_sh = jnp.where(keep, pltpu.roll(hk, s, 0), 0.0)
            else:
                keep = rowi < SUBLANES - s
                p_sh = jnp.where(keep, pltpu.roll(pk, SUBLANES - s, 0), 1.0)
                h_sh = jnp.where(keep, pltpu.roll(hk, SUBLANES - s, 0), 0.0)
            hk = pk * h_sh + hk
            pk = pk * p_sh
            s *= 2
        carry = carry_ref[...]
        leave = pk * carry + hk
        if not reverse:
            enter = jnp.where(rowi == 0, carry, pltpu.roll(leave, 1, 0))
            carry_ref[...] = jnp.broadcast_to(leave[SUBLANES - 1:SUBLANES], carry.shape)
        else:
            enter = jnp.where(rowi == SUBLANES - 1, carry, pltpu.roll(leave, SUBLANES - 1, 0))
            carry_ref[...] = jnp.broadcast_to(leave[0:1], carry.shape)
        for k in range(K):
            rows = slice(SUBLANES * k, SUBLANES * (k + 1))
            hfull = ga_ref[rows] + gi_ref[rows] * enter
            for j in range(NC):
                ph_ref[j, pl.ds(k, SUBLANES, stride=PS), :] = hfull[:, j * LANES:(j + 1) * LANES]
        for c in range(SUBLANES):
            for j in range(NC):
                h_ref[0, 0, c * K:(c + 1) * K, j * LANES:(j + 1) * LANES] = ph_ref[j, c * PS:c * PS + K]

    @pl.when(d == 0)
    def _():
        run(False)

    @pl.when(d == 1)
    def _():
        run(True)


def _lru_call(xr, cw, cb, wa, ba, wi, bi, lam):
    B, S, W = xr.shape
    T = LRU_ROWS
    n = S // T
    tile = lambda d, i: i + d * (n - 1 - 2 * i)
    par = lambda a: pl.BlockSpec((1,) + a.shape[1:], lambda b, d, i: (d,) + (0,) * (a.ndim - 1))
    return pl.pallas_call(
        _lru_kernel,
        grid=(B, 2, n),
        in_specs=[pl.BlockSpec((1, T, W), lambda b, d, i: (b, tile(d, i), 0))]
        + [par(a) for a in (cw, cb, wa, ba, wi, bi, lam)],
        out_specs=pl.BlockSpec((1, 1, T, W), lambda b, d, i: (d, b, tile(d, i), 0)),
        out_shape=jax.ShapeDtypeStruct((2, B, S, W), F32),
        scratch_shapes=[pltpu.VMEM((W // LANES, T + SUBLANES * SUBLANES, LANES), F32),
                        pltpu.VMEM((W // LANES, T + SUBLANES * SUBLANES, LANES), F32),
                        pltpu.VMEM((CONV_WIDTH - 1, SUBLANES, W), F32),
                        pltpu.VMEM((SUBLANES, W), F32),
                        pltpu.VMEM((T, W), F32), pltpu.VMEM((T, W), F32), pltpu.VMEM((T, W), F32)],
        compiler_params=pltpu.CompilerParams(
            dimension_semantics=("parallel", "arbitrary", "arbitrary"), vmem_limit_bytes=VMEM_LIMIT),
        name="lru",
    )(xr, cw, cb, wa, ba, wi, bi, lam)


def _attn_kernel(bounded_ref, q_ref, k_ref, v_ref, o_ref, s_ref):
    TQ, TK = ATTN_Q, ATTN_K
    n = k_ref.shape[2] // TK

    def finish(t, acc):
        o_ref[0, 0, t] = acc[0:V_DIM] / acc[V_DIM:V_DIM + 1]

    def bounded_tile(t, carry):
        qt = q_ref[0, 0, t]
        acc = None
        TG = TK // ATTN_SPLIT
        ng = n * ATTN_SPLIT
        ahead = ATTN_FAST_AHEAD
        queue = [_dot(k_ref[0, 0, j * TG:(j + 1) * TG, :], qt) for j in range(ahead)]
        for j in range(ng):
            if j + ahead < ng:
                queue.append(_dot(k_ref[0, 0, (j + ahead) * TG:(j + ahead + 1) * TG, :], qt))
            s = queue.pop(0)
            part = j % ATTN_SPLIT
            vt = v_ref[0, 0, j // ATTN_SPLIT, :, part * TG:(part + 1) * TG]
            d = _dot(vt, jnp.exp2(s).astype(BF16))
            acc = d if acc is None else acc + d
        finish(t, acc)
        return carry

    def online_tile(t, carry):
        qt = q_ref[0, 0, t]

        def scores(j, slot):
            s_ref[slot] = _dot(k_ref[0, 0, j * TK:(j + 1) * TK, :], qt)

        def consume(j, slot, m, acc):
            s = s_ref[slot]
            m_new = jnp.maximum(m, jnp.max(s, axis=0, keepdims=True))
            alpha = jnp.exp2(m - m_new)
            p = jnp.exp2(s - m_new).astype(BF16)
            return m_new, alpha * acc + _dot(v_ref[0, 0, j], p)

        m, acc = jnp.full((1, TQ), -jnp.inf, F32), jnp.zeros((V_ROWS, TQ), F32)
        for j in range(min(ATTN_AHEAD, n)):
            scores(j, j % ATTN_SLOTS)
        for j in range(n):
            if j + ATTN_AHEAD < n:
                scores(j + ATTN_AHEAD, (j + ATTN_AHEAD) % ATTN_SLOTS)
            m, acc = consume(j, j % ATTN_SLOTS, m, acc)
        finish(t, acc)
        return carry

    @pl.when(bounded_ref[0] != 0)
    def _():
        lax.fori_loop(0, ATTN_TILES, bounded_tile, 0)

    @pl.when(bounded_ref[0] == 0)
    def _():
        lax.fori_loop(0, ATTN_TILES, online_tile, 0)


def _attn_call(bounded, q, k, v):
    B, H, NQ, _, TQ = q.shape
    S = NQ * TQ
    assert TQ == ATTN_Q and v.shape[2] * ATTN_K == S and v.shape[4] == ATTN_K
    NT = ATTN_TILES
    return pl.pallas_call(
        _attn_kernel,
        grid_spec=pltpu.PrefetchScalarGridSpec(
            num_scalar_prefetch=1,
            grid=(B, H, NQ // NT),
            in_specs=[pl.BlockSpec((1, 1, NT, HEAD_PAD, TQ), lambda b, h, i, f: (b, h, i, 0, 0)),
                      pl.BlockSpec((1, 1, S, HEAD_PAD), lambda b, h, i, f: (b, h, 0, 0)),
                      pl.BlockSpec((1, 1) + v.shape[2:], lambda b, h, i, f: (b, h, 0, 0, 0))],
            out_specs=pl.BlockSpec((1, 1, NT, V_DIM, TQ), lambda b, h, i, f: (b, h, i, 0, 0)),
            scratch_shapes=[pltpu.VMEM((ATTN_SLOTS, ATTN_K, TQ), F32)]),
        out_shape=jax.ShapeDtypeStruct((B, H, NQ, V_DIM, TQ), F32),
        compiler_params=pltpu.CompilerParams(
            dimension_semantics=("parallel", "parallel", "arbitrary"), vmem_limit_bytes=VMEM_LIMIT),
        name="attn",
    )(bounded, q, k, v)


def _memkv_kernel(mem_ref, mn_ref, wkv_ref, gk_ref, k_ref, v_ref):
    mn = _rms(mem_ref[0], mn_ref[...]).astype(BF16)
    kv = _dot(mn, wkv_ref[...])
    for hh in range(MEM_HEADS):
        kh = kv[:, hh * MEM_HEAD_DIM:(hh + 1) * MEM_HEAD_DIM]
        k_ref[0, hh] = _rms(kh, gk_ref[...]).astype(BF16)
        v_ref[0, hh] = kv[:, MEM_WIDTH + hh * MEM_HEAD_DIM:MEM_WIDTH + (hh + 1) * MEM_HEAD_DIM].astype(BF16)


def _memkv_call(mem, mn, wkv, gk):
    B, M, D = mem.shape
    full = lambda a: pl.BlockSpec(a.shape, lambda b: (0,) * a.ndim)
    hb = pl.BlockSpec((1, MEM_HEADS, M, MEM_HEAD_DIM), lambda b: (b, 0, 0, 0))
    return pl.pallas_call(
        _memkv_kernel,
        grid=(B,),
        in_specs=[pl.BlockSpec((1, M, D), lambda b: (b, 0, 0)), full(mn), full(wkv), full(gk)],
        out_specs=[hb, hb],
        out_shape=[jax.ShapeDtypeStruct((B, MEM_HEADS, M, MEM_HEAD_DIM), BF16)] * 2,
        compiler_params=pltpu.CompilerParams(
            dimension_semantics=("parallel",), vmem_limit_bytes=VMEM_LIMIT),
        name="memkv",
    )(mem, mn, wkv, gk)


def _gelu_tanh(x):
    c = math.sqrt(2.0 / math.pi)
    return x * (0.5 * (1.0 + jnp.tanh(c * (x + 0.044715 * (x * x * x)))))


def _mix_kernel(x_ref, hf_ref, hb_ref, yg_ref, mla_ref, ln_ref, mnn_ref, wol_ref, wom_ref,
                man_ref, wmq_ref, gmq_ref, km_ref, vm_ref, wmo_ref, o_ref):
    x = x_ref[0]
    lru = (hf_ref[0, 0] + hb_ref[0, 0]) * _gelu_tanh(yg_ref[0])
    lru_n = _rms(lru, ln_ref[...]).astype(BF16)
    mt = jnp.concatenate([mla_ref[0, hh, 0] for hh in range(MLA_HEADS)], axis=0)
    mt = mt * lax.rsqrt(jnp.mean(mt * mt, axis=0, keepdims=True) + EPS) * mnn_ref[...]
    mla_n = mt.T.astype(BF16)
    x1 = x + _dot(lru_n, wol_ref[...]) + _dot(mla_n, wom_ref[...])

    hq = _rms(x1, man_ref[...]).astype(BF16)
    qm = _dot(hq, wmq_ref[...])
    scale = MEM_HEAD_DIM ** -0.5
    outs = []
    for hh in range(MEM_HEADS):
        qh = _rms(qm[:, hh * MEM_HEAD_DIM:(hh + 1) * MEM_HEAD_DIM], gmq_ref[...]) * scale
        s = _dot_nt(qh.astype(BF16), km_ref[0, hh])
        p = jnp.exp(s - jnp.max(s, axis=-1, keepdims=True))
        l = jnp.sum(p, axis=-1, keepdims=True)
        outs.append(_dot(p.astype(BF16), vm_ref[0, hh]) / l)
    om = jnp.concatenate(outs, axis=-1).astype(BF16)
    o_ref[0] = x1 + _dot(om, wmo_ref[...])


def _mix_call(x, h2, yg, mla, ln, mnn, wol, wom, man, wmq, gmq, km, vm, wmo):
    B, S, D = x.shape
    T = MIX_ROWS
    M = km.shape[2]
    full = lambda a: pl.BlockSpec(a.shape, lambda b, i: (0,) * a.ndim)
    row = lambda w: pl.BlockSpec((1, T, w), lambda b, i: (b, i, 0))
    hspec = lambda d: pl.BlockSpec((1, 1, T, LRU_WIDTH), lambda b, i: (d, b, i, 0))
    mspec = pl.BlockSpec((1, MEM_HEADS, M, MEM_HEAD_DIM), lambda b, i: (b, 0, 0, 0))
    return pl.pallas_call(
        _mix_kernel,
        grid=(B, S // T),
        in_specs=[row(D), hspec(0), hspec(1), row(LRU_WIDTH),
                  pl.BlockSpec((1, MLA_HEADS, 1, V_DIM, T), lambda b, i: (b, 0, i, 0, 0)), full(ln), full(mnn),
                  full(wol), full(wom), full(man), full(wmq), full(gmq), mspec, mspec, full(wmo)],
        out_specs=row(D),
        out_shape=jax.ShapeDtypeStruct((B, S, D), F32),
        compiler_params=pltpu.CompilerParams(
            dimension_semantics=("parallel", "parallel"), vmem_limit_bytes=VMEM_LIMIT),
        name="mix",
    )(x, h2, h2, yg, mla, ln, mnn, wol, wom, man, wmq, gmq, km, vm, wmo)


def _ffn_kernel(x_ref, xp_ref, xn_ref, fn_ref, wup_ref, cw_ref, cb_ref, wd_ref, o_ref,
                pm_ref, res_ref, xe_ref, g_ref, u_ref, act_ref):
    T = FFN_ROWS
    K = T // SUBLANES
    PS = K + SUBLANES
    CB = FFN_COLS
    NS = FFN_SLOTS
    D = x_ref.shape[2]
    NC = D // LANES
    dff = wd_ref.shape[0]
    nb = dff // CB
    i = pl.program_id(1)

    x = x_ref[0]
    for c in range(SUBLANES):
        for j in range(NC):
            pm_ref[j, c * PS:c * PS + K] = x[c * K:(c + 1) * K, j * LANES:(j + 1) * LANES]
    gain = fn_ref[...]
    for k in range(K):
        xk = jnp.concatenate([pm_ref[j, pl.ds(k, SUBLANES, stride=PS), :] for j in range(NC)], axis=1)
        res_ref[SUBLANES * k:SUBLANES * (k + 1)] = xk
        xe_ref[SUBLANES * k:SUBLANES * (k + 1)] = _rms(xk, gain).astype(BF16)
    prev = jnp.where(i == 0, 0.0, xp_ref[0])
    nxt = jnp.where(i == pl.num_programs(1) - 1, 0.0, xn_ref[0])
    xe_ref[T:T + HALO] = _rms(prev, gain).astype(BF16)
    xe_ref[T + HALO:T + 2 * HALO] = _rms(nxt, gain).astype(BF16)

    def up(c):
        xe = xe_ref[...]
        g_ref[c % NS] = _dot(xe, wup_ref[:, c * CB:(c + 1) * CB])
        u_ref[c % NS] = _dot(xe, wup_ref[:, dff + c * CB:dff + (c + 1) * CB])

    rowi = lax.broadcasted_iota(jnp.int32, (SUBLANES, CB), 0)

    def conv(ref, slot, c0):
        first, last = ref[slot, 0:SUBLANES], ref[slot, T - SUBLANES:T]
        before = jnp.where(rowi == SUBLANES - 1, ref[slot, T:T + HALO], last)
        after = jnp.where(rowi == 0, ref[slot, T + HALO:T + 2 * HALO], first)
        gm = jnp.concatenate([pltpu.roll(before, 1, 0), ref[slot, 0:T - SUBLANES]], axis=0)
        gp = jnp.concatenate([ref[slot, SUBLANES:T], pltpu.roll(after, SUBLANES - 1, 0)], axis=0)
        cw = cw_ref[:, c0:c0 + CB]
        return cb_ref[:, c0:c0 + CB] + cw[0:1] * gm + cw[1:2] * ref[slot, 0:T] + cw[2:3] * gp

    for c in range(min(NS - 1, nb)):
        up(c)
    for c in range(nb):
        if c + NS - 1 < nb:
            up(c + NS - 1)
        gc = conv(g_ref, c % NS, c * CB)
        uc = conv(u_ref, c % NS, dff + c * CB)
        act_ref[:, c * CB:(c + 1) * CB] = (gc * jax.nn.sigmoid(gc) * uc).astype(BF16)
    y = res_ref[...] + _dot(act_ref[...], wd_ref[...])
    for k in range(K):
        for j in range(NC):
            pm_ref[j, pl.ds(k, SUBLANES, stride=PS), :] = y[SUBLANES * k:SUBLANES * (k + 1),
                                                            j * LANES:(j + 1) * LANES]
    for c in range(SUBLANES):
        for j in range(NC):
            o_ref[0, c * K:(c + 1) * K, j * LANES:(j + 1) * LANES] = pm_ref[j, c * PS:c * PS + K]


def _ffn_call(x, fn, wup, cw, cb, wd):
    B, S, D = x.shape
    T = FFN_ROWS
    hb = T // HALO
    last = S // HALO - 1
    full = lambda a: pl.BlockSpec(a.shape, lambda b, i: (0,) * a.ndim, pipeline_mode=pl.Buffered(1))
    return pl.pallas_call(
        _ffn_kernel,
        grid=(B, S // T),
        in_specs=[pl.BlockSpec((1, T, D), lambda b, i: (b, i, 0)),
                  pl.BlockSpec((1, HALO, D), lambda b, i: (b, jnp.maximum(i * hb - 1, 0), 0)),
                  pl.BlockSpec((1, HALO, D), lambda b, i: (b, jnp.minimum((i + 1) * hb, last), 0)),
                  full(fn), full(wup), full(cw), full(cb), full(wd)],
        out_specs=pl.BlockSpec((1, T, D), lambda b, i: (b, i, 0)),
        out_shape=jax.ShapeDtypeStruct((B, S, D), F32),
        scratch_shapes=[pltpu.VMEM((D // LANES, T + SUBLANES * SUBLANES, LANES), F32),
                        pltpu.VMEM((T, D), F32),
                        pltpu.VMEM((T + 2 * HALO, D), BF16),
                        pltpu.VMEM((FFN_SLOTS, T + 2 * HALO, FFN_COLS), F32),
                        pltpu.VMEM((FFN_SLOTS, T + 2 * HALO, FFN_COLS), F32),
                        pltpu.VMEM((T, wd.shape[0]), BF16)],
        compiler_params=pltpu.CompilerParams(
            dimension_semantics=("parallel", "parallel"), vmem_limit_bytes=VMEM_LIMIT),
        name="ffn",
    )(x, x, x, fn, wup, cw, cb, wd)


def _block_diag_dense(w):
    nb, bs, _ = w.shape
    eye = jnp.eye(nb, dtype=w.dtype)
    return (eye[:, None, :, None] * w[:, :, None, :]).reshape(nb * bs, nb * bs)


def _layer(x, mem, pos3, attn_norm, w_in, lru_conv_w, lru_conv_b, lru_w_a, lru_b_a, lru_w_i, lru_b_i,
           lru_lambda, q_a_norm, w_uq, kv_a_norm, w_ukv, mla_q_norm, mla_k_norm, lru_out_norm,
           mla_out_norm, w_out, mem_attn_norm, mem_norm, w_mem_q, w_mem_kv, mem_q_norm, mem_k_norm,
           w_mem_o, ffn_norm, w_up, ffn_conv_w, ffn_conv_b, w_down):
    D = x.shape[-1]
    H = MLA_HEADS
    row = lambda a: a.reshape(1, -1)
    off_kr = 2 * LRU_WIDTH + Q_LORA + KV_LORA
    pad_cols = lambda a, lo, hi: jnp.pad(a, ((0, 0),) * (a.ndim - 1) + ((lo, hi),))

    win = jnp.concatenate([w_in[:, :off_kr], pad_cols(w_in[:, off_kr:], QK_NOPE, HEAD_PAD - QK_HEAD)],
                          axis=1).astype(BF16)
    wuq = w_uq.reshape(Q_LORA, H, QK_HEAD).transpose(1, 2, 0).astype(BF16)
    wkv = w_ukv.reshape(KV_LORA, H, QK_NOPE + V_DIM).transpose(1, 0, 2)
    wuk = wkv[:, :, :QK_NOPE].transpose(0, 2, 1).astype(BF16)
    wuv = pad_cols(wkv[:, :, QK_NOPE:], 0, V_ROWS - V_DIM).transpose(0, 2, 1).astype(BF16)
    gq = pad_cols(row(mla_q_norm), 0, HEAD_PAD - QK_HEAD).reshape(HEAD_PAD, 1)
    gk = pad_cols(row(mla_k_norm), 0, HEAD_PAD - QK_HEAD).reshape(HEAD_PAD, 1)
    inv = (ROPE_THETA ** (-jnp.arange(0, QK_ROPE, 2, dtype=F32) / QK_ROPE)).reshape(QK_ROPE // 2, 1)
    score_bound = math.sqrt(QK_HEAD) * LOG2E * jnp.max(jnp.abs(mla_q_norm)) * jnp.max(jnp.abs(mla_k_norm))

    xr, yg, q, k, v = _proj_call(x, pos3, row(attn_norm), win, row(q_a_norm), row(kv_a_norm),
                                 wuq, wuk, wuv, gq, gk, inv, score_bound.reshape(1, 1))

    per_tile = MXU_DIM // (LRU_WIDTH // LRU_BLOCKS)
    tiles = lambda w: jax.vmap(jax.vmap(_block_diag_dense))(
        w.reshape(2, LRU_BLOCKS // per_tile, per_tile, *w.shape[2:])).astype(BF16)
    wa, wi = tiles(lru_w_a), tiles(lru_w_i)
    r3 = lambda a: a.reshape(2, 1, LRU_WIDTH)
    h2 = _lru_call(xr, lru_conv_w, r3(lru_conv_b), wa, r3(lru_b_a), wi, r3(lru_b_i), r3(lru_lambda))

    bounded = (2.0 * score_bound <= EXP2_SAFE_RANGE).astype(jnp.int32).reshape(1)
    mla = _attn_call(bounded, q, k, v)
    km, vm = _memkv_call(mem, row(mem_norm), w_mem_kv.astype(BF16), row(mem_k_norm))

    wo = w_out.astype(BF16)
    x2 = _mix_call(x, h2, yg, mla, row(lru_out_norm), mla_out_norm.reshape(-1, 1), wo[:LRU_WIDTH], wo[LRU_WIDTH:],
                   row(mem_attn_norm), w_mem_q.astype(BF16), row(mem_q_norm), km, vm, w_mem_o.astype(BF16))

    return _ffn_call(x2, row(ffn_norm), w_up.astype(BF16), ffn_conv_w, row(ffn_conv_b), w_down.astype(BF16))


def kernel(x, mem, positions, attn_norm, w_in, lru_conv_w, lru_conv_b, lru_w_a, lru_b_a, lru_w_i, lru_b_i, lru_lambda, q_a_norm, w_uq, kv_a_norm, w_ukv, mla_q_norm, mla_k_norm, lru_out_norm, mla_out_norm, w_out, mem_attn_norm, mem_norm, w_mem_q, w_mem_kv, mem_q_norm, mem_k_norm, w_mem_o, ffn_norm, w_up, ffn_conv_w, ffn_conv_b, w_down):
    pos3 = positions[:, None, :]
    params = (attn_norm, w_in, lru_conv_w, lru_conv_b, lru_w_a, lru_b_a, lru_w_i, lru_b_i, lru_lambda,
              q_a_norm, w_uq, kv_a_norm, w_ukv, mla_q_norm, mla_k_norm, lru_out_norm, mla_out_norm, w_out,
              mem_attn_norm, mem_norm, w_mem_q, w_mem_kv, mem_q_norm, mem_k_norm, w_mem_o, ffn_norm, w_up,
              ffn_conv_w, ffn_conv_b, w_down)
    for l in range(attn_norm.shape[0]):
        x = _layer(x, mem, pos3, *[p[l] for p in params])
    return x
```

```python
import functools
import math

import jax
import jax.numpy as jnp
from jax import lax
from jax.experimental import pallas as pl
from jax.experimental.pallas import tpu as pltpu

F32 = jnp.float32
BF16 = jnp.bfloat16

EPS = 1e-6
LRU_WIDTH = 512
LRU_BLOCKS = 8
CONV_WIDTH = 4
LRU_C = 8.0
MLA_HEADS = 8
QK_NOPE = 64
QK_ROPE = 32
QK_HEAD = QK_NOPE + QK_ROPE
V_DIM = 64
Q_LORA = 256
KV_LORA = 128
MLA_WIDTH = MLA_HEADS * V_DIM
ROPE_THETA = 10000.0
MEM_HEADS = 4
MEM_HEAD_DIM = 128
MEM_WIDTH = MEM_HEADS * MEM_HEAD_DIM
FFN_CONV = 3

LANES = 128
SUBLANES = 8
MXU_DIM = 256
HEAD_PAD = LANES
LOG2E = 1.4426950408889634
EXP2_SAFE_RANGE = 100.0

PROJ_ROWS = 512
LRU_ROWS = 512
ATTN_Q = 512
ATTN_K = 512
ATTN_SPLIT = 2
ATTN_FAST_AHEAD = 2
ATTN_TILES = 4
ATTN_AHEAD = 2
ATTN_SLOTS = ATTN_AHEAD + 1
MIX_ROWS = 512
FFN_ROWS = 512
FFN_COLS = MXU_DIM
FFN_SLOTS = 3
HALO = SUBLANES

VMEM_LIMIT = 56 * 1024 * 1024


def _rms(x, g):
    return x * lax.rsqrt(jnp.mean(x * x, axis=-1, keepdims=True) + EPS) * g


def _dot(a, b):
    return jnp.dot(a, b, preferred_element_type=F32)


def _dot_nt(a, b):
    return lax.dot_general(a, b, (((1,), (1,)), ((), ())), preferred_element_type=F32)


def _proj_kernel(x_ref, pos_ref, an_ref, win_ref, qan_ref, kvan_ref, wuq_ref, wuk_ref, wuv_ref,
                 gq_ref, gk_ref, inv_ref, sb_ref,
                 xr_ref, yg_ref, q_ref, k_ref, v_ref):
    x = x_ref[0]
    h = _rms(x, an_ref[...]).astype(BF16)
    proj = _dot(h, win_ref[...])
    xr_ref[0] = proj[:, 0:LRU_WIDTH]
    yg_ref[0] = proj[:, LRU_WIDTH:2 * LRU_WIDTH]
    o = 2 * LRU_WIDTH
    cq_f = _rms(proj[:, o:o + Q_LORA], qan_ref[...])
    o += Q_LORA
    ckv_f = _rms(proj[:, o:o + KV_LORA], kvan_ref[...])
    o += KV_LORA
    kr_t = proj[:, o:o + HEAD_PAD].T
    cq_t = cq_f.T.astype(BF16)
    ckv_t = ckv_f.T.astype(BF16)
    T = x.shape[0]
    half = QK_ROPE // 2
    r0, r1, r2 = QK_NOPE, QK_NOPE + half, QK_HEAD

    ang = inv_ref[...] * pos_ref[0].astype(F32)
    cos_t = jnp.cos(ang)
    sin_t = jnp.sin(ang)

    row8 = lax.broadcasted_iota(jnp.int32, (SUBLANES, T), 0)
    zpad = jnp.zeros((HEAD_PAD - r2 - SUBLANES, T), F32)
    q_pad = jnp.concatenate([jnp.where(row8 == 0, -sb_ref[...], 0.0), zpad], axis=0)
    k_pad = jnp.concatenate([jnp.where(row8 == 0, 1.0, 0.0), zpad], axis=0)
    gq = gq_ref[0:r2] * ((QK_HEAD ** -0.5) * LOG2E)
    gk = gk_ref[0:r2]

    def norm_rope(t, g):
        ss = jnp.sum(t * t, axis=0, keepdims=True) * (1.0 / QK_HEAD)
        tn = t * lax.rsqrt(ss + EPS) * g
        t1, t2 = tn[r0:r1], tn[r1:r2]
        return [tn[0:r0], t1 * cos_t - t2 * sin_t, t1 * sin_t + t2 * cos_t]

    for hh in range(MLA_HEADS):
        qt = _dot(wuq_ref[hh], cq_t)
        q_ref[0, hh, 0] = jnp.concatenate(norm_rope(qt, gq) + [q_pad], axis=0).astype(BF16)
        kt = jnp.concatenate([_dot(wuk_ref[hh], ckv_t), kr_t[r0:r2]], axis=0)
        k_ref[0, hh] = jnp.concatenate(norm_rope(kt, gk) + [k_pad], axis=0).T.astype(BF16)
        v_ref[0, hh, 0] = _dot(wuv_ref[hh], ckv_t).astype(BF16)


def _proj_call(x, pos3, an, win, qan, kvan, wuq, wuk, wuv, gq, gk, inv, sb):
    B, S, D = x.shape
    T = PROJ_ROWS
    H = MLA_HEADS
    full = lambda a: pl.BlockSpec(a.shape, lambda b, i: (0,) * a.ndim)
    row = lambda w: pl.BlockSpec((1, T, w), lambda b, i: (b, i, 0))
    return pl.pallas_call(
        _proj_kernel,
        grid=(B, S // T),
        in_specs=[row(D), pl.BlockSpec((1, 1, T), lambda b, i: (b, 0, i))]
        + [full(a) for a in (an, win, qan, kvan, wuq, wuk, wuv, gq, gk, inv, sb)],
        out_specs=[row(LRU_WIDTH), row(LRU_WIDTH),
                   pl.BlockSpec((1, H, 1, HEAD_PAD, T), lambda b, i: (b, 0, i, 0, 0)),
                   pl.BlockSpec((1, H, T, HEAD_PAD), lambda b, i: (b, 0, i, 0)),
                   pl.BlockSpec((1, H, 1, V_DIM, T), lambda b, i: (b, 0, i, 0, 0))],
        out_shape=[jax.ShapeDtypeStruct((B, S, LRU_WIDTH), F32),
                   jax.ShapeDtypeStruct((B, S, LRU_WIDTH), F32),
                   jax.ShapeDtypeStruct((B, H, S // T, HEAD_PAD, T), BF16),
                   jax.ShapeDtypeStruct((B, H, S, HEAD_PAD), BF16),
                   jax.ShapeDtypeStruct((B, H, S // T, V_DIM, T), BF16)],
        compiler_params=pltpu.CompilerParams(
            dimension_semantics=("parallel", "parallel"), vmem_limit_bytes=VMEM_LIMIT),
        name="proj",
    )(x, pos3, an, win, qan, kvan, wuq, wuk, wuv, gq, gk, inv, sb)


def _lru_kernel(x_ref, cw_ref, cb_ref, wa_ref, ba_ref, wi_ref, bi_ref, lam_ref,
                h_ref, px_ref, ph_ref, halo_ref, carry_ref, xc_ref, ga_ref, gi_ref):
    T = LRU_ROWS
    K = T // SUBLANES
    PS = K + SUBLANES
    W = x_ref.shape[2]
    NC = W // LANES
    NH = CONV_WIDTH - 1
    d = pl.program_id(1)
    i = pl.program_id(2)

    @pl.when(i == 0)
    def _():
        halo_ref[...] = jnp.zeros_like(halo_ref)
        carry_ref[...] = jnp.zeros_like(carry_ref)

    def run(reverse):
        x = x_ref[0]
        for c in range(SUBLANES):
            for j in range(NC):
                px_ref[j, c * PS:c * PS + K] = x[c * K:(c + 1) * K, j * LANES:(j + 1) * LANES]
        slabs = [jnp.concatenate([px_ref[j, pl.ds(k, SUBLANES, stride=PS), :] for j in range(NC)], axis=1)
                 for k in range(K)]
        rowi = lax.broadcasted_iota(jnp.int32, (SUBLANES, W), 0)
        if not reverse:
            edge = [slabs[K - NH + m] for m in range(NH)]
            pre = [pltpu.roll(jnp.where(rowi == SUBLANES - 1, halo_ref[m], edge[m]), 1, 0) for m in range(NH)]
            xext = jnp.concatenate(pre + slabs, axis=0)
        else:
            edge = [slabs[m] for m in range(NH)]
            post = [pltpu.roll(jnp.where(rowi == 0, halo_ref[m], edge[m]), SUBLANES - 1, 0) for m in range(NH)]
            xext = jnp.concatenate(slabs + post, axis=0)
        for m in range(NH):
            halo_ref[m] = edge[m]
        cw = cw_ref[0]
        xc = cb_ref[0] + cw[0:1] * xext[0:T]
        for k in range(1, CONV_WIDTH):
            xc = xc + cw[k:k + 1] * xext[SUBLANES * k:SUBLANES * k + T]

        xb = xc.astype(BF16)
        xc_ref[...] = xc
        for g in range(W // MXU_DIM):
            cols = slice(g * MXU_DIM, (g + 1) * MXU_DIM)
            ga_ref[:, cols] = _dot(xb[:, cols], wa_ref[0, g]) + ba_ref[0, :, cols]
            gi_ref[:, cols] = _dot(xb[:, cols], wi_ref[0, g]) + bi_ref[0, :, cols]
        z = -lam_ref[0]
        sp = jnp.maximum(z, 0.0) + jnp.log1p(jnp.exp(-jnp.abs(z)))
        rate = jnp.broadcast_to((-LRU_C * LOG2E) * sp, (SUBLANES, W))

        hk = jnp.zeros((SUBLANES, W), F32)
        pk = jnp.ones((SUBLANES, W), F32)
        for k in (range(K - 1, -1, -1) if reverse else range(K)):
            rows = slice(SUBLANES * k, SUBLANES * (k + 1))
            ak = jnp.exp2(jax.nn.sigmoid(ga_ref[rows]) * rate)
            mult = jnp.sqrt(1.0 - ak * ak)
            hk = ak * hk + mult * (jax.nn.sigmoid(gi_ref[rows]) * xc_ref[rows])
            pk = ak * pk
            ga_ref[rows] = hk
            gi_ref[rows] = pk
        s = 1
        while s < SUBLANES:
            if not reverse:
                keep = rowi >= s
                p_sh = jnp.where(keep, pltpu.roll(pk, s, 0), 1.0)
                h_sh = jnp.where(keep, pltpu.roll(hk, s, 0), 0.0)
            else:
                keep = rowi < SUBLANES - s
                p_sh = jnp.where(keep, pltpu.roll(pk, SUBLANES - s, 0), 1.0)
                h_sh = jnp.where(keep, pltpu.roll(hk, SUBLANES - s, 0), 0.0)
            hk = pk * h_sh + hk
            pk = pk * p_sh
            s *= 2
        carry = carry_ref[...]
        leave = pk * carry + hk
        if not reverse:
            enter = jnp.where(rowi == 0, carry, pltpu.roll(leave, 1, 0))
            carry_ref[...] = jnp.broadcast_to(leave[SUBLANES - 1:SUBLANES], carry.shape)
        else:
            enter = jnp.where(rowi == SUBLANES - 1, carry, pltpu.roll(leave, SUBLANES - 1, 0))
            carry_ref[...] = jnp.broadcast_to(leave[0:1], carry.shape)
        for k in range(K):
            rows = slice(SUBLANES * k, SUBLANES * (k + 1))
            hfull = ga_ref[rows] + gi_ref[rows] * enter
            for j in range(NC):
                ph_ref[j, pl.ds(k, SUBLANES, stride=PS), :] = hfull[:, j * LANES:(j + 1) * LANES]
        for c in range(SUBLANES):
            for j in range(NC):
                h_ref[0, 0, c * K:(c + 1) * K, j * LANES:(j + 1) * LANES] = ph_ref[j, c * PS:c * PS + K]

    @pl.when(d == 0)
    def _():
        run(False)

    @pl.when(d == 1)
    def _():
        run(True)


def _lru_call(xr, cw, cb, wa, ba, wi, bi, lam):
    B, S, W = xr.shape
    T = LRU_ROWS
    n = S // T
    tile = lambda d, i: i + d * (n - 1 - 2 * i)
    par = lambda a: pl.BlockSpec((1,) + a.shape[1:], lambda b, d, i: (d,) + (0,) * (a.ndim - 1))
    return pl.pallas_call(
        _lru_kernel,
        grid=(B, 2, n),
        in_specs=[pl.BlockSpec((1, T, W), lambda b, d, i: (b, tile(d, i), 0))]
        + [par(a) for a in (cw, cb, wa, ba, wi, bi, lam)],
        out_specs=pl.BlockSpec((1, 1, T, W), lambda b, d, i: (d, b, tile(d, i), 0)),
        out_shape=jax.ShapeDtypeStruct((2, B, S, W), F32),
        scratch_shapes=[pltpu.VMEM((W // LANES, T + SUBLANES * SUBLANES, LANES), F32),
                        pltpu.VMEM((W // LANES, T + SUBLANES * SUBLANES, LANES), F32),
                        pltpu.VMEM((CONV_WIDTH - 1, SUBLANES, W), F32),
                        pltpu.VMEM((SUBLANES, W), F32),
                        pltpu.VMEM((T, W), F32), pltpu.VMEM((T, W), F32), pltpu.VMEM((T, W), F32)],
        compiler_params=pltpu.CompilerParams(
            dimension_semantics=("parallel", "arbitrary", "arbitrary"), vmem_limit_bytes=VMEM_LIMIT),
        name="lru",
    )(xr, cw, cb, wa, ba, wi, bi, lam)


def _attn_kernel(bounded_ref, q_ref, k_ref, v_ref, o_ref, s_ref):
    TQ, TK = ATTN_Q, ATTN_K
    n = k_ref.shape[2] // TK

    def key_sum(p):
        return jnp.sum(p.reshape(p.shape[0] // SUBLANES, SUBLANES, TQ), axis=0)

    def finish(t, acc, lsum):
        o_ref[0, 0, t] = acc / jnp.sum(lsum, axis=0, keepdims=True)

    def bounded_tile(t, carry):
        qt = q_ref[0, 0, t]
        acc = lsum = None
        TG = TK // ATTN_SPLIT
        ng = n * ATTN_SPLIT
        ahead = ATTN_FAST_AHEAD
        queue = [_dot(k_ref[0, 0, j * TG:(j + 1) * TG, :], qt) for j in range(ahead)]
        for j in range(ng):
            if j + ahead < ng:
                queue.append(_dot(k_ref[0, 0, (j + ahead) * TG:(j + ahead + 1) * TG, :], qt))
            p = jnp.exp2(queue.pop(0))
            part = j % ATTN_SPLIT
            vt = v_ref[0, 0, j // ATTN_SPLIT, :, part * TG:(part + 1) * TG]
            d = _dot(vt, p.astype(BF16))
            acc = d if acc is None else acc + d
            lsum = key_sum(p) if lsum is None else lsum + key_sum(p)
        finish(t, acc, lsum)
        return carry

    def online_tile(t, carry):
        qt = q_ref[0, 0, t]

        def scores(j, slot):
            s_ref[slot] = _dot(k_ref[0, 0, j * TK:(j + 1) * TK, :], qt)

        def consume(j, slot, m, acc, lsum):
            s = s_ref[slot]
            m_new = jnp.maximum(m, jnp.max(s, axis=0, keepdims=True))
            alpha = jnp.exp2(m - m_new)
            p = jnp.exp2(s - m_new)
            acc = alpha * acc + _dot(v_ref[0, 0, j], p.astype(BF16))
            return m_new, acc, alpha * lsum + key_sum(p)

        m = jnp.full((1, TQ), -jnp.inf, F32)
        acc, lsum = jnp.zeros((V_DIM, TQ), F32), jnp.zeros((SUBLANES, TQ), F32)
        for j in range(min(ATTN_AHEAD, n)):
            scores(j, j % ATTN_SLOTS)
        for j in range(n):
            if j + ATTN_AHEAD < n:
                scores(j + ATTN_AHEAD, (j + ATTN_AHEAD) % ATTN_SLOTS)
            m, acc, lsum = consume(j, j % ATTN_SLOTS, m, acc, lsum)
        finish(t, acc, lsum)
        return carry

    @pl.when(bounded_ref[0] != 0)
    def _():
        lax.fori_loop(0, ATTN_TILES, bounded_tile, 0)

    @pl.when(bounded_ref[0] == 0)
    def _():
        lax.fori_loop(0, ATTN_TILES, online_tile, 0)


def _attn_call(bounded, q, k, v):
    B, H, NQ, _, TQ = q.shape
    S = NQ * TQ
    assert TQ == ATTN_Q and v.shape[2] * ATTN_K == S and v.shape[4] == ATTN_K
    NT = ATTN_TILES
    return pl.pallas_call(
        _attn_kernel,
        grid_spec=pltpu.PrefetchScalarGridSpec(
            num_scalar_prefetch=1,
            grid=(B, H, NQ // NT),
            in_specs=[pl.BlockSpec((1, 1, NT, HEAD_PAD, TQ), lambda b, h, i, f: (b, h, i, 0, 0)),
                      pl.BlockSpec((1, 1, S, HEAD_PAD), lambda b, h, i, f: (b, h, 0, 0)),
                      pl.BlockSpec((1, 1) + v.shape[2:], lambda b, h, i, f: (b, h, 0, 0, 0))],
            out_specs=pl.BlockSpec((1, 1, NT, V_DIM, TQ), lambda b, h, i, f: (b, h, i, 0, 0)),
            scratch_shapes=[pltpu.VMEM((ATTN_SLOTS, ATTN_K, TQ), F32)]),
        out_shape=jax.ShapeDtypeStruct((B, H, NQ, V_DIM, TQ), F32),
        compiler_params=pltpu.CompilerParams(
            dimension_semantics=("parallel", "parallel", "arbitrary"), vmem_limit_bytes=VMEM_LIMIT),
        name="attn",
    )(bounded, q, k, v)


def _memkv_kernel(mem_ref, mn_ref, wkv_ref, gk_ref, k_ref, v_ref):
    mn = _rms(mem_ref[0], mn_ref[...]).astype(BF16)
    kv = _dot(mn, wkv_ref[...])
    for hh in range(MEM_HEADS):
        kh = kv[:, hh * MEM_HEAD_DIM:(hh + 1) * MEM_HEAD_DIM]
        k_ref[0, hh] = _rms(kh, gk_ref[...]).astype(BF16)
        v_ref[0, hh] = kv[:, MEM_WIDTH + hh * MEM_HEAD_DIM:MEM_WIDTH + (hh + 1) * MEM_HEAD_DIM].astype(BF16)


def _memkv_call(mem, mn, wkv, gk):
    B, M, D = mem.shape
    full = lambda a: pl.BlockSpec(a.shape, lambda b: (0,) * a.ndim)
    hb = pl.BlockSpec((1, MEM_HEADS, M, MEM_HEAD_DIM), lambda b: (b, 0, 0, 0))
    return pl.pallas_call(
        _memkv_kernel,
        grid=(B,),
        in_specs=[pl.BlockSpec((1, M, D), lambda b: (b, 0, 0)), full(mn), full(wkv), full(gk)],
        out_specs=[hb, hb],
        out_shape=[jax.ShapeDtypeStruct((B, MEM_HEADS, M, MEM_HEAD_DIM), BF16)] * 2,
        compiler_params=pltpu.CompilerParams(
            dimension_semantics=("parallel",), vmem_limit_bytes=VMEM_LIMIT),
        name="memkv",
    )(mem, mn, wkv, gk)


def _gelu_tanh(x):
    c = math.sqrt(2.0 / math.pi)
    return x * (0.5 * (1.0 + jnp.tanh(c * (x + 0.044715 * (x * x * x)))))


def _mix_kernel(x_ref, hf_ref, hb_ref, yg_ref, mla_ref, ln_ref, mnn_ref, wol_ref, wom_ref,
                man_ref, wmq_ref, gmq_ref, km_ref, vm_ref, wmo_ref, o_ref):
    x = x_ref[0]
    lru = (hf_ref[0, 0] + hb_ref[0, 0]) * _gelu_tanh(yg_ref[0])
    lru_n = _rms(lru, ln_ref[...]).astype(BF16)
    mt = jnp.concatenate([mla_ref[0, hh, 0] for hh in range(MLA_HEADS)], axis=0)
    mt = mt * lax.rsqrt(jnp.mean(mt * mt, axis=0, keepdims=True) + EPS) * mnn_ref[...]
    mla_n = mt.T.astype(BF16)
    x1 = x + _dot(lru_n, wol_ref[...]) + _dot(mla_n, wom_ref[...])

    hq = _rms(x1, man_ref[...]).astype(BF16)
    qm = _dot(hq, wmq_ref[...])
    scale = MEM_HEAD_DIM ** -0.5
    outs = []
    for hh in range(MEM_HEADS):
        qh = _rms(qm[:, hh * MEM_HEAD_DIM:(hh + 1) * MEM_HEAD_DIM], gmq_ref[...]) * scale
        s = _dot_nt(qh.astype(BF16), km_ref[0, hh])
        p = jnp.exp(s - jnp.max(s, axis=-1, keepdims=True))
        l = jnp.sum(p, axis=-1, keepdims=True)
        outs.append(_dot(p.astype(BF16), vm_ref[0, hh]) / l)
    om = jnp.concatenate(outs, axis=-1).astype(BF16)
    o_ref[0] = x1 + _dot(om, wmo_ref[...])


def _mix_call(x, h2, yg, mla, ln, mnn, wol, wom, man, wmq, gmq, km, vm, wmo):
    B, S, D = x.shape
    T = MIX_ROWS
    M = km.shape[2]
    full = lambda a: pl.BlockSpec(a.shape, lambda b, i: (0,) * a.ndim)
    row = lambda w: pl.BlockSpec((1, T, w), lambda b, i: (b, i, 0))
    hspec = lambda d: pl.BlockSpec((1, 1, T, LRU_WIDTH), lambda b, i: (d, b, i, 0))
    mspec = pl.BlockSpec((1, MEM_HEADS, M, MEM_HEAD_DIM), lambda b, i: (b, 0, 0, 0))
    return pl.pallas_call(
        _mix_kernel,
        grid=(B, S // T),
        in_specs=[row(D), hspec(0), hspec(1), row(LRU_WIDTH),
                  pl.BlockSpec((1, MLA_HEADS, 1, V_DIM, T), lambda b, i: (b, 0, i, 0, 0)), full(ln), full(mnn),
                  full(wol), full(wom), full(man), full(wmq), full(gmq), mspec, mspec, full(wmo)],
        out_specs=row(D),
        out_shape=jax.ShapeDtypeStruct((B, S, D), F32),
        compiler_params=pltpu.CompilerParams(
            dimension_semantics=("parallel", "parallel"), vmem_limit_bytes=VMEM_LIMIT),
        name="mix",
    )(x, h2, h2, yg, mla, ln, mnn, wol, wom, man, wmq, gmq, km, vm, wmo)


def _ffn_kernel(x_ref, xp_ref, xn_ref, fn_ref, wup_ref, cw_ref, cb_ref, wd_ref, o_ref,
                pm_ref, res_ref, xe_ref, g_ref, u_ref, act_ref):
    T = FFN_ROWS
    K = T // SUBLANES
    PS = K + SUBLANES
    CB = FFN_COLS
    NS = FFN_SLOTS
    D = x_ref.shape[2]
    NC = D // LANES
    dff = wd_ref.shape[0]
    nb = dff // CB
    i = pl.program_id(1)

    x = x_ref[0]
    for c in range(SUBLANES):
        for j in range(NC):
            pm_ref[j, c * PS:c * PS + K] = x[c * K:(c + 1) * K, j * LANES:(j + 1) * LANES]
    gain = fn_ref[...]
    for k in range(K):
        xk = jnp.concatenate([pm_ref[j, pl.ds(k, SUBLANES, stride=PS), :] for j in range(NC)], axis=1)
        res_ref[SUBLANES * k:SUBLANES * (k + 1)] = xk
        xe_ref[SUBLANES * k:SUBLANES * (k + 1)] = _rms(xk, gain).astype(BF16)
    prev = jnp.where(i == 0, 0.0, xp_ref[0])
    nxt = jnp.where(i == pl.num_programs(1) - 1, 0.0, xn_ref[0])
    xe_ref[T:T + HALO] = _rms(prev, gain).astype(BF16)
    xe_ref[T + HALO:T + 2 * HALO] = _rms(nxt, gain).astype(BF16)

    def up(c):
        xe = xe_ref[...]
        g_ref[c % NS] = _dot(xe, wup_ref[:, c * CB:(c + 1) * CB])
        u_ref[c % NS] = _dot(xe, wup_ref[:, dff + c * CB:dff + (c + 1) * CB])

    rowi = lax.broadcasted_iota(jnp.int32, (SUBLANES, CB), 0)

    def conv(ref, slot, c0):
        first, last = ref[slot, 0:SUBLANES], ref[slot, T - SUBLANES:T]
        before = jnp.where(rowi == SUBLANES - 1, ref[slot, T:T + HALO], last)
        after = jnp.where(rowi == 0, ref[slot, T + HALO:T + 2 * HALO], first)
        gm = jnp.concatenate([pltpu.roll(before, 1, 0), ref[slot, 0:T - SUBLANES]], axis=0)
        gp = jnp.concatenate([ref[slot, SUBLANES:T], pltpu.roll(after, SUBLANES - 1, 0)], axis=0)
        cw = cw_ref[:, c0:c0 + CB]
        return cb_ref[:, c0:c0 + CB] + cw[0:1] * gm + cw[1:2] * ref[slot, 0:T] + cw[2:3] * gp

    for c in range(min(NS - 1, nb)):
        up(c)
    for c in range(nb):
        if c + NS - 1 < nb:
            up(c + NS - 1)
        gc = conv(g_ref, c % NS, c * CB)
        uc = conv(u_ref, c % NS, dff + c * CB)
        act_ref[:, c * CB:(c + 1) * CB] = (gc * jax.nn.sigmoid(gc) * uc).astype(BF16)
    y = res_ref[...] + _dot(act_ref[...], wd_ref[...])
    for k in range(K):
        for j in range(NC):
            pm_ref[j, pl.ds(k, SUBLANES, stride=PS), :] = y[SUBLANES * k:SUBLANES * (k + 1),
                                                            j * LANES:(j + 1) * LANES]
    for c in range(SUBLANES):
        for j in range(NC):
            o_ref[0, c * K:(c + 1) * K, j * LANES:(j + 1) * LANES] = pm_ref[j, c * PS:c * PS + K]


def _ffn_call(x, fn, wup, cw, cb, wd):
    B, S, D = x.shape
    T = FFN_ROWS
    hb = T // HALO
    last = S // HALO - 1
    full = lambda a: pl.BlockSpec(a.shape, lambda b, i: (0,) * a.ndim, pipeline_mode=pl.Buffered(1))
    return pl.pallas_call(
        _ffn_kernel,
        grid=(B, S // T),
        in_specs=[pl.BlockSpec((1, T, D), lambda b, i: (b, i, 0)),
                  pl.BlockSpec((1, HALO, D), lambda b, i: (b, jnp.maximum(i * hb - 1, 0), 0)),
                  pl.BlockSpec((1, HALO, D), lambda b, i: (b, jnp.minimum((i + 1) * hb, last), 0)),
                  full(fn), full(wup), full(cw), full(cb), full(wd)],
        out_specs=pl.BlockSpec((1, T, D), lambda b, i: (b, i, 0)),
        out_shape=jax.ShapeDtypeStruct((B, S, D), F32),
        scratch_shapes=[pltpu.VMEM((D // LANES, T + SUBLANES * SUBLANES, LANES), F32),
                        pltpu.VMEM((T, D), F32),
                        pltpu.VMEM((T + 2 * HALO, D), BF16),
                        pltpu.VMEM((FFN_SLOTS, T + 2 * HALO, FFN_COLS), F32),
                        pltpu.VMEM((FFN_SLOTS, T + 2 * HALO, FFN_COLS), F32),
                        pltpu.VMEM((T, wd.shape[0]), BF16)],
        compiler_params=pltpu.CompilerParams(
            dimension_semantics=("parallel", "parallel"), vmem_limit_bytes=VMEM_LIMIT),
        name="ffn",
    )(x, x, x, fn, wup, cw, cb, wd)


def _block_diag_dense(w):
    nb, bs, _ = w.shape
    eye = jnp.eye(nb, dtype=w.dtype)
    return (eye[:, None, :, None] * w[:, :, None, :]).reshape(nb * bs, nb * bs)


def _layer(x, mem, pos3, attn_norm, w_in, lru_conv_w, lru_conv_b, lru_w_a, lru_b_a, lru_w_i, lru_b_i,
           lru_lambda, q_a_norm, w_uq, kv_a_norm, w_ukv, mla_q_norm, mla_k_norm, lru_out_norm,
           mla_out_norm, w_out, mem_attn_norm, mem_norm, w_mem_q, w_mem_kv, mem_q_norm, mem_k_norm,
           w_mem_o, ffn_norm, w_up, ffn_conv_w, ffn_conv_b, w_down):
    D = x.shape[-1]
    H = MLA_HEADS
    row = lambda a: a.reshape(1, -1)
    off_kr = 2 * LRU_WIDTH + Q_LORA + KV_LORA
    pad_cols = lambda a, lo, hi: jnp.pad(a, ((0, 0),) * (a.ndim - 1) + ((lo, hi),))

    win = jnp.concatenate([w_in[:, :off_kr], pad_cols(w_in[:, off_kr:], QK_NOPE, HEAD_PAD - QK_HEAD)],
                          axis=1).astype(BF16)
    wuq = w_uq.reshape(Q_LORA, H, QK_HEAD).transpose(1, 2, 0).astype(BF16)
    wkv = w_ukv.reshape(KV_LORA, H, QK_NOPE + V_DIM).transpose(1, 0, 2)
    wuk = wkv[:, :, :QK_NOPE].transpose(0, 2, 1).astype(BF16)
    wuv = wkv[:, :, QK_NOPE:].transpose(0, 2, 1).astype(BF16)
    gq = pad_cols(row(mla_q_norm), 0, HEAD_PAD - QK_HEAD).reshape(HEAD_PAD, 1)
    gk = pad_cols(row(mla_k_norm), 0, HEAD_PAD - QK_HEAD).reshape(HEAD_PAD, 1)
    inv = (ROPE_THETA ** (-jnp.arange(0, QK_ROPE, 2, dtype=F32) / QK_ROPE)).reshape(QK_ROPE // 2, 1)
    score_bound = math.sqrt(QK_HEAD) * LOG2E * jnp.max(jnp.abs(mla_q_norm)) * jnp.max(jnp.abs(mla_k_norm))

    xr, yg, q, k, v = _proj_call(x, pos3, row(attn_norm), win, row(q_a_norm), row(kv_a_norm),
                                 wuq, wuk, wuv, gq, gk, inv, score_bound.reshape(1, 1))

    per_tile = MXU_DIM // (LRU_WIDTH // LRU_BLOCKS)
    tiles = lambda w: jax.vmap(jax.vmap(_block_diag_dense))(
        w.reshape(2, LRU_BLOCKS // per_tile, per_tile, *w.shape[2:])).astype(BF16)
    wa, wi = tiles(lru_w_a), tiles(lru_w_i)
    r3 = lambda a: a.reshape(2, 1, LRU_WIDTH)
    h2 = _lru_call(xr, lru_conv_w, r3(lru_conv_b), wa, r3(lru_b_a), wi, r3(lru_b_i), r3(lru_lambda))

    bounded = (2.0 * score_bound <= EXP2_SAFE_RANGE).astype(jnp.int32).reshape(1)
    mla = _attn_call(bounded, q, k, v)
    km, vm = _memkv_call(mem, row(mem_norm), w_mem_kv.astype(BF16), row(mem_k_norm))

    wo = w_out.astype(BF16)
    x2 = _mix_call(x, h2, yg, mla, row(lru_out_norm), mla_out_norm.reshape(-1, 1), wo[:LRU_WIDTH], wo[LRU_WIDTH:],
                   row(mem_attn_norm), w_mem_q.astype(BF16), row(mem_q_norm), km, vm, w_mem_o.astype(BF16))

    return _ffn_call(x2, row(ffn_norm), w_up.astype(BF16), ffn_conv_w, row(ffn_conv_b), w_down.astype(BF16))


def kernel(x, mem, positions, attn_norm, w_in, lru_conv_w, lru_conv_b, lru_w_a, lru_b_a, lru_w_i, lru_b_i, lru_lambda, q_a_norm, w_uq, kv_a_norm, w_ukv, mla_q_norm, mla_k_norm, lru_out_norm, mla_out_norm, w_out, mem_attn_norm, mem_norm, w_mem_q, w_mem_kv, mem_q_norm, mem_k_norm, w_mem_o, ffn_norm, w_up, ffn_conv_w, ffn_conv_b, w_down):
    pos3 = positions[:, None, :]
    params = (attn_norm, w_in, lru_conv_w, lru_conv_b, lru_w_a, lru_b_a, lru_w_i, lru_b_i, lru_lambda,
              q_a_norm, w_uq, kv_a_norm, w_ukv, mla_q_norm, mla_k_norm, lru_out_norm, mla_out_norm, w_out,
              mem_attn_norm, mem_norm, w_mem_q, w_mem_kv, mem_q_norm, mem_k_norm, w_mem_o, ffn_norm, w_up,
              ffn_conv_w, ffn_conv_b, w_down)
    for l in range(attn_norm.shape[0]):
        x = _layer(x, mem, pos3, *[p[l] for p in params])
    return x
```

```python
import math

import jax
import jax.numpy as jnp
from jax import lax
from jax.experimental import pallas as pl
from jax.experimental.pallas import tpu as pltpu

F32 = jnp.float32
BF16 = jnp.bfloat16

EPS = 1e-6
LRU_WIDTH = 512
LRU_BLOCKS = 8
CONV_WIDTH = 4
LRU_C = 8.0
MLA_HEADS = 8
QK_NOPE = 64
QK_ROPE = 32
QK_HEAD = QK_NOPE + QK_ROPE
V_DIM = 64
Q_LORA = 256
KV_LORA = 128
MLA_WIDTH = MLA_HEADS * V_DIM
ROPE_THETA = 10000.0
MEM_HEADS = 4
MEM_HEAD_DIM = 128
MEM_WIDTH = MEM_HEADS * MEM_HEAD_DIM
FFN_CONV = 3

LANES = 128
SUBLANES = 8
MXU_DIM = 256
HEAD_PAD = LANES
LOG2E = 1.4426950408889634
EXP2_SAFE_RANGE = 100.0

PROJ_ROWS = 512
LRU_ROWS = 1024
ATTN_Q = 512
ATTN_K = 512
ATTN_SPLIT = 2
ATTN_FAST_AHEAD = 2
ATTN_TILES = 4
ATTN_UNROLL = 4
ATTN_AHEAD = 2
ATTN_SLOTS = ATTN_AHEAD + 1
MIX_ROWS = 512
FFN_ROWS = 512
FFN_COLS = MXU_DIM
FFN_SLOTS = 3
HALO = SUBLANES

VMEM_LIMIT = 56 * 1024 * 1024


def _rms(x, g):
    return x * lax.rsqrt(jnp.mean(x * x, axis=-1, keepdims=True) + EPS) * g


def _dot(a, b):
    return jnp.dot(a, b, preferred_element_type=F32)


def _dot_nt(a, b):
    return lax.dot_general(a, b, (((1,), (1,)), ((), ())), preferred_element_type=F32)


def _proj_kernel(x_ref, pos_ref, an_ref, win_ref, qan_ref, kvan_ref, wuq_ref, wuk_ref, wuv_ref,
                 gq_ref, gk_ref, inv_ref, sb_ref,
                 xr_ref, yg_ref, q_ref, k_ref, v_ref):
    x = x_ref[0]
    h = _rms(x, an_ref[...]).astype(BF16)
    proj = _dot(h, win_ref[...])
    xr_ref[0] = proj[:, 0:LRU_WIDTH]
    yg_ref[0] = proj[:, LRU_WIDTH:2 * LRU_WIDTH]
    o = 2 * LRU_WIDTH
    cq_f = _rms(proj[:, o:o + Q_LORA], qan_ref[...])
    o += Q_LORA
    ckv_f = _rms(proj[:, o:o + KV_LORA], kvan_ref[...])
    o += KV_LORA
    kr_t = proj[:, o:o + HEAD_PAD].T
    cq_t = cq_f.T.astype(BF16)
    ckv_t = ckv_f.T.astype(BF16)
    T = x.shape[0]
    half = QK_ROPE // 2
    r0, r1, r2 = QK_NOPE, QK_NOPE + half, QK_HEAD

    ang = inv_ref[...] * pos_ref[0].astype(F32)
    cos_t = jnp.cos(ang)
    sin_t = jnp.sin(ang)

    row8 = lax.broadcasted_iota(jnp.int32, (SUBLANES, T), 0)
    zpad = jnp.zeros((HEAD_PAD - r2 - SUBLANES, T), F32)
    q_pad = jnp.concatenate([jnp.where(row8 == 0, -sb_ref[...], 0.0), zpad], axis=0)
    k_pad = jnp.concatenate([jnp.where(row8 == 0, 1.0, 0.0), zpad], axis=0)
    gq = gq_ref[0:r2] * ((QK_HEAD ** -0.5) * LOG2E)
    gk = gk_ref[0:r2]

    def norm_rope(t, g):
        ss = jnp.sum(t * t, axis=0, keepdims=True) * (1.0 / QK_HEAD)
        tn = t * lax.rsqrt(ss + EPS) * g
        t1, t2 = tn[r0:r1], tn[r1:r2]
        return [tn[0:r0], t1 * cos_t - t2 * sin_t, t1 * sin_t + t2 * cos_t]

    for hh in range(MLA_HEADS):
        qt = _dot(wuq_ref[hh], cq_t)
        q_ref[0, hh, 0] = jnp.concatenate(norm_rope(qt, gq) + [q_pad], axis=0).astype(BF16)
        kt = jnp.concatenate([_dot(wuk_ref[hh], ckv_t), kr_t[r0:r2]], axis=0)
        k_ref[0, hh] = jnp.concatenate(norm_rope(kt, gk) + [k_pad], axis=0).T.astype(BF16)
        v_ref[0, hh, 0] = _dot(wuv_ref[hh], ckv_t).astype(BF16)


def _proj_call(x, pos3, an, win, qan, kvan, wuq, wuk, wuv, gq, gk, inv, sb):
    B, S, D = x.shape
    T = PROJ_ROWS
    H = MLA_HEADS
    full = lambda a: pl.BlockSpec(a.shape, lambda b, i: (0,) * a.ndim)
    row = lambda w: pl.BlockSpec((1, T, w), lambda b, i: (b, i, 0))
    return pl.pallas_call(
        _proj_kernel,
        grid=(B, S // T),
        in_specs=[row(D), pl.BlockSpec((1, 1, T), lambda b, i: (b, 0, i))]
        + [full(a) for a in (an, win, qan, kvan, wuq, wuk, wuv, gq, gk, inv, sb)],
        out_specs=[row(LRU_WIDTH), row(LRU_WIDTH),
                   pl.BlockSpec((1, H, 1, HEAD_PAD, T), lambda b, i: (b, 0, i, 0, 0)),
                   pl.BlockSpec((1, H, T, HEAD_PAD), lambda b, i: (b, 0, i, 0)),
                   pl.BlockSpec((1, H, 1, V_DIM, T), lambda b, i: (b, 0, i, 0, 0))],
        out_shape=[jax.ShapeDtypeStruct((B, S, LRU_WIDTH), F32),
                   jax.ShapeDtypeStruct((B, S, LRU_WIDTH), F32),
                   jax.ShapeDtypeStruct((B, H, S // T, HEAD_PAD, T), BF16),
                   jax.ShapeDtypeStruct((B, H, S, HEAD_PAD), BF16),
                   jax.ShapeDtypeStruct((B, H, S // T, V_DIM, T), BF16)],
        compiler_params=pltpu.CompilerParams(
            dimension_semantics=("parallel", "parallel"), vmem_limit_bytes=VMEM_LIMIT),
        name="proj",
    )(x, pos3, an, win, qan, kvan, wuq, wuk, wuv, gq, gk, inv, sb)


def _lru_kernel(x_ref, cw_ref, cb_ref, wa_ref, ba_ref, wi_ref, bi_ref, lam_ref,
                h_ref, px_ref, ph_ref, halo_ref, carry_ref, xc_ref, ga_ref, gi_ref):
    T = LRU_ROWS
    K = T // SUBLANES
    PS = K + SUBLANES
    W = x_ref.shape[2]
    NC = W // LANES
    NH = CONV_WIDTH - 1
    d = pl.program_id(1)
    i = pl.program_id(2)

    @pl.when(i == 0)
    def _():
        halo_ref[...] = jnp.zeros_like(halo_ref)
        carry_ref[...] = jnp.zeros_like(carry_ref)

    def run(reverse):
        x = x_ref[0]
        for c in range(SUBLANES):
            for j in range(NC):
                px_ref[j, c * PS:c * PS + K] = x[c * K:(c + 1) * K, j * LANES:(j + 1) * LANES]
        slabs = [jnp.concatenate([px_ref[j, pl.ds(k, SUBLANES, stride=PS), :] for j in range(NC)], axis=1)
                 for k in range(K)]
        rowi = lax.broadcasted_iota(jnp.int32, (SUBLANES, W), 0)
        if not reverse:
            edge = [slabs[K - NH + m] for m in range(NH)]
            pre = [pltpu.roll(jnp.where(rowi == SUBLANES - 1, halo_ref[m], edge[m]), 1, 0) for m in range(NH)]
            xext = jnp.concatenate(pre + slabs, axis=0)
        else:
            edge = [slabs[m] for m in range(NH)]
            post = [pltpu.roll(jnp.where(rowi == 0, halo_ref[m], edge[m]), SUBLANES - 1, 0) for m in range(NH)]
            xext = jnp.concatenate(slabs + post, axis=0)
        for m in range(NH):
            halo_ref[m] = edge[m]
        cw = cw_ref[0]
        xc = cb_ref[0] + cw[0:1] * xext[0:T]
        for k in range(1, CONV_WIDTH):
            xc = xc + cw[k:k + 1] * xext[SUBLANES * k:SUBLANES * k + T]

        xb = xc.astype(BF16)
        xc_ref[...] = xc
        for g in range(W // MXU_DIM):
            cols = slice(g * MXU_DIM, (g + 1) * MXU_DIM)
            ga_ref[:, cols] = _dot(xb[:, cols], wa_ref[0, g]) + ba_ref[0, :, cols]
            gi_ref[:, cols] = _dot(xb[:, cols], wi_ref[0, g]) + bi_ref[0, :, cols]
        z = -lam_ref[0]
        sp = jnp.maximum(z, 0.0) + jnp.log1p(jnp.exp(-jnp.abs(z)))
        rate = jnp.broadcast_to((-LRU_C * LOG2E) * sp, (SUBLANES, W))

        hk = jnp.zeros((SUBLANES, W), F32)
        pk = jnp.ones((SUBLANES, W), F32)
        for k in (range(K - 1, -1, -1) if reverse else range(K)):
            rows = slice(SUBLANES * k, SUBLANES * (k + 1))
            ak = jnp.exp2(jax.nn.sigmoid(ga_ref[rows]) * rate)
            mult = jnp.sqrt(1.0 - ak * ak)
            hk = ak * hk + mult * (jax.nn.sigmoid(gi_ref[rows]) * xc_ref[rows])
            pk = ak * pk
            ga_ref[rows] = hk
            gi_ref[rows] = pk
        s = 1
        while s < SUBLANES:
            if not reverse:
                keep = rowi >= s
                p_sh = jnp.where(keep, pltpu.roll(pk, s, 0), 1.0)
                h_sh = jnp.where(keep, pltpu.roll(hk, s, 0), 0.0)
            else:
                keep = rowi < SUBLANES - s
                p_sh = jnp.where(keep, pltpu.roll(pk, SUBLANES - s, 0), 1.0)
                h_sh = jnp.where(keep, pltpu.roll(hk, SUBLANES - s, 0), 0.0)
            hk = pk * h_sh + hk
            pk = pk * p_sh
            s *= 2
        carry = carry_ref[...]
        leave = pk * carry + hk
        if not reverse:
            enter = jnp.where(rowi == 0, carry, pltpu.roll(leave, 1, 0))
            carry_ref[...] = jnp.broadcast_to(leave[SUBLANES - 1:SUBLANES], carry.shape)
        else:
            enter = jnp.where(rowi == SUBLANES - 1, carry, pltpu.roll(leave, SUBLANES - 1, 0))
            carry_ref[...] = jnp.broadcast_to(leave[0:1], carry.shape)
        for k in range(K):
            rows = slice(SUBLANES * k, SUBLANES * (k + 1))
            hfull = ga_ref[rows] + gi_ref[rows] * enter
            for j in range(NC):
                ph_ref[j, pl.ds(k, SUBLANES, stride=PS), :] = hfull[:, j * LANES:(j + 1) * LANES]
        for c in range(SUBLANES):
            for j in range(NC):
                h_ref[0, 0, c * K:(c + 1) * K, j * LANES:(j + 1) * LANES] = ph_ref[j, c * PS:c * PS + K]

    @pl.when(d == 0)
    def _():
        run(False)

    @pl.when(d == 1)
    def _():
        run(True)


def _lru_call(xr, cw, cb, wa, ba, wi, bi, lam):
    B, S, W = xr.shape
    T = LRU_ROWS
    n = S // T
    tile = lambda d, i: i + d * (n - 1 - 2 * i)
    par = lambda a: pl.BlockSpec((1,) + a.shape[1:], lambda b, d, i: (d,) + (0,) * (a.ndim - 1))
    return pl.pallas_call(
        _lru_kernel,
        grid=(B, 2, n),
        in_specs=[pl.BlockSpec((1, T, W), lambda b, d, i: (b, tile(d, i), 0))]
        + [par(a) for a in (cw, cb, wa, ba, wi, bi, lam)],
        out_specs=pl.BlockSpec((1, 1, T, W), lambda b, d, i: (d, b, tile(d, i), 0)),
        out_shape=jax.ShapeDtypeStruct((2, B, S, W), F32),
        scratch_shapes=[pltpu.VMEM((W // LANES, T + SUBLANES * SUBLANES, LANES), F32),
                        pltpu.VMEM((W // LANES, T + SUBLANES * SUBLANES, LANES), F32),
                        pltpu.VMEM((CONV_WIDTH - 1, SUBLANES, W), F32),
                        pltpu.VMEM((SUBLANES, W), F32),
                        pltpu.VMEM((T, W), F32), pltpu.VMEM((T, W), F32), pltpu.VMEM((T, W), F32)],
        compiler_params=pltpu.CompilerParams(
            dimension_semantics=("parallel", "arbitrary", "arbitrary"), vmem_limit_bytes=VMEM_LIMIT),
        name="lru",
    )(xr, cw, cb, wa, ba, wi, bi, lam)


def _attn_kernel(bounded_ref, q_ref, k_ref, v_ref, o_ref, s_ref):
    TQ, TK = ATTN_Q, ATTN_K
    n = k_ref.shape[2] // TK

    def key_sum(p):
        return jnp.sum(p.reshape(p.shape[0] // SUBLANES, SUBLANES, TQ), axis=0)

    def finish(t, acc, lsum):
        o_ref[0, 0, t] = acc / jnp.sum(lsum, axis=0, keepdims=True)

    def bounded_tiles(tt, carry):
        U = ATTN_UNROLL
        qts = [q_ref[0, 0, tt * U + u] for u in range(U)]
        TG = TK // ATTN_SPLIT
        ng = n * ATTN_SPLIT
        items = [(u, j) for u in range(U) for j in range(ng)]
        ahead = ATTN_FAST_AHEAD
        score = lambda u, j: _dot(k_ref[0, 0, j * TG:(j + 1) * TG, :], qts[u])
        queue = [score(*items[i]) for i in range(ahead)]
        acc = lsum = None
        for i, (u, j) in enumerate(items):
            if i + ahead < len(items):
                queue.append(score(*items[i + ahead]))
            p = jnp.exp2(queue.pop(0))
            part = j % ATTN_SPLIT
            vt = v_ref[0, 0, j // ATTN_SPLIT, :, part * TG:(part + 1) * TG]
            d = _dot(vt, p.astype(BF16))
            acc = d if j == 0 else acc + d
            lsum = key_sum(p) if j == 0 else lsum + key_sum(p)
            if j == ng - 1:
                finish(tt * U + u, acc, lsum)
        return carry

    def online_tile(t, carry):
        qt = q_ref[0, 0, t]

        def scores(j, slot):
            s_ref[slot] = _dot(k_ref[0, 0, j * TK:(j + 1) * TK, :], qt)

        def consume(j, slot, m, acc, lsum):
            s = s_ref[slot]
            m_new = jnp.maximum(m, jnp.max(s, axis=0, keepdims=True))
            alpha = jnp.exp2(m - m_new)
            p = jnp.exp2(s - m_new)
            acc = alpha * acc + _dot(v_ref[0, 0, j], p.astype(BF16))
            return m_new, acc, alpha * lsum + key_sum(p)

        m = jnp.full((1, TQ), -jnp.inf, F32)
        acc, lsum = jnp.zeros((V_DIM, TQ), F32), jnp.zeros((SUBLANES, TQ), F32)
        for j in range(min(ATTN_AHEAD, n)):
            scores(j, j % ATTN_SLOTS)
        for j in range(n):
            if j + ATTN_AHEAD < n:
                scores(j + ATTN_AHEAD, (j + ATTN_AHEAD) % ATTN_SLOTS)
            m, acc, lsum = consume(j, j % ATTN_SLOTS, m, acc, lsum)
        finish(t, acc, lsum)
        return carry

    @pl.when(bounded_ref[0] != 0)
    def _():
        lax.fori_loop(0, ATTN_TILES // ATTN_UNROLL, bounded_tiles, 0)

    @pl.when(bounded_ref[0] == 0)
    def _():
        lax.fori_loop(0, ATTN_TILES, online_tile, 0)


def _attn_call(bounded, q, k, v):
    B, H, NQ, _, TQ = q.shape
    S = NQ * TQ
    assert TQ == ATTN_Q and v.shape[2] * ATTN_K == S and v.shape[4] == ATTN_K
    NT = ATTN_TILES
    return pl.pallas_call(
        _attn_kernel,
        grid_spec=pltpu.PrefetchScalarGridSpec(
            num_scalar_prefetch=1,
            grid=(B, H, NQ // NT),
            in_specs=[pl.BlockSpec((1, 1, NT, HEAD_PAD, TQ), lambda b, h, i, f: (b, h, i, 0, 0)),
                      pl.BlockSpec((1, 1, S, HEAD_PAD), lambda b, h, i, f: (b, h, 0, 0)),
                      pl.BlockSpec((1, 1) + v.shape[2:], lambda b, h, i, f: (b, h, 0, 0, 0))],
            out_specs=pl.BlockSpec((1, 1, NT, V_DIM, TQ), lambda b, h, i, f: (b, h, i, 0, 0)),
            scratch_shapes=[pltpu.VMEM((ATTN_SLOTS, ATTN_K, TQ), F32)]),
        out_shape=jax.ShapeDtypeStruct((B, H, NQ, V_DIM, TQ), F32),
        compiler_params=pltpu.CompilerParams(
            dimension_semantics=("parallel", "parallel", "arbitrary"), vmem_limit_bytes=VMEM_LIMIT),
        name="attn",
    )(bounded, q, k, v)


def _memkv_kernel(mem_ref, mn_ref, wkv_ref, gk_ref, k_ref, v_ref):
    mn = _rms(mem_ref[0], mn_ref[...]).astype(BF16)
    kv = _dot(mn, wkv_ref[...])
    for hh in range(MEM_HEADS):
        kh = kv[:, hh * MEM_HEAD_DIM:(hh + 1) * MEM_HEAD_DIM]
        k_ref[0, hh] = _rms(kh, gk_ref[...]).astype(BF16)
        v_ref[0, hh] = kv[:, MEM_WIDTH + hh * MEM_HEAD_DIM:MEM_WIDTH + (hh + 1) * MEM_HEAD_DIM].astype(BF16)


def _memkv_call(mem, mn, wkv, gk):
    B, M, D = mem.shape
    full = lambda a: pl.BlockSpec(a.shape, lambda b: (0,) * a.ndim)
    hb = pl.BlockSpec((1, MEM_HEADS, M, MEM_HEAD_DIM), lambda b: (b, 0, 0, 0))
    return pl.pallas_call(
        _memkv_kernel,
        grid=(B,),
        in_specs=[pl.BlockSpec((1, M, D), lambda b: (b, 0, 0)), full(mn), full(wkv), full(gk)],
        out_specs=[hb, hb],
        out_shape=[jax.ShapeDtypeStruct((B, MEM_HEADS, M, MEM_HEAD_DIM), BF16)] * 2,
        compiler_params=pltpu.CompilerParams(
            dimension_semantics=("parallel",), vmem_limit_bytes=VMEM_LIMIT),
        name="memkv",
    )(mem, mn, wkv, gk)


def _gelu_tanh(x):
    c = math.sqrt(2.0 / math.pi)
    return x * (0.5 * (1.0 + jnp.tanh(c * (x + 0.044715 * (x * x * x)))))


def _mix_kernel(x_ref, hf_ref, hb_ref, yg_ref, mla_ref, ln_ref, mnn_ref, wol_ref, wom_ref,
                man_ref, wmq_ref, gmq_ref, km_ref, vm_ref, wmo_ref, o_ref):
    x = x_ref[0]
    lru = (hf_ref[0, 0] + hb_ref[0, 0]) * _gelu_tanh(yg_ref[0])
    lru_n = _rms(lru, ln_ref[...]).astype(BF16)
    mt = jnp.concatenate([mla_ref[0, hh, 0] for hh in range(MLA_HEADS)], axis=0)
    mt = mt * lax.rsqrt(jnp.mean(mt * mt, axis=0, keepdims=True) + EPS) * mnn_ref[...]
    mla_n = mt.T.astype(BF16)
    x1 = x + _dot(lru_n, wol_ref[...]) + _dot(mla_n, wom_ref[...])

    hq = _rms(x1, man_ref[...]).astype(BF16)
    qm = _dot(hq, wmq_ref[...])
    scale = MEM_HEAD_DIM ** -0.5
    outs = []
    for hh in range(MEM_HEADS):
        qh = _rms(qm[:, hh * MEM_HEAD_DIM:(hh + 1) * MEM_HEAD_DIM], gmq_ref[...]) * scale
        s = _dot_nt(qh.astype(BF16), km_ref[0, hh])
        p = jnp.exp(s - jnp.max(s, axis=-1, keepdims=True))
        l = jnp.sum(p, axis=-1, keepdims=True)
        outs.append(_dot(p.astype(BF16), vm_ref[0, hh]) / l)
    om = jnp.concatenate(outs, axis=-1).astype(BF16)
    o_ref[0] = x1 + _dot(om, wmo_ref[...])


def _mix_call(x, h2, yg, mla, ln, mnn, wol, wom, man, wmq, gmq, km, vm, wmo):
    B, S, D = x.shape
    T = MIX_ROWS
    M = km.shape[2]
    full = lambda a: pl.BlockSpec(a.shape, lambda b, i: (0,) * a.ndim)
    row = lambda w: pl.BlockSpec((1, T, w), lambda b, i: (b, i, 0))
    hspec = lambda d: pl.BlockSpec((1, 1, T, LRU_WIDTH), lambda b, i: (d, b, i, 0))
    mspec = pl.BlockSpec((1, MEM_HEADS, M, MEM_HEAD_DIM), lambda b, i: (b, 0, 0, 0))
    return pl.pallas_call(
        _mix_kernel,
        grid=(B, S // T),
        in_specs=[row(D), hspec(0), hspec(1), row(LRU_WIDTH),
                  pl.BlockSpec((1, MLA_HEADS, 1, V_DIM, T), lambda b, i: (b, 0, i, 0, 0)), full(ln), full(mnn),
                  full(wol), full(wom), full(man), full(wmq), full(gmq), mspec, mspec, full(wmo)],
        out_specs=row(D),
        out_shape=jax.ShapeDtypeStruct((B, S, D), F32),
        compiler_params=pltpu.CompilerParams(
            dimension_semantics=("parallel", "parallel"), vmem_limit_bytes=VMEM_LIMIT),
        name="mix",
    )(x, h2, h2, yg, mla, ln, mnn, wol, wom, man, wmq, gmq, km, vm, wmo)


def _ffn_kernel(x_ref, xp_ref, xn_ref, fn_ref, wup_ref, cw_ref, cb_ref, wd_ref, o_ref,
                pm_ref, res_ref, xe_ref, g_ref, u_ref, act_ref):
    T = FFN_ROWS
    K = T // SUBLANES
    PS = K + SUBLANES
    CB = FFN_COLS
    NS = FFN_SLOTS
    D = x_ref.shape[2]
    NC = D // LANES
    dff = wd_ref.shape[0]
    nb = dff // CB
    i = pl.program_id(1)

    x = x_ref[0]
    for c in range(SUBLANES):
        for j in range(NC):
            pm_ref[j, c * PS:c * PS + K] = x[c * K:(c + 1) * K, j * LANES:(j + 1) * LANES]
    gain = fn_ref[...]
    for k in range(K):
        xk = jnp.concatenate([pm_ref[j, pl.ds(k, SUBLANES, stride=PS), :] for j in range(NC)], axis=1)
        res_ref[SUBLANES * k:SUBLANES * (k + 1)] = xk
        xe_ref[SUBLANES * k:SUBLANES * (k + 1)] = _rms(xk, gain).astype(BF16)
    prev = jnp.where(i == 0, 0.0, xp_ref[0])
    nxt = jnp.where(i == pl.num_programs(1) - 1, 0.0, xn_ref[0])
    xe_ref[T:T + HALO] = _rms(prev, gain).astype(BF16)
    xe_ref[T + HALO:T + 2 * HALO] = _rms(nxt, gain).astype(BF16)

    def up(c):
        xe = xe_ref[...]
        g_ref[c % NS] = _dot(xe, wup_ref[:, c * CB:(c + 1) * CB])
        u_ref[c % NS] = _dot(xe, wup_ref[:, dff + c * CB:dff + (c + 1) * CB])

    rowi = lax.broadcasted_iota(jnp.int32, (SUBLANES, CB), 0)

    def conv(ref, slot, c0):
        first, last = ref[slot, 0:SUBLANES], ref[slot, T - SUBLANES:T]
        before = jnp.where(rowi == SUBLANES - 1, ref[slot, T:T + HALO], last)
        after = jnp.where(rowi == 0, ref[slot, T + HALO:T + 2 * HALO], first)
        gm = jnp.concatenate([pltpu.roll(before, 1, 0), ref[slot, 0:T - SUBLANES]], axis=0)
        gp = jnp.concatenate([ref[slot, SUBLANES:T], pltpu.roll(after, SUBLANES - 1, 0)], axis=0)
        cw = cw_ref[:, c0:c0 + CB]
        return cb_ref[:, c0:c0 + CB] + cw[0:1] * gm + cw[1:2] * ref[slot, 0:T] + cw[2:3] * gp

    for c in range(min(NS - 1, nb)):
        up(c)
    for c in range(nb):
        if c + NS - 1 < nb:
            up(c + NS - 1)
        gc = conv(g_ref, c % NS, c * CB)
        uc = conv(u_ref, c % NS, dff + c * CB)
        act_ref[:, c * CB:(c + 1) * CB] = (gc * jax.nn.sigmoid(gc) * uc).astype(BF16)
    y = res_ref[...] + _dot(act_ref[...], wd_ref[...])
    for k in range(K):
        for j in range(NC):
            pm_ref[j, pl.ds(k, SUBLANES, stride=PS), :] = y[SUBLANES * k:SUBLANES * (k + 1),
                                                            j * LANES:(j + 1) * LANES]
    for c in range(SUBLANES):
        for j in range(NC):
            o_ref[0, c * K:(c + 1) * K, j * LANES:(j + 1) * LANES] = pm_ref[j, c * PS:c * PS + K]


def _ffn_call(x, fn, wup, cw, cb, wd):
    B, S, D = x.shape
    T = FFN_ROWS
    hb = T // HALO
    last = S // HALO - 1
    full = lambda a: pl.BlockSpec(a.shape, lambda b, i: (0,) * a.ndim, pipeline_mode=pl.Buffered(1))
    return pl.pallas_call(
        _ffn_kernel,
        grid=(B, S // T),
        in_specs=[pl.BlockSpec((1, T, D), lambda b, i: (b, i, 0)),
                  pl.BlockSpec((1, HALO, D), lambda b, i: (b, jnp.maximum(i * hb - 1, 0), 0)),
                  pl.BlockSpec((1, HALO, D), lambda b, i: (b, jnp.minimum((i + 1) * hb, last), 0)),
                  full(fn), full(wup), full(cw), full(cb), full(wd)],
        out_specs=pl.BlockSpec((1, T, D), lambda b, i: (b, i, 0)),
        out_shape=jax.ShapeDtypeStruct((B, S, D), F32),
        scratch_shapes=[pltpu.VMEM((D // LANES, T + SUBLANES * SUBLANES, LANES), F32),
                        pltpu.VMEM((T, D), F32),
                        pltpu.VMEM((T + 2 * HALO, D), BF16),
                        pltpu.VMEM((FFN_SLOTS, T + 2 * HALO, FFN_COLS), F32),
                        pltpu.VMEM((FFN_SLOTS, T + 2 * HALO, FFN_COLS), F32),
                        pltpu.VMEM((T, wd.shape[0]), BF16)],
        compiler_params=pltpu.CompilerParams(
            dimension_semantics=("parallel", "parallel"), vmem_limit_bytes=VMEM_LIMIT),
        name="ffn",
    )(x, x, x, fn, wup, cw, cb, wd)


def _block_diag_dense(w):
    nb, bs, _ = w.shape
    eye = jnp.eye(nb, dtype=w.dtype)
    return (eye[:, None, :, None] * w[:, :, None, :]).reshape(nb * bs, nb * bs)


def _layer(x, mem, pos3, attn_norm, w_in, lru_conv_w, lru_conv_b, lru_w_a, lru_b_a, lru_w_i, lru_b_i,
           lru_lambda, q_a_norm, w_uq, kv_a_norm, w_ukv, mla_q_norm, mla_k_norm, lru_out_norm,
           mla_out_norm, w_out, mem_attn_norm, mem_norm, w_mem_q, w_mem_kv, mem_q_norm, mem_k_norm,
           w_mem_o, ffn_norm, w_up, ffn_conv_w, ffn_conv_b, w_down):
    D = x.shape[-1]
    H = MLA_HEADS
    row = lambda a: a.reshape(1, -1)
    off_kr = 2 * LRU_WIDTH + Q_LORA + KV_LORA
    pad_cols = lambda a, lo, hi: jnp.pad(a, ((0, 0),) * (a.ndim - 1) + ((lo, hi),))

    win = jnp.concatenate([w_in[:, :off_kr], pad_cols(w_in[:, off_kr:], QK_NOPE, HEAD_PAD - QK_HEAD)],
                          axis=1).astype(BF16)
    wuq = w_uq.reshape(Q_LORA, H, QK_HEAD).transpose(1, 2, 0).astype(BF16)
    wkv = w_ukv.reshape(KV_LORA, H, QK_NOPE + V_DIM).transpose(1, 0, 2)
    wuk = wkv[:, :, :QK_NOPE].transpose(0, 2, 1).astype(BF16)
    wuv = wkv[:, :, QK_NOPE:].transpose(0, 2, 1).astype(BF16)
    gq = pad_cols(row(mla_q_norm), 0, HEAD_PAD - QK_HEAD).reshape(HEAD_PAD, 1)
    gk = pad_cols(row(mla_k_norm), 0, HEAD_PAD - QK_HEAD).reshape(HEAD_PAD, 1)
    inv = (ROPE_THETA ** (-jnp.arange(0, QK_ROPE, 2, dtype=F32) / QK_ROPE)).reshape(QK_ROPE // 2, 1)
    score_bound = math.sqrt(QK_HEAD) * LOG2E * jnp.max(jnp.abs(mla_q_norm)) * jnp.max(jnp.abs(mla_k_norm))

    xr, yg, q, k, v = _proj_call(x, pos3, row(attn_norm), win, row(q_a_norm), row(kv_a_norm),
                                 wuq, wuk, wuv, gq, gk, inv, score_bound.reshape(1, 1))

    per_tile = MXU_DIM // (LRU_WIDTH // LRU_BLOCKS)
    tiles = lambda w: jax.vmap(jax.vmap(_block_diag_dense))(
        w.reshape(2, LRU_BLOCKS // per_tile, per_tile, *w.shape[2:])).astype(BF16)
    wa, wi = tiles(lru_w_a), tiles(lru_w_i)
    r3 = lambda a: a.reshape(2, 1, LRU_WIDTH)
    h2 = _lru_call(xr, lru_conv_w, r3(lru_conv_b), wa, r3(lru_b_a), wi, r3(lru_b_i), r3(lru_lambda))

    bounded = (2.0 * score_bound <= EXP2_SAFE_RANGE).astype(jnp.int32).reshape(1)
    mla = _attn_call(bounded, q, k, v)
    km, vm = _memkv_call(mem, row(mem_norm), w_mem_kv.astype(BF16), row(mem_k_norm))

    wo = w_out.astype(BF16)
    x2 = _mix_call(x, h2, yg, mla, row(lru_out_norm), mla_out_norm.reshape(-1, 1), wo[:LRU_WIDTH], wo[LRU_WIDTH:],
                   row(mem_attn_norm), w_mem_q.astype(BF16), row(mem_q_norm), km, vm, w_mem_o.astype(BF16))

    return _ffn_call(x2, row(ffn_norm), w_up.astype(BF16), ffn_conv_w, row(ffn_conv_b), w_down.astype(BF16))


def kernel(x, mem, positions, attn_norm, w_in, lru_conv_w, lru_conv_b, lru_w_a, lru_b_a, lru_w_i, lru_b_i, lru_lambda, q_a_norm, w_uq, kv_a_norm, w_ukv, mla_q_norm, mla_k_norm, lru_out_norm, mla_out_norm, w_out, mem_attn_norm, mem_norm, w_mem_q, w_mem_kv, mem_q_norm, mem_k_norm, w_mem_o, ffn_norm, w_up, ffn_conv_w, ffn_conv_b, w_down):
    pos3 = positions[:, None, :]
    params = (attn_norm, w_in, lru_conv_w, lru_conv_b, lru_w_a, lru_b_a, lru_w_i, lru_b_i, lru_lambda,
              q_a_norm, w_uq, kv_a_norm, w_ukv, mla_q_norm, mla_k_norm, lru_out_norm, mla_out_norm, w_out,
              mem_attn_norm, mem_norm, w_mem_q, w_mem_kv, mem_q_norm, mem_k_norm, w_mem_o, ffn_norm, w_up,
              ffn_conv_w, ffn_conv_b, w_down)
    for l in range(attn_norm.shape[0]):
        x = _layer(x, mem, pos3, *[p[l] for p in params])
    return x
```

```python
import math

import jax
import jax.numpy as jnp
from jax import lax
from jax.experimental import pallas as pl
from jax.experimental.pallas import tpu as pltpu

F32 = jnp.float32
BF16 = jnp.bfloat16

EPS = 1e-6
LRU_WIDTH = 512
LRU_BLOCKS = 8
CONV_WIDTH = 4
LRU_C = 8.0
MLA_HEADS = 8
QK_NOPE = 64
QK_ROPE = 32
QK_HEAD = QK_NOPE + QK_ROPE
V_DIM = 64
Q_LORA = 256
KV_LORA = 128
MLA_WIDTH = MLA_HEADS * V_DIM
ROPE_THETA = 10000.0
MEM_HEADS = 4
MEM_HEAD_DIM = 128
MEM_WIDTH = MEM_HEADS * MEM_HEAD_DIM
FFN_CONV = 3

LANES = 128
SUBLANES = 8
MXU_DIM = 256
HEAD_PAD = LANES
LOG2E = 1.4426950408889634
EXP2_SAFE_RANGE = 100.0

PROJ_ROWS = 512
LRU_ROWS = 1024
ATTN_Q = 512
ATTN_K = 512
ATTN_SPLIT = 2
ATTN_FAST_AHEAD = 2
ATTN_TILES = 4
ATTN_UNROLL = 4
ATTN_AHEAD = 2
ATTN_SLOTS = ATTN_AHEAD + 1
MIX_ROWS = 512
FFN_ROWS = 512
FFN_COLS = MXU_DIM
FFN_SLOTS = 3
HALO = SUBLANES

VMEM_LIMIT = 56 * 1024 * 1024


def _rms(x, g):
    return x * lax.rsqrt(jnp.mean(x * x, axis=-1, keepdims=True) + EPS) * g


def _dot(a, b):
    return jnp.dot(a, b, preferred_element_type=F32)


def _dot_nt(a, b):
    return lax.dot_general(a, b, (((1,), (1,)), ((), ())), preferred_element_type=F32)


def _proj_kernel(x_ref, pos_ref, an_ref, win_ref, qan_ref, kvan_ref, wuq_ref, wuk_ref, wuv_ref,
                 gq_ref, gk_ref, inv_ref, sb_ref,
                 xr_ref, yg_ref, q_ref, k_ref, v_ref):
    x = x_ref[0]
    h = _rms(x, an_ref[...]).astype(BF16)
    proj = _dot(h, win_ref[...])
    xr_ref[0] = proj[:, 0:LRU_WIDTH]
    yg_ref[0] = proj[:, LRU_WIDTH:2 * LRU_WIDTH]
    o = 2 * LRU_WIDTH
    cq_f = _rms(proj[:, o:o + Q_LORA], qan_ref[...])
    o += Q_LORA
    ckv_f = _rms(proj[:, o:o + KV_LORA], kvan_ref[...])
    o += KV_LORA
    kr_t = proj[:, o:o + HEAD_PAD].T
    cq_t = cq_f.T.astype(BF16)
    ckv_t = ckv_f.T.astype(BF16)
    T = x.shape[0]
    half = QK_ROPE // 2
    r0, r1, r2 = QK_NOPE, QK_NOPE + half, QK_HEAD

    ang = inv_ref[...] * pos_ref[0].astype(F32)
    cos_t = jnp.cos(ang)
    sin_t = jnp.sin(ang)

    row8 = lax.broadcasted_iota(jnp.int32, (SUBLANES, T), 0)
    zpad = jnp.zeros((HEAD_PAD - r2 - SUBLANES, T), F32)
    q_pad = jnp.concatenate([jnp.where(row8 == 0, -sb_ref[...], 0.0), zpad], axis=0)
    k_pad = jnp.concatenate([jnp.where(row8 == 0, 1.0, 0.0), zpad], axis=0)
    gq = gq_ref[0:r2] * ((QK_HEAD ** -0.5) * LOG2E)
    gk = gk_ref[0:r2]

    def norm_rope(t, g):
        ss = jnp.sum(t * t, axis=0, keepdims=True) * (1.0 / QK_HEAD)
        tn = t * lax.rsqrt(ss + EPS) * g
        t1, t2 = tn[r0:r1], tn[r1:r2]
        return [tn[0:r0], t1 * cos_t - t2 * sin_t, t1 * sin_t + t2 * cos_t]

    for hh in range(MLA_HEADS):
        qt = _dot(wuq_ref[hh], cq_t)
        q_ref[0, hh, 0] = jnp.concatenate(norm_rope(qt, gq) + [q_pad], axis=0).astype(BF16)
        kt = jnp.concatenate([_dot(wuk_ref[hh], ckv_t), kr_t[r0:r2]], axis=0)
        k_ref[0, hh] = jnp.concatenate(norm_rope(kt, gk) + [k_pad], axis=0).T.astype(BF16)
        v_ref[0, hh, 0] = _dot(wuv_ref[hh], ckv_t).astype(BF16)


def _proj_call(x, pos3, an, win, qan, kvan, wuq, wuk, wuv, gq, gk, inv, sb):
    B, S, D = x.shape
    T = PROJ_ROWS
    H = MLA_HEADS
    full = lambda a: pl.BlockSpec(a.shape, lambda b, i: (0,) * a.ndim)
    row = lambda w: pl.BlockSpec((1, T, w), lambda b, i: (b, i, 0))
    return pl.pallas_call(
        _proj_kernel,
        grid=(B, S // T),
        in_specs=[row(D), pl.BlockSpec((1, 1, T), lambda b, i: (b, 0, i))]
        + [full(a) for a in (an, win, qan, kvan, wuq, wuk, wuv, gq, gk, inv, sb)],
        out_specs=[row(LRU_WIDTH), row(LRU_WIDTH),
                   pl.BlockSpec((1, H, 1, HEAD_PAD, T), lambda b, i: (b, 0, i, 0, 0)),
                   pl.BlockSpec((1, H, T, HEAD_PAD), lambda b, i: (b, 0, i, 0)),
                   pl.BlockSpec((1, H, 1, V_DIM, T), lambda b, i: (b, 0, i, 0, 0))],
        out_shape=[jax.ShapeDtypeStruct((B, S, LRU_WIDTH), F32),
                   jax.ShapeDtypeStruct((B, S, LRU_WIDTH), F32),
                   jax.ShapeDtypeStruct((B, H, S // T, HEAD_PAD, T), BF16),
                   jax.ShapeDtypeStruct((B, H, S, HEAD_PAD), BF16),
                   jax.ShapeDtypeStruct((B, H, S // T, V_DIM, T), BF16)],
        compiler_params=pltpu.CompilerParams(
            dimension_semantics=("parallel", "parallel"), vmem_limit_bytes=VMEM_LIMIT),
        name="proj",
    )(x, pos3, an, win, qan, kvan, wuq, wuk, wuv, gq, gk, inv, sb)


def _lru_kernel(x_ref, cw_ref, cb_ref, wa_ref, ba_ref, wi_ref, bi_ref, lam_ref,
                h_ref, px_ref, ph_ref, halo_ref, carry_ref, xc_ref, ga_ref, gi_ref):
    T = LRU_ROWS
    K = T // SUBLANES
    PS = K + SUBLANES
    W = x_ref.shape[2]
    NC = W // LANES
    NH = CONV_WIDTH - 1
    d = pl.program_id(1)
    i = pl.program_id(2)

    @pl.when(i == 0)
    def _():
        halo_ref[...] = jnp.zeros_like(halo_ref)
        carry_ref[...] = jnp.zeros_like(carry_ref)

    def run(reverse):
        x = x_ref[0]
        for c in range(SUBLANES):
            for j in range(NC):
                px_ref[j, c * PS:c * PS + K] = x[c * K:(c + 1) * K, j * LANES:(j + 1) * LANES]
        slabs = [jnp.concatenate([px_ref[j, pl.ds(k, SUBLANES, stride=PS), :] for j in range(NC)], axis=1)
                 for k in range(K)]
        rowi = lax.broadcasted_iota(jnp.int32, (SUBLANES, W), 0)
        if not reverse:
            edge = [slabs[K - NH + m] for m in range(NH)]
            pre = [pltpu.roll(jnp.where(rowi == SUBLANES - 1, halo_ref[m], edge[m]), 1, 0) for m in range(NH)]
            xext = jnp.concatenate(pre + slabs, axis=0)
        else:
            edge = [slabs[m] for m in range(NH)]
            post = [pltpu.roll(jnp.where(rowi == 0, halo_ref[m], edge[m]), SUBLANES - 1, 0) for m in range(NH)]
            xext = jnp.concatenate(slabs + post, axis=0)
        for m in range(NH):
            halo_ref[m] = edge[m]
        cw = cw_ref[0]
        xc = cb_ref[0] + cw[0:1] * xext[0:T]
        for k in range(1, CONV_WIDTH):
            xc = xc + cw[k:k + 1] * xext[SUBLANES * k:SUBLANES * k + T]

        xb = xc.astype(BF16)
        xc_ref[...] = xc
        for g in range(W // MXU_DIM):
            cols = slice(g * MXU_DIM, (g + 1) * MXU_DIM)
            ga_ref[:, cols] = _dot(xb[:, cols], wa_ref[0, g]) + ba_ref[0, :, cols]
            gi_ref[:, cols] = _dot(xb[:, cols], wi_ref[0, g]) + bi_ref[0, :, cols]
        z = -lam_ref[0]
        sp = jnp.maximum(z, 0.0) + jnp.log1p(jnp.exp(-jnp.abs(z)))
        rate = jnp.broadcast_to((-LRU_C * LOG2E) * sp, (SUBLANES, W))

        hk = jnp.zeros((SUBLANES, W), F32)
        pk = jnp.ones((SUBLANES, W), F32)
        for k in (range(K - 1, -1, -1) if reverse else range(K)):
            rows = slice(SUBLANES * k, SUBLANES * (k + 1))
            ak = jnp.exp2(jax.nn.sigmoid(ga_ref[rows]) * rate)
            mult = jnp.sqrt(1.0 - ak * ak)
            hk = ak * hk + mult * (jax.nn.sigmoid(gi_ref[rows]) * xc_ref[rows])
            pk = ak * pk
            ga_ref[rows] = hk
            gi_ref[rows] = pk
        s = 1
        while s < SUBLANES:
            if not reverse:
                keep = rowi >= s
                p_sh = jnp.where(keep, pltpu.roll(pk, s, 0), 1.0)
                h_sh = jnp.where(keep, pltpu.roll(hk, s, 0), 0.0)
            else:
                keep = rowi < SUBLANES - s
                p_sh = jnp.where(keep, pltpu.roll(pk, SUBLANES - s, 0), 1.0)
                h_sh = jnp.where(keep, pltpu.roll(hk, SUBLANES - s, 0), 0.0)
            hk = pk * h_sh + hk
            pk = pk * p_sh
            s *= 2
        carry = carry_ref[...]
        leave = pk * carry + hk
        if not reverse:
            enter = jnp.where(rowi == 0, carry, pltpu.roll(leave, 1, 0))
            carry_ref[...] = jnp.broadcast_to(leave[SUBLANES - 1:SUBLANES], carry.shape)
        else:
            enter = jnp.where(rowi == SUBLANES - 1, carry, pltpu.roll(leave, SUBLANES - 1, 0))
            carry_ref[...] = jnp.broadcast_to(leave[0:1], carry.shape)
        for k in range(K):
            rows = slice(SUBLANES * k, SUBLANES * (k + 1))
            hfull = ga_ref[rows] + gi_ref[rows] * enter
            for j in range(NC):
                ph_ref[j, pl.ds(k, SUBLANES, stride=PS), :] = hfull[:, j * LANES:(j + 1) * LANES]
        for c in range(SUBLANES):
            for j in range(NC):
                h_ref[0, 0, c * K:(c + 1) * K, j * LANES:(j + 1) * LANES] = ph_ref[j, c * PS:c * PS + K]

    @pl.when(d == 0)
    def _():
        run(False)

    @pl.when(d == 1)
    def _():
        run(True)


def _lru_call(xr, cw, cb, wa, ba, wi, bi, lam):
    B, S, W = xr.shape
    T = LRU_ROWS
    n = S // T
    tile = lambda d, i: i + d * (n - 1 - 2 * i)
    par = lambda a: pl.BlockSpec((1,) + a.shape[1:], lambda b, d, i: (d,) + (0,) * (a.ndim - 1))
    return pl.pallas_call(
        _lru_kernel,
        grid=(B, 2, n),
        in_specs=[pl.BlockSpec((1, T, W), lambda b, d, i: (b, tile(d, i), 0))]
        + [par(a) for a in (cw, cb, wa, ba, wi, bi, lam)],
        out_specs=pl.BlockSpec((1, 1, T, W), lambda b, d, i: (d, b, tile(d, i), 0)),
        out_shape=jax.ShapeDtypeStruct((2, B, S, W), F32),
        scratch_shapes=[pltpu.VMEM((W // LANES, T + SUBLANES * SUBLANES, LANES), F32),
                        pltpu.VMEM((W // LANES, T + SUBLANES * SUBLANES, LANES), F32),
                        pltpu.VMEM((CONV_WIDTH - 1, SUBLANES, W), F32),
                        pltpu.VMEM((SUBLANES, W), F32),
                        pltpu.VMEM((T, W), F32), pltpu.VMEM((T, W), F32), pltpu.VMEM((T, W), F32)],
        compiler_params=pltpu.CompilerParams(
            dimension_semantics=("parallel", "arbitrary", "arbitrary"), vmem_limit_bytes=VMEM_LIMIT),
        name="lru",
    )(xr, cw, cb, wa, ba, wi, bi, lam)


def _attn_kernel(bounded_ref, q_ref, k_ref, v_ref, o_ref, s_ref):
    TQ, TK = ATTN_Q, ATTN_K
    n = k_ref.shape[2] // TK

    def key_sum(p):
        return jnp.sum(p.reshape(p.shape[0] // SUBLANES, SUBLANES, TQ), axis=0)

    def finish(t, acc, lsum):
        o_ref[0, 0, t] = acc / jnp.sum(lsum, axis=0, keepdims=True)

    def bounded_tiles(tt, carry):
        U = ATTN_UNROLL
        qts = [q_ref[0, 0, tt * U + u] for u in range(U)]
        TG = TK // ATTN_SPLIT
        ng = n * ATTN_SPLIT
        items = [(u, j) for u in range(U) for j in range(ng)]
        ahead = ATTN_FAST_AHEAD
        score = lambda u, j: _dot(k_ref[0, 0, j * TG:(j + 1) * TG, :], qts[u])
        queue = [score(*items[i]) for i in range(ahead)]
        acc = lsum = None
        for i, (u, j) in enumerate(items):
            if i + ahead < len(items):
                queue.append(score(*items[i + ahead]))
            p = jnp.exp2(queue.pop(0))
            part = j % ATTN_SPLIT
            vt = v_ref[0, 0, j // ATTN_SPLIT, :, part * TG:(part + 1) * TG]
            d = _dot(vt, p.astype(BF16))
            acc = d if j == 0 else acc + d
            lsum = key_sum(p) if j == 0 else lsum + key_sum(p)
            if j == ng - 1:
                finish(tt * U + u, acc, lsum)
        return carry

    def online_tile(t, carry):
        qt = q_ref[0, 0, t]

        def scores(j, slot):
            s_ref[slot] = _dot(k_ref[0, 0, j * TK:(j + 1) * TK, :], qt)

        def consume(j, slot, m, acc, lsum):
            s = s_ref[slot]
            m_new = jnp.maximum(m, jnp.max(s, axis=0, keepdims=True))
            alpha = jnp.exp2(m - m_new)
            p = jnp.exp2(s - m_new)
            acc = alpha * acc + _dot(v_ref[0, 0, j], p.astype(BF16))
            return m_new, acc, alpha * lsum + key_sum(p)

        m = jnp.full((1, TQ), -jnp.inf, F32)
        acc, lsum = jnp.zeros((V_DIM, TQ), F32), jnp.zeros((SUBLANES, TQ), F32)
        for j in range(min(ATTN_AHEAD, n)):
            scores(j, j % ATTN_SLOTS)
        for j in range(n):
            if j + ATTN_AHEAD < n:
                scores(j + ATTN_AHEAD, (j + ATTN_AHEAD) % ATTN_SLOTS)
            m, acc, lsum = consume(j, j % ATTN_SLOTS, m, acc, lsum)
        finish(t, acc, lsum)
        return carry

    @pl.when(bounded_ref[0] != 0)
    def _():
        lax.fori_loop(0, ATTN_TILES // ATTN_UNROLL, bounded_tiles, 0)

    @pl.when(bounded_ref[0] == 0)
    def _():
        lax.fori_loop(0, ATTN_TILES, online_tile, 0)


def _attn_call(bounded, q, k, v):
    B, H, NQ, _, TQ = q.shape
    S = NQ * TQ
    assert TQ == ATTN_Q and v.shape[2] * ATTN_K == S and v.shape[4] == ATTN_K
    NT = ATTN_TILES
    return pl.pallas_call(
        _attn_kernel,
        grid_spec=pltpu.PrefetchScalarGridSpec(
            num_scalar_prefetch=1,
            grid=(B, H, NQ // NT),
            in_specs=[pl.BlockSpec((1, 1, NT, HEAD_PAD, TQ), lambda b, h, i, f: (b, h, i, 0, 0)),
                      pl.BlockSpec((1, 1, S, HEAD_PAD), lambda b, h, i, f: (b, h, 0, 0)),
                      pl.BlockSpec((1, 1) + v.shape[2:], lambda b, h, i, f: (b, h, 0, 0, 0))],
            out_specs=pl.BlockSpec((1, 1, NT, V_DIM, TQ), lambda b, h, i, f: (b, h, i, 0, 0)),
            scratch_shapes=[pltpu.VMEM((ATTN_SLOTS, ATTN_K, TQ), F32)]),
        out_shape=jax.ShapeDtypeStruct((B, H, NQ, V_DIM, TQ), F32),
        compiler_params=pltpu.CompilerParams(
            dimension_semantics=("parallel", "parallel", "arbitrary"), vmem_limit_bytes=VMEM_LIMIT),
        name="attn",
    )(bounded, q, k, v)


def _gelu_tanh(x):
    c = math.sqrt(2.0 / math.pi)
    return x * (0.5 * (1.0 + jnp.tanh(c * (x + 0.044715 * (x * x * x)))))


def _mix_kernel(x_ref, hf_ref, hb_ref, yg_ref, mla_ref, ln_ref, mnn_ref, wol_ref, wom_ref,
                man_ref, wmq_ref, gmq_ref, mem_ref, mn_ref, wkv_ref, gmk_ref, wmo_ref, o_ref, km_ref, vm_ref):
    @pl.when(pl.program_id(1) == 0)
    def _():
        kv = _dot(_rms(mem_ref[0], mn_ref[...]).astype(BF16), wkv_ref[...])
        for hh in range(MEM_HEADS):
            kh = kv[:, hh * MEM_HEAD_DIM:(hh + 1) * MEM_HEAD_DIM]
            km_ref[0, hh] = _rms(kh, gmk_ref[...]).astype(BF16)
            vm_ref[0, hh] = kv[:, MEM_WIDTH + hh * MEM_HEAD_DIM:MEM_WIDTH + (hh + 1) * MEM_HEAD_DIM].astype(BF16)

    x = x_ref[0]
    lru = (hf_ref[0, 0] + hb_ref[0, 0]) * _gelu_tanh(yg_ref[0])
    lru_n = _rms(lru, ln_ref[...]).astype(BF16)
    mt = jnp.concatenate([mla_ref[0, hh, 0] for hh in range(MLA_HEADS)], axis=0)
    mt = mt * lax.rsqrt(jnp.mean(mt * mt, axis=0, keepdims=True) + EPS) * mnn_ref[...]
    mla_n = mt.T.astype(BF16)
    x1 = x + _dot(lru_n, wol_ref[...]) + _dot(mla_n, wom_ref[...])

    hq = _rms(x1, man_ref[...]).astype(BF16)
    qm = _dot(hq, wmq_ref[...])
    scale = MEM_HEAD_DIM ** -0.5
    outs = []
    for hh in range(MEM_HEADS):
        qh = _rms(qm[:, hh * MEM_HEAD_DIM:(hh + 1) * MEM_HEAD_DIM], gmq_ref[...]) * scale
        s = _dot_nt(qh.astype(BF16), km_ref[0, hh])
        p = jnp.exp(s - jnp.max(s, axis=-1, keepdims=True))
        l = jnp.sum(p, axis=-1, keepdims=True)
        outs.append(_dot(p.astype(BF16), vm_ref[0, hh]) / l)
    om = jnp.concatenate(outs, axis=-1).astype(BF16)
    o_ref[0] = x1 + _dot(om, wmo_ref[...])


def _mix_call(x, h2, yg, mla, ln, mnn, wol, wom, man, wmq, gmq, mem, mn, wkv, gmk, wmo):
    B, S, D = x.shape
    T = MIX_ROWS
    M = mem.shape[1]
    full = lambda a: pl.BlockSpec(a.shape, lambda b, i: (0,) * a.ndim)
    row = lambda w: pl.BlockSpec((1, T, w), lambda b, i: (b, i, 0))
    hspec = lambda d: pl.BlockSpec((1, 1, T, LRU_WIDTH), lambda b, i: (d, b, i, 0))
    kv_scratch = pltpu.VMEM((1, MEM_HEADS, M, MEM_HEAD_DIM), BF16)
    return pl.pallas_call(
        _mix_kernel,
        grid=(B, S // T),
        in_specs=[row(D), hspec(0), hspec(1), row(LRU_WIDTH),
                  pl.BlockSpec((1, MLA_HEADS, 1, V_DIM, T), lambda b, i: (b, 0, i, 0, 0)), full(ln), full(mnn),
                  full(wol), full(wom), full(man), full(wmq), full(gmq),
                  pl.BlockSpec((1, M, D), lambda b, i: (b, 0, 0)), full(mn), full(wkv), full(gmk), full(wmo)],
        out_specs=row(D),
        out_shape=jax.ShapeDtypeStruct((B, S, D), F32),
        scratch_shapes=[kv_scratch, kv_scratch],
        compiler_params=pltpu.CompilerParams(
            dimension_semantics=("parallel", "arbitrary"), vmem_limit_bytes=VMEM_LIMIT),
        name="mix",
    )(x, h2, h2, yg, mla, ln, mnn, wol, wom, man, wmq, gmq, mem, mn, wkv, gmk, wmo)


def _ffn_kernel(x_ref, xp_ref, xn_ref, fn_ref, wup_ref, cw_ref, cb_ref, wd_ref, o_ref,
                pm_ref, res_ref, xe_ref, g_ref, u_ref, act_ref):
    T = FFN_ROWS
    K = T // SUBLANES
    PS = K + SUBLANES
    CB = FFN_COLS
    NS = FFN_SLOTS
    D = x_ref.shape[2]
    NC = D // LANES
    dff = wd_ref.shape[0]
    nb = dff // CB
    i = pl.program_id(1)

    x = x_ref[0]
    for c in range(SUBLANES):
        for j in range(NC):
            pm_ref[j, c * PS:c * PS + K] = x[c * K:(c + 1) * K, j * LANES:(j + 1) * LANES]
    gain = fn_ref[...]
    for k in range(K):
        xk = jnp.concatenate([pm_ref[j, pl.ds(k, SUBLANES, stride=PS), :] for j in range(NC)], axis=1)
        res_ref[SUBLANES * k:SUBLANES * (k + 1)] = xk
        xe_ref[SUBLANES * k:SUBLANES * (k + 1)] = _rms(xk, gain).astype(BF16)
    prev = jnp.where(i == 0, 0.0, xp_ref[0])
    nxt = jnp.where(i == pl.num_programs(1) - 1, 0.0, xn_ref[0])
    xe_ref[T:T + HALO] = _rms(prev, gain).astype(BF16)
    xe_ref[T + HALO:T + 2 * HALO] = _rms(nxt, gain).astype(BF16)

    def up(c):
        xe = xe_ref[...]
        g_ref[c % NS] = _dot(xe, wup_ref[:, c * CB:(c + 1) * CB])
        u_ref[c % NS] = _dot(xe, wup_ref[:, dff + c * CB:dff + (c + 1) * CB])

    rowi = lax.broadcasted_iota(jnp.int32, (SUBLANES, CB), 0)

    def conv(ref, slot, c0):
        first, last = ref[slot, 0:SUBLANES], ref[slot, T - SUBLANES:T]
        before = jnp.where(rowi == SUBLANES - 1, ref[slot, T:T + HALO], last)
        after = jnp.where(rowi == 0, ref[slot, T + HALO:T + 2 * HALO], first)
        gm = jnp.concatenate([pltpu.roll(before, 1, 0), ref[slot, 0:T - SUBLANES]], axis=0)
        gp = jnp.concatenate([ref[slot, SUBLANES:T], pltpu.roll(after, SUBLANES - 1, 0)], axis=0)
        cw = cw_ref[:, c0:c0 + CB]
        return cb_ref[:, c0:c0 + CB] + cw[0:1] * gm + cw[1:2] * ref[slot, 0:T] + cw[2:3] * gp

    for c in range(min(NS - 1, nb)):
        up(c)
    for c in range(nb):
        if c + NS - 1 < nb:
            up(c + NS - 1)
        gc = conv(g_ref, c % NS, c * CB)
        uc = conv(u_ref, c % NS, dff + c * CB)
        act_ref[:, c * CB:(c + 1) * CB] = (gc * jax.nn.sigmoid(gc) * uc).astype(BF16)
    y = res_ref[...] + _dot(act_ref[...], wd_ref[...])
    for k in range(K):
        for j in range(NC):
            pm_ref[j, pl.ds(k, SUBLANES, stride=PS), :] = y[SUBLANES * k:SUBLANES * (k + 1),
                                                            j * LANES:(j + 1) * LANES]
    for c in range(SUBLANES):
        for j in range(NC):
            o_ref[0, c * K:(c + 1) * K, j * LANES:(j + 1) * LANES] = pm_ref[j, c * PS:c * PS + K]


def _ffn_call(x, fn, wup, cw, cb, wd):
    B, S, D = x.shape
    T = FFN_ROWS
    hb = T // HALO
    last = S // HALO - 1
    full = lambda a: pl.BlockSpec(a.shape, lambda b, i: (0,) * a.ndim, pipeline_mode=pl.Buffered(1))
    return pl.pallas_call(
        _ffn_kernel,
        grid=(B, S // T),
        in_specs=[pl.BlockSpec((1, T, D), lambda b, i: (b, i, 0)),
                  pl.BlockSpec((1, HALO, D), lambda b, i: (b, jnp.maximum(i * hb - 1, 0), 0)),
                  pl.BlockSpec((1, HALO, D), lambda b, i: (b, jnp.minimum((i + 1) * hb, last), 0)),
                  full(fn), full(wup), full(cw), full(cb), full(wd)],
        out_specs=pl.BlockSpec((1, T, D), lambda b, i: (b, i, 0)),
        out_shape=jax.ShapeDtypeStruct((B, S, D), F32),
        scratch_shapes=[pltpu.VMEM((D // LANES, T + SUBLANES * SUBLANES, LANES), F32),
                        pltpu.VMEM((T, D), F32),
                        pltpu.VMEM((T + 2 * HALO, D), BF16),
                        pltpu.VMEM((FFN_SLOTS, T + 2 * HALO, FFN_COLS), F32),
                        pltpu.VMEM((FFN_SLOTS, T + 2 * HALO, FFN_COLS), F32),
                        pltpu.VMEM((T, wd.shape[0]), BF16)],
        compiler_params=pltpu.CompilerParams(
            dimension_semantics=("parallel", "parallel"), vmem_limit_bytes=VMEM_LIMIT),
        name="ffn",
    )(x, x, x, fn, wup, cw, cb, wd)


def _block_diag_dense(w):
    nb, bs, _ = w.shape
    eye = jnp.eye(nb, dtype=w.dtype)
    return (eye[:, None, :, None] * w[:, :, None, :]).reshape(nb * bs, nb * bs)


def _layer(x, mem, pos3, attn_norm, w_in, lru_conv_w, lru_conv_b, lru_w_a, lru_b_a, lru_w_i, lru_b_i,
           lru_lambda, q_a_norm, w_uq, kv_a_norm, w_ukv, mla_q_norm, mla_k_norm, lru_out_norm,
           mla_out_norm, w_out, mem_attn_norm, mem_norm, w_mem_q, w_mem_kv, mem_q_norm, mem_k_norm,
           w_mem_o, ffn_norm, w_up, ffn_conv_w, ffn_conv_b, w_down):
    D = x.shape[-1]
    H = MLA_HEADS
    row = lambda a: a.reshape(1, -1)
    off_kr = 2 * LRU_WIDTH + Q_LORA + KV_LORA
    pad_cols = lambda a, lo, hi: jnp.pad(a, ((0, 0),) * (a.ndim - 1) + ((lo, hi),))

    win = jnp.concatenate([w_in[:, :off_kr], pad_cols(w_in[:, off_kr:], QK_NOPE, HEAD_PAD - QK_HEAD)],
                          axis=1).astype(BF16)
    wuq = w_uq.reshape(Q_LORA, H, QK_HEAD).transpose(1, 2, 0).astype(BF16)
    wkv = w_ukv.reshape(KV_LORA, H, QK_NOPE + V_DIM).transpose(1, 0, 2)
    wuk = wkv[:, :, :QK_NOPE].transpose(0, 2, 1).astype(BF16)
    wuv = wkv[:, :, QK_NOPE:].transpose(0, 2, 1).astype(BF16)
    gq = pad_cols(row(mla_q_norm), 0, HEAD_PAD - QK_HEAD).reshape(HEAD_PAD, 1)
    gk = pad_cols(row(mla_k_norm), 0, HEAD_PAD - QK_HEAD).reshape(HEAD_PAD, 1)
    inv = (ROPE_THETA ** (-jnp.arange(0, QK_ROPE, 2, dtype=F32) / QK_ROPE)).reshape(QK_ROPE // 2, 1)
    score_bound = math.sqrt(QK_HEAD) * LOG2E * jnp.max(jnp.abs(mla_q_norm)) * jnp.max(jnp.abs(mla_k_norm))

    xr, yg, q, k, v = _proj_call(x, pos3, row(attn_norm), win, row(q_a_norm), row(kv_a_norm),
                                 wuq, wuk, wuv, gq, gk, inv, score_bound.reshape(1, 1))

    per_tile = MXU_DIM // (LRU_WIDTH // LRU_BLOCKS)
    tiles = lambda w: jax.vmap(jax.vmap(_block_diag_dense))(
        w.reshape(2, LRU_BLOCKS // per_tile, per_tile, *w.shape[2:])).astype(BF16)
    wa, wi = tiles(lru_w_a), tiles(lru_w_i)
    r3 = lambda a: a.reshape(2, 1, LRU_WIDTH)
    h2 = _lru_call(xr, lru_conv_w, r3(lru_conv_b), wa, r3(lru_b_a), wi, r3(lru_b_i), r3(lru_lambda))

    bounded = (2.0 * score_bound <= EXP2_SAFE_RANGE).astype(jnp.int32).reshape(1)
    mla = _attn_call(bounded, q, k, v)

    wo = w_out.astype(BF16)
    x2 = _mix_call(x, h2, yg, mla, row(lru_out_norm), mla_out_norm.reshape(-1, 1), wo[:LRU_WIDTH], wo[LRU_WIDTH:],
                   row(mem_attn_norm), w_mem_q.astype(BF16), row(mem_q_norm),
                   mem, row(mem_norm), w_mem_kv.astype(BF16), row(mem_k_norm), w_mem_o.astype(BF16))

    return _ffn_call(x2, row(ffn_norm), w_up.astype(BF16), ffn_conv_w, row(ffn_conv_b), w_down.astype(BF16))


def kernel(x, mem, positions, attn_norm, w_in, lru_conv_w, lru_conv_b, lru_w_a, lru_b_a, lru_w_i, lru_b_i, lru_lambda, q_a_norm, w_uq, kv_a_norm, w_ukv, mla_q_norm, mla_k_norm, lru_out_norm, mla_out_norm, w_out, mem_attn_norm, mem_norm, w_mem_q, w_mem_kv, mem_q_norm, mem_k_norm, w_mem_o, ffn_norm, w_up, ffn_conv_w, ffn_conv_b, w_down):
    pos3 = positions[:, None, :]
    params = (attn_norm, w_in, lru_conv_w, lru_conv_b, lru_w_a, lru_b_a, lru_w_i, lru_b_i, lru_lambda,
              q_a_norm, w_uq, kv_a_norm, w_ukv, mla_q_norm, mla_k_norm, lru_out_norm, mla_out_norm, w_out,
              mem_attn_norm, mem_norm, w_mem_q, w_mem_kv, mem_q_norm, mem_k_norm, w_mem_o, ffn_norm, w_up,
              ffn_conv_w, ffn_conv_b, w_down)
    for l in range(attn_norm.shape[0]):
        x = _layer(x, mem, pos3, *[p[l] for p in params])
    return x
```
